```python
import jax
import jax.numpy as jnp
from jax import lax
import numpy as np


D_MODEL = 2048
BATCH = 8
SEQ = 8192
DEPTH = 4

CTX_LEN = 256
GRID_W = 64
MIX_WIDTH = D_MODEL
RET_WIDTH = MIX_WIDTH // 2
CONV_WIDTH = MIX_WIDTH - RET_WIDTH
RET_HEAD_DIM = 128
RET_HEADS = RET_WIDTH // RET_HEAD_DIM
RET_CHUNK = 128
CONV_K = 31
FFN_K = 3
D_FF = ((8 * D_MODEL // 3 + 255) // 256) * 256
IN_WIDTH = 4 * RET_WIDTH + 2 * CONV_WIDTH
N_MOD = 6
ROPE_THETA = 10000.0
EPS = 1e-6

kernel_name = "hybrid_retention_conformer_dit"


def rmsnorm(x, g):
    xf = x.astype(jnp.float32)
    xf = xf * lax.rsqrt(jnp.mean(xf * xf, axis=-1, keepdims=True) + EPS)
    return (xf * g.astype(jnp.float32)).astype(x.dtype)


def layernorm(x, g, b):
    xf = x.astype(jnp.float32)
    mu = jnp.mean(xf, axis=-1, keepdims=True)
    var = jnp.mean(jnp.square(xf - mu), axis=-1, keepdims=True)
    y = (xf - mu) * lax.rsqrt(var + EPS) * g.astype(jnp.float32) + b.astype(jnp.float32)
    return y.astype(x.dtype)


def modulate(h, shift, scale):
    return h * (1 + scale) + shift


def rope_tables(pos, dim):
    inv_freq = 1.0 / (ROPE_THETA ** (jnp.arange(0, dim // 2, dtype=jnp.float32) / (dim // 2)))
    ang = pos.astype(jnp.float32)[:, None] * inv_freq[None, :]
    return jnp.cos(ang)[:, None, :], jnp.sin(ang)[:, None, :]


def apply_rope(x, cos, sin):
    half = x.shape[-1] // 2
    x1, x2 = x[..., :half], x[..., half:]
    cos = cos.astype(x.dtype)
    sin = sin.astype(x.dtype)
    return jnp.concatenate([x1 * cos - x2 * sin, x1 * sin + x2 * cos], axis=-1)


def rope_2d(x, row_cs, col_cs):
    half = x.shape[-1] // 2
    return jnp.concatenate([apply_rope(x[..., :half], *row_cs), apply_rope(x[..., half:], *col_cs)], axis=-1)


def dwconv1d(x, w, b):
    y = lax.conv_general_dilated(x, w[:, None, :].astype(x.dtype), window_strides=(1,), padding='SAME',
                                 dimension_numbers=('NWC', 'WIO', 'NWC'), feature_group_count=x.shape[-1])
    return y + b.astype(x.dtype)


def dwconv2d(x, w, b):
    y = lax.conv_general_dilated(x, w[:, :, None, :].astype(x.dtype), window_strides=(1, 1), padding='SAME',
                                 dimension_numbers=('NHWC', 'HWIO', 'NHWC'), feature_group_count=x.shape[-1])
    return y + b.astype(x.dtype)


def retention_chunkwise(q, k, v, log_gamma, s0):
    bsz, nh, length, dh = q.shape
    n_chunks = length // RET_CHUNK
    q = q.astype(jnp.float32)
    k = k.astype(jnp.float32) * (dh ** -0.5)
    v = v.astype(jnp.float32)
    to_chunks = lambda t: jnp.moveaxis(t.reshape(bsz, nh, n_chunks, RET_CHUNK, dh), 2, 0)
    idx = jnp.arange(RET_CHUNK, dtype=jnp.float32)
    diff = idx[:, None] - idx[None, :]
    lg = log_gamma[:, None, None]
    inner_decay = jnp.where(diff >= 0, jnp.exp(lg * jnp.maximum(diff, 0.0)), 0.0)
    q_decay = jnp.exp(log_gamma[:, None] * (idx + 1.0))[:, :, None]
    k_decay = jnp.exp(log_gamma[:, None] * (RET_CHUNK - 1.0 - idx))[:, :, None]
    chunk_decay = jnp.exp(log_gamma * RET_CHUNK)[:, None, None]

    def step(s, qkv):
        qc, kc, vc = qkv
        scores = jnp.einsum('bhnd,bhmd->bhnm', qc, kc) * inner_decay
        o = jnp.einsum('bhnm,bhmd->bhnd', scores, vc) + jnp.einsum('bhnd,bhde->bhne', qc, s) * q_decay
        s = s * chunk_decay + jnp.einsum('bhmd,bhme->bhde', kc * k_decay, vc)
        return s, o

    s_final, o = lax.scan(step, s0.astype(jnp.float32), (to_chunks(q), to_chunks(k), to_chunks(v)))
    o = jnp.moveaxis(o, 0, 2).reshape(bsz, nh, length, dh)
    return o, s_final


def bidir_retention(q, k, v, lg_f, lg_b, s0_f, s0_b):
    o_f, s_f = retention_chunkwise(q, k, v, lg_f, s0_f)
    rev = lambda t: jnp.flip(t, axis=2)
    o_b, s_b = retention_chunkwise(rev(q), rev(k), rev(v), lg_b, s0_b)
    return o_f + rev(o_b), s_f, s_b


def retention_output(o, g):
    o = o * lax.rsqrt(jnp.mean(o * o, axis=-1, keepdims=True) + EPS)
    bsz, nh, length, dh = o.shape
    o = jnp.transpose(o, (0, 2, 1, 3)).reshape(bsz, length, nh * dh)
    return (jax.nn.silu(g.astype(jnp.float32)) * o).astype(g.dtype)


def to_heads(t):
    bsz, length, _ = t.shape
    return t.reshape(bsz, length, RET_HEADS, RET_HEAD_DIM)


def heads_first(t):
    return jnp.transpose(t, (0, 2, 1, 3))


def conformer_conv(a, b, w, bias, ln_g, ln_b):
    u = a * jax.nn.sigmoid(b)
    u = dwconv1d(u, w, bias)
    return jax.nn.silu(layernorm(u, ln_g, ln_b))


def conv_ffn(h, w_up, w_down, conv):
    gate, val = jnp.split(h @ w_up, 2, axis=-1)
    return (jax.nn.silu(conv(gate)) * val) @ w_down


def split_mixer_proj(p):
    return jnp.split(p, [RET_WIDTH, 2 * RET_WIDTH, 3 * RET_WIDTH, 4 * RET_WIDTH, 4 * RET_WIDTH + CONV_WIDTH], axis=-1)


def _fwd_setup_inputs(seed: int = 0) -> dict:
    key = jax.random.key(seed)
    ks = jax.random.split(key, 24)
    f32 = jnp.float32
    nrm = lambda k, shape, s: s * jax.random.normal(k, shape, f32)
    decay_base = jnp.log(jnp.exp2(5.0 + jnp.arange(RET_HEADS, dtype=f32)) - 1.0)
    return {
        'x': nrm(ks[0], (BATCH, SEQ, D_MODEL), 1.0),
        'c': nrm(ks[1], (BATCH, D_MODEL), 1.0),
        'ctx': nrm(ks[2], (BATCH, CTX_LEN, D_MODEL), 1.0),
        'c_ctx': nrm(ks[3], (D_MODEL,), 1.0),
        'w_mod': nrm(ks[4], (DEPTH, D_MODEL, N_MOD * D_MODEL), 0.5 * D_MODEL ** -0.5),
        'b_mod': nrm(ks[5], (DEPTH, N_MOD * D_MODEL), 0.02),
        'norm1_g': 1.0 + nrm(ks[6], (DEPTH, D_MODEL), 0.02),
        'norm2_g': 1.0 + nrm(ks[7], (DEPTH, D_MODEL), 0.02),
        'w_in': nrm(ks[8], (DEPTH, D_MODEL, IN_WIDTH), D_MODEL ** -0.5),
        'ret_decay_f': decay_base + nrm(ks[9], (DEPTH, RET_HEADS), 0.1),
        'ret_decay_b': decay_base + nrm(ks[10], (DEPTH, RET_HEADS), 0.1),
        'conv_dw_w': nrm(ks[11], (DEPTH, CONV_K, CONV_WIDTH), CONV_K ** -0.5),
        'conv_dw_b': nrm(ks[12], (DEPTH, CONV_WIDTH), 0.02),
        'conv_ln_g': 1.0 + nrm(ks[13], (DEPTH, CONV_WIDTH), 0.02),
        'conv_ln_b': nrm(ks[14], (DEPTH, CONV_WIDTH), 0.02),
        'w_out': nrm(ks[15], (DEPTH, MIX_WIDTH, D_MODEL), MIX_WIDTH ** -0.5),
        'ffn_w_up': nrm(ks[16], (DEPTH, D_MODEL, 2 * D_FF), D_MODEL ** -0.5),
        'ffn_dw_w': nrm(ks[17], (DEPTH, FFN_K, FFN_K, D_FF), 1.0 / FFN_K),
        'ffn_dw_b': nrm(ks[18], (DEPTH, D_FF), 0.02),
        'ffn_w_down': nrm(ks[19], (DEPTH, D_FF, D_MODEL), D_FF ** -0.5),
        'final_norm_g': 1.0 + nrm(ks[20], (D_MODEL,), 0.02),
    }


def _fwd_reference(x, c, ctx, c_ctx, w_mod, b_mod, norm1_g, norm2_g, w_in, ret_decay_f, ret_decay_b,
              conv_dw_w, conv_dw_b, conv_ln_g, conv_ln_b, w_out, ffn_w_up, ffn_dw_w, ffn_dw_b,
              ffn_w_down, final_norm_g):
    bsz, length, _ = x.shape
    rows = length // GRID_W
    t = jnp.arange(length)
    row_cs = rope_tables(t // GRID_W, RET_HEAD_DIM // 2)
    col_cs = rope_tables(t % GRID_W, RET_HEAD_DIM // 2)
    silu_c = jax.nn.silu(c)
    silu_cc = jax.nn.silu(c_ctx)[None, :]
    zero_state = jnp.zeros((bsz, RET_HEADS, RET_HEAD_DIM, RET_HEAD_DIM), jnp.float32)
    xc = ctx
    for l in range(DEPTH):
        last = l == DEPTH - 1
        mod = (silu_c @ w_mod[l] + b_mod[l])[:, None, :]
        mod_c = (silu_cc @ w_mod[l] + b_mod[l])[:, None, :]
        sh1, sc1, g1, sh2, sc2, g2 = jnp.split(mod, N_MOD, axis=-1)
        csh1, csc1, cg1, csh2, csc2, cg2 = jnp.split(mod_c, N_MOD, axis=-1)

        hx = modulate(rmsnorm(x, norm1_g[l]), sh1, sc1)
        hc = modulate(rmsnorm(xc, norm1_g[l]), csh1, csc1)
        q, k, v, g, a, bg = split_mixer_proj(hx @ w_in[l])
        cq, ck, cv, cgt, ca, cb = split_mixer_proj(hc @ w_in[l])
        lg_f = jax.nn.log_sigmoid(ret_decay_f[l].astype(jnp.float32))
        lg_b = jax.nn.log_sigmoid(ret_decay_b[l].astype(jnp.float32))

        co, s_f, s_b = bidir_retention(heads_first(to_heads(cq)), heads_first(to_heads(ck)),
                                       heads_first(to_heads(cv)), lg_f, lg_b, zero_state, zero_state)
        q = heads_first(rope_2d(to_heads(q), row_cs, col_cs))
        k = heads_first(rope_2d(to_heads(k), row_cs, col_cs))
        o, _, _ = bidir_retention(q, k, heads_first(to_heads(v)), lg_f, lg_b, s_f, s_b)

        mix = jnp.concatenate([
            retention_output(o, g),
            conformer_conv(a, bg, conv_dw_w[l], conv_dw_b[l], conv_ln_g[l], conv_ln_b[l]),
        ], axis=-1)
        x = x + g1 * (mix @ w_out[l])

        lat_conv = lambda u: dwconv2d(u.reshape(bsz, rows, GRID_W, D_FF), ffn_dw_w[l], ffn_dw_b[l]).reshape(bsz, length, D_FF)
        x = x + g2 * conv_ffn(modulate(rmsnorm(x, norm2_g[l]), sh2, sc2), ffn_w_up[l], ffn_w_down[l], lat_conv)

        if not last:
            cmix = jnp.concatenate([
                retention_output(co, cgt),
                conformer_conv(ca, cb, conv_dw_w[l], conv_dw_b[l], conv_ln_g[l], conv_ln_b[l]),
            ], axis=-1)
            xc = xc + cg1 * (cmix @ w_out[l])
            ctx_conv = lambda u: dwconv1d(u, ffn_dw_w[l][FFN_K // 2], ffn_dw_b[l])
            xc = xc + cg2 * conv_ffn(modulate(rmsnorm(xc, norm2_g[l]), csh2, csc2), ffn_w_up[l], ffn_w_down[l], ctx_conv)

    return rmsnorm(x, final_norm_g)


import jax as _jax
import jax.numpy as _jnp

TWIN_FORMAT = 'train_step'
FWD_PARAMS = ['x', 'c', 'ctx', 'c_ctx', 'w_mod', 'b_mod', 'norm1_g', 'norm2_g', 'w_in', 'ret_decay_f', 'ret_decay_b', 'conv_dw_w', 'conv_dw_b', 'conv_ln_g', 'conv_ln_b', 'w_out', 'ffn_w_up', 'ffn_dw_w', 'ffn_dw_b', 'ffn_w_down', 'final_norm_g']
TWIN_WEIGHTS = ['c_ctx', 'w_mod', 'b_mod', 'norm1_g', 'norm2_g', 'w_in', 'ret_decay_f', 'ret_decay_b', 'conv_dw_w', 'conv_dw_b', 'conv_ln_g', 'conv_ln_b', 'w_out', 'ffn_w_up', 'ffn_dw_w', 'ffn_dw_b', 'ffn_w_down', 'final_norm_g']
TWIN_DIFF_INPUT = 'x'
TWIN_INPUTS = ['x', 'c', 'ctx', 'c_ctx', 'w_mod', 'b_mod', 'norm1_g', 'norm2_g', 'w_in', 'ret_decay_f', 'ret_decay_b', 'conv_dw_w', 'conv_dw_b', 'conv_ln_g', 'conv_ln_b', 'w_out', 'ffn_w_up', 'ffn_dw_w', 'ffn_dw_b', 'ffn_w_down', 'final_norm_g', 'loss_target', 'm_c_ctx', 'm_w_mod', 'm_b_mod', 'm_norm1_g', 'm_norm2_g', 'm_w_in', 'm_ret_decay_f', 'm_ret_decay_b', 'm_conv_dw_w', 'm_conv_dw_b', 'm_conv_ln_g', 'm_conv_ln_b', 'm_w_out', 'm_ffn_w_up', 'm_ffn_dw_w', 'm_ffn_dw_b', 'm_ffn_w_down', 'm_final_norm_g', 'v_c_ctx', 'v_w_mod', 'v_b_mod', 'v_norm1_g', 'v_norm2_g', 'v_w_in', 'v_ret_decay_f', 'v_ret_decay_b', 'v_conv_dw_w', 'v_conv_dw_b', 'v_conv_ln_g', 'v_conv_ln_b', 'v_w_out', 'v_ffn_w_up', 'v_ffn_dw_w', 'v_ffn_dw_b', 'v_ffn_w_down', 'v_final_norm_g']
TWIN_OUTPUTS = ['loss', 'grad_x', 'grad_c_ctx', 'grad_w_mod', 'grad_b_mod', 'grad_norm1_g', 'grad_norm2_g', 'grad_w_in', 'grad_ret_decay_f', 'grad_ret_decay_b', 'grad_conv_dw_w', 'grad_conv_dw_b', 'grad_conv_ln_g', 'grad_conv_ln_b', 'grad_w_out', 'grad_ffn_w_up', 'grad_ffn_dw_w', 'grad_ffn_dw_b', 'grad_ffn_w_down', 'grad_final_norm_g', 'delta_c_ctx', 'delta_w_mod', 'delta_b_mod', 'delta_norm1_g', 'delta_norm2_g', 'delta_w_in', 'delta_ret_decay_f', 'delta_ret_decay_b', 'delta_conv_dw_w', 'delta_conv_dw_b', 'delta_conv_ln_g', 'delta_conv_ln_b', 'delta_w_out', 'delta_ffn_w_up', 'delta_ffn_dw_w', 'delta_ffn_dw_b', 'delta_ffn_w_down', 'delta_final_norm_g', 'new_m_c_ctx', 'new_m_w_mod', 'new_m_b_mod', 'new_m_norm1_g', 'new_m_norm2_g', 'new_m_w_in', 'new_m_ret_decay_f', 'new_m_ret_decay_b', 'new_m_conv_dw_w', 'new_m_conv_dw_b', 'new_m_conv_ln_g', 'new_m_conv_ln_b', 'new_m_w_out', 'new_m_ffn_w_up', 'new_m_ffn_dw_w', 'new_m_ffn_dw_b', 'new_m_ffn_w_down', 'new_m_final_norm_g', 'new_v_c_ctx', 'new_v_w_mod', 'new_v_b_mod', 'new_v_norm1_g', 'new_v_norm2_g', 'new_v_w_in', 'new_v_ret_decay_f', 'new_v_ret_decay_b', 'new_v_conv_dw_w', 'new_v_conv_dw_b', 'new_v_conv_ln_g', 'new_v_conv_ln_b', 'new_v_w_out', 'new_v_ffn_w_up', 'new_v_ffn_dw_w', 'new_v_ffn_dw_b', 'new_v_ffn_w_down', 'new_v_final_norm_g']
TWIN_LEAF_KINDS = {'loss': 'loss', 'grad_x': 'grad_x', 'grad_c_ctx': 'grad_w', 'grad_w_mod': 'grad_w', 'grad_b_mod': 'grad_w', 'grad_norm1_g': 'grad_w', 'grad_norm2_g': 'grad_w', 'grad_w_in': 'grad_w', 'grad_ret_decay_f': 'grad_w', 'grad_ret_decay_b': 'grad_w', 'grad_conv_dw_w': 'grad_w', 'grad_conv_dw_b': 'grad_w', 'grad_conv_ln_g': 'grad_w', 'grad_conv_ln_b': 'grad_w', 'grad_w_out': 'grad_w', 'grad_ffn_w_up': 'grad_w', 'grad_ffn_dw_w': 'grad_w', 'grad_ffn_dw_b': 'grad_w', 'grad_ffn_w_down': 'grad_w', 'grad_final_norm_g': 'grad_w', 'delta_c_ctx': 'delta_w', 'delta_w_mod': 'delta_w', 'delta_b_mod': 'delta_w', 'delta_norm1_g': 'delta_w', 'delta_norm2_g': 'delta_w', 'delta_w_in': 'delta_w', 'delta_ret_decay_f': 'delta_w', 'delta_ret_decay_b': 'delta_w', 'delta_conv_dw_w': 'delta_w', 'delta_conv_dw_b': 'delta_w', 'delta_conv_ln_g': 'delta_w', 'delta_conv_ln_b': 'delta_w', 'delta_w_out': 'delta_w', 'delta_ffn_w_up': 'delta_w', 'delta_ffn_dw_w': 'delta_w', 'delta_ffn_dw_b': 'delta_w', 'delta_ffn_w_down': 'delta_w', 'delta_final_norm_g': 'delta_w', 'new_m_c_ctx': 'new_m', 'new_m_w_mod': 'new_m', 'new_m_b_mod': 'new_m', 'new_m_norm1_g': 'new_m', 'new_m_norm2_g': 'new_m', 'new_m_w_in': 'new_m', 'new_m_ret_decay_f': 'new_m', 'new_m_ret_decay_b': 'new_m', 'new_m_conv_dw_w': 'new_m', 'new_m_conv_dw_b': 'new_m', 'new_m_conv_ln_g': 'new_m', 'new_m_conv_ln_b': 'new_m', 'new_m_w_out': 'new_m', 'new_m_ffn_w_up': 'new_m', 'new_m_ffn_dw_w': 'new_m', 'new_m_ffn_dw_b': 'new_m', 'new_m_ffn_w_down': 'new_m', 'new_m_final_norm_g': 'new_m', 'new_v_c_ctx': 'new_v', 'new_v_w_mod': 'new_v', 'new_v_b_mod': 'new_v', 'new_v_norm1_g': 'new_v', 'new_v_norm2_g': 'new_v', 'new_v_w_in': 'new_v', 'new_v_ret_decay_f': 'new_v', 'new_v_ret_decay_b': 'new_v', 'new_v_conv_dw_w': 'new_v', 'new_v_conv_dw_b': 'new_v', 'new_v_conv_ln_g': 'new_v', 'new_v_conv_ln_b': 'new_v', 'new_v_w_out': 'new_v', 'new_v_ffn_w_up': 'new_v', 'new_v_ffn_dw_w': 'new_v', 'new_v_ffn_dw_b': 'new_v', 'new_v_ffn_w_down': 'new_v', 'new_v_final_norm_g': 'new_v'}


def _forward(args):
    return _fwd_reference(*[args[k] for k in FWD_PARAMS])


def _output_shape():
    def fwd():
        inp = _fwd_setup_inputs(0)
        return _fwd_reference(*[inp[k] for k in FWD_PARAMS])
    out = _jax.eval_shape(fwd)
    return out.shape, out.dtype

N_MICROBATCH = 1
ADAM_LR = 0.001
ADAM_B1 = 0.9
ADAM_B2 = 0.999
ADAM_EPS = 1e-08
ADAM_WD = 0.01
ADAM_STEP = 10
PER_EXAMPLE_BATCH_AXIS = {'x': 0, 'c': 0, 'ctx': 0, 'loss_target': 0}
SHARED_INPUTS = []
_WEIGHT_DTYPES = {'c_ctx': _jnp.float32, 'w_mod': _jnp.float32, 'b_mod': _jnp.float32, 'norm1_g': _jnp.float32, 'norm2_g': _jnp.float32, 'w_in': _jnp.float32, 'ret_decay_f': _jnp.float32, 'ret_decay_b': _jnp.float32, 'conv_dw_w': _jnp.float32, 'conv_dw_b': _jnp.float32, 'conv_ln_g': _jnp.float32, 'conv_ln_b': _jnp.float32, 'w_out': _jnp.float32, 'ffn_w_up': _jnp.float32, 'ffn_dw_w': _jnp.float32, 'ffn_dw_b': _jnp.float32, 'ffn_w_down': _jnp.float32, 'final_norm_g': _jnp.float32}
MOMENT_SCALE = {'c_ctx': 2.810799e-02, 'w_mod': 4.073566e-02, 'b_mod': 6.944106e-02, 'norm1_g': 3.725870e-02, 'norm2_g': 3.947309e-02, 'w_in': 2.518456e-02, 'ret_decay_f': 1.315204e-01, 'ret_decay_b': 6.181970e-02, 'conv_dw_w': 2.356993e-02, 'conv_dw_b': 4.347944e-02, 'conv_ln_g': 2.778034e-02, 'conv_ln_b': 2.612626e-02, 'w_out': 2.423640e-02, 'ffn_w_up': 1.747115e-02, 'ffn_dw_w': 1.763426e-02, 'ffn_dw_b': 1.501796e-02, 'ffn_w_down': 2.860729e-02, 'final_norm_g': 3.200875e+01}


def _to_microbatches(a, axis):
    t = _jnp.moveaxis(a, axis, 0)
    t = t.reshape((N_MICROBATCH, t.shape[0] // N_MICROBATCH) + t.shape[1:])
    return _jnp.moveaxis(t, 1, axis + 1)


def setup_inputs(seed: int = 0) -> dict:
    inp = _fwd_setup_inputs(seed)
    key = _jax.random.fold_in(_jax.random.key(seed), 7919)
    shape, _ = _output_shape()
    out = dict(inp)
    out["loss_target"] = _jax.random.normal(_jax.random.fold_in(key, 0), shape, _jnp.float32)
    for i, name in enumerate(TWIN_WEIGHTS):
        w = inp[name].astype(_jnp.float32)
        if MOMENT_SCALE is None:
            s = _jnp.sqrt(_jnp.mean(_jnp.square(w)) + 1e-30)
        else:
            s = MOMENT_SCALE[name]
        km, kv = _jax.random.split(_jax.random.fold_in(key, i + 1))
        out[name] = w
        out["m_" + name] = s * _jax.random.normal(km, w.shape, _jnp.float32)
        out["v_" + name] = (s * s) * _jax.random.uniform(kv, w.shape, _jnp.float32, 0.5, 1.5)
    if N_MICROBATCH > 1:
        for name, axis in PER_EXAMPLE_BATCH_AXIS.items():
            out[name] = _to_microbatches(out[name], axis)
    return {'x': out['x'], 'c': out['c'], 'ctx': out['ctx'], 'c_ctx': out['c_ctx'], 'w_mod': out['w_mod'], 'b_mod': out['b_mod'], 'norm1_g': out['norm1_g'], 'norm2_g': out['norm2_g'], 'w_in': out['w_in'], 'ret_decay_f': out['ret_decay_f'], 'ret_decay_b': out['ret_decay_b'], 'conv_dw_w': out['conv_dw_w'], 'conv_dw_b': out['conv_dw_b'], 'conv_ln_g': out['conv_ln_g'], 'conv_ln_b': out['conv_ln_b'], 'w_out': out['w_out'], 'ffn_w_up': out['ffn_w_up'], 'ffn_dw_w': out['ffn_dw_w'], 'ffn_dw_b': out['ffn_dw_b'], 'ffn_w_down': out['ffn_w_down'], 'final_norm_g': out['final_norm_g'], 'loss_target': out['loss_target'], 'm_c_ctx': out['m_c_ctx'], 'm_w_mod': out['m_w_mod'], 'm_b_mod': out['m_b_mod'], 'm_norm1_g': out['m_norm1_g'], 'm_norm2_g': out['m_norm2_g'], 'm_w_in': out['m_w_in'], 'm_ret_decay_f': out['m_ret_decay_f'], 'm_ret_decay_b': out['m_ret_decay_b'], 'm_conv_dw_w': out['m_conv_dw_w'], 'm_conv_dw_b': out['m_conv_dw_b'], 'm_conv_ln_g': out['m_conv_ln_g'], 'm_conv_ln_b': out['m_conv_ln_b'], 'm_w_out': out['m_w_out'], 'm_ffn_w_up': out['m_ffn_w_up'], 'm_ffn_dw_w': out['m_ffn_dw_w'], 'm_ffn_dw_b': out['m_ffn_dw_b'], 'm_ffn_w_down': out['m_ffn_w_down'], 'm_final_norm_g': out['m_final_norm_g'], 'v_c_ctx': out['v_c_ctx'], 'v_w_mod': out['v_w_mod'], 'v_b_mod': out['v_b_mod'], 'v_norm1_g': out['v_norm1_g'], 'v_norm2_g': out['v_norm2_g'], 'v_w_in': out['v_w_in'], 'v_ret_decay_f': out['v_ret_decay_f'], 'v_ret_decay_b': out['v_ret_decay_b'], 'v_conv_dw_w': out['v_conv_dw_w'], 'v_conv_dw_b': out['v_conv_dw_b'], 'v_conv_ln_g': out['v_conv_ln_g'], 'v_conv_ln_b': out['v_conv_ln_b'], 'v_w_out': out['v_w_out'], 'v_ffn_w_up': out['v_ffn_w_up'], 'v_ffn_dw_w': out['v_ffn_dw_w'], 'v_ffn_dw_b': out['v_ffn_dw_b'], 'v_ffn_w_down': out['v_ffn_w_down'], 'v_final_norm_g': out['v_final_norm_g']}


def _loss(weights, diff, rest, loss_target):
    with _jax.named_scope("forward"):
        args = {**rest, TWIN_DIFF_INPUT: diff, **{k: w.astype(_WEIGHT_DTYPES[k]) for k, w in weights.items()}}
        y = _forward(args)
    with _jax.named_scope("loss_head"):
        err = _jnp.square(y.astype(_jnp.float32) - loss_target)
        return 0.5 * _jnp.sum(_jnp.mean(err, axis=-1)) if err.ndim else 0.5 * err


def _adamw(w, g, m, v):
    m = ADAM_B1 * m + (1.0 - ADAM_B1) * g
    v = ADAM_B2 * v + (1.0 - ADAM_B2) * _jnp.square(g)
    m_hat = m / (1.0 - ADAM_B1 ** ADAM_STEP)
    v_hat = v / (1.0 - ADAM_B2 ** ADAM_STEP)
    delta = -ADAM_LR * (m_hat / (_jnp.sqrt(v_hat) + ADAM_EPS) + ADAM_WD * w)
    return delta, m, v


def reference(x, c, ctx, c_ctx, w_mod, b_mod, norm1_g, norm2_g, w_in, ret_decay_f, ret_decay_b, conv_dw_w, conv_dw_b, conv_ln_g, conv_ln_b, w_out, ffn_w_up, ffn_dw_w, ffn_dw_b, ffn_w_down, final_norm_g, loss_target, m_c_ctx, m_w_mod, m_b_mod, m_norm1_g, m_norm2_g, m_w_in, m_ret_decay_f, m_ret_decay_b, m_conv_dw_w, m_conv_dw_b, m_conv_ln_g, m_conv_ln_b, m_w_out, m_ffn_w_up, m_ffn_dw_w, m_ffn_dw_b, m_ffn_w_down, m_final_norm_g, v_c_ctx, v_w_mod, v_b_mod, v_norm1_g, v_norm2_g, v_w_in, v_ret_decay_f, v_ret_decay_b, v_conv_dw_w, v_conv_dw_b, v_conv_ln_g, v_conv_ln_b, v_w_out, v_ffn_w_up, v_ffn_dw_w, v_ffn_dw_b, v_ffn_w_down, v_final_norm_g):
    given = dict(x=x, c=c, ctx=ctx, c_ctx=c_ctx, w_mod=w_mod, b_mod=b_mod, norm1_g=norm1_g, norm2_g=norm2_g, w_in=w_in, ret_decay_f=ret_decay_f, ret_decay_b=ret_decay_b, conv_dw_w=conv_dw_w, conv_dw_b=conv_dw_b, conv_ln_g=conv_ln_g, conv_ln_b=conv_ln_b, w_out=w_out, ffn_w_up=ffn_w_up, ffn_dw_w=ffn_dw_w, ffn_dw_b=ffn_dw_b, ffn_w_down=ffn_w_down, final_norm_g=final_norm_g, loss_target=loss_target, m_c_ctx=m_c_ctx, m_w_mod=m_w_mod, m_b_mod=m_b_mod, m_norm1_g=m_norm1_g, m_norm2_g=m_norm2_g, m_w_in=m_w_in, m_ret_decay_f=m_ret_decay_f, m_ret_decay_b=m_ret_decay_b, m_conv_dw_w=m_conv_dw_w, m_conv_dw_b=m_conv_dw_b, m_conv_ln_g=m_conv_ln_g, m_conv_ln_b=m_conv_ln_b, m_w_out=m_w_out, m_ffn_w_up=m_ffn_w_up, m_ffn_dw_w=m_ffn_dw_w, m_ffn_dw_b=m_ffn_dw_b, m_ffn_w_down=m_ffn_w_down, m_final_norm_g=m_final_norm_g, v_c_ctx=v_c_ctx, v_w_mod=v_w_mod, v_b_mod=v_b_mod, v_norm1_g=v_norm1_g, v_norm2_g=v_norm2_g, v_w_in=v_w_in, v_ret_decay_f=v_ret_decay_f, v_ret_decay_b=v_ret_decay_b, v_conv_dw_w=v_conv_dw_w, v_conv_dw_b=v_conv_dw_b, v_conv_ln_g=v_conv_ln_g, v_conv_ln_b=v_conv_ln_b, v_w_out=v_w_out, v_ffn_w_up=v_ffn_w_up, v_ffn_dw_w=v_ffn_dw_w, v_ffn_dw_b=v_ffn_dw_b, v_ffn_w_down=v_ffn_w_down, v_final_norm_g=v_final_norm_g)
    weights = {n: given[n] for n in TWIN_WEIGHTS}
    shared = {n: given[n] for n in SHARED_INPUTS}
    per_example = {n: given[n] for n in ['x', 'c', 'ctx']}
    grad_fn = _jax.value_and_grad(_loss, argnums=(0, 1))

    def one_microbatch(ex, loss_target):
        ex = dict(ex)
        diff = ex.pop(TWIN_DIFF_INPUT)
        return grad_fn(weights, diff, {**shared, **ex}, loss_target)

    if N_MICROBATCH == 1:
        loss, (grad_w, grad_x) = one_microbatch(per_example, given["loss_target"])
    else:
        def body(carry, xs):
            loss_sum, grad_sum = carry
            l_k, (gw_k, gx_k) = one_microbatch(xs[0], xs[1])
            with _jax.named_scope("update"):
                return (loss_sum + l_k, _jax.tree.map(_jnp.add, grad_sum, gw_k)), gx_k

        init = (_jnp.zeros((), _jnp.float32), _jax.tree.map(_jnp.zeros_like, weights))
        (loss, grad_w), grad_x = _jax.lax.scan(body, init, (per_example, given["loss_target"]))
    with _jax.named_scope("update"):
        delta_w, new_m, new_v = {}, {}, {}
        for n in TWIN_WEIGHTS:
            delta_w[n], new_m[n], new_v[n] = _adamw(weights[n], grad_w[n], given["m_" + n], given["v_" + n])
    return (loss, grad_x, *[grad_w[n] for n in TWIN_WEIGHTS], *[delta_w[n] for n in TWIN_WEIGHTS],
            *[new_m[n] for n in TWIN_WEIGHTS], *[new_v[n] for n in TWIN_WEIGHTS])
```

```python
import functools
import math

import jax
import jax.numpy as jnp
from jax import lax
from jax.experimental import pallas as pl
from jax.experimental.pallas import tpu as pltpu

F32 = jnp.float32
BF16 = jnp.bfloat16
EPS = 1e-6
N_DEV = 8
ROW_TILE = 256
RET_CHUNK = 128
GRID_W = 64
CONV_K = 31
CONV_HALO = 16
FFN_HALO = 128
ROPE_THETA = 10000.0
ADAM_LR = 0.001
ADAM_B1 = 0.9
ADAM_B2 = 0.999
ADAM_EPS = 1e-08
ADAM_WD = 0.01
ADAM_STEP = 10
VMEM_LIMIT = 56 * 1024 * 1024
ADAM_VMEM_BYTES = 24 * 1024 * 1024
MESH = pl.DeviceIdType.MESH
ANY = pl.BlockSpec(memory_space=pl.ANY)


def _pick(n, cands):
    for t in cands:
        if n % t == 0:
            return t
    return n


def _params(sem):
    return pltpu.CompilerParams(dimension_semantics=sem, vmem_limit_bytes=VMEM_LIMIT)


def _sigmoid(z):
    return 1.0 / (1.0 + jnp.exp(-z))


def mm_nn(a, b3, out_dtype, name):
    M, K = a.shape
    R, _, n = b3.shape
    tm = _pick(M, (768, 512, 256, 128))
    tn = _pick(n, (1408, 1024, 768, 512, 256, 128))
    tk = _pick(K, (512, 256, 128))
    nb, nk = n // tn, K // tk

    def body(a_ref, b_ref, o_ref, acc):
        k = pl.program_id(2)

        @pl.when(k == 0)
        def _():
            acc[...] = jnp.zeros_like(acc)

        acc[...] += jnp.dot(a_ref[...], b_ref[0], preferred_element_type=F32)

        @pl.when(k == nk - 1)
        def _():
            o_ref[...] = acc[...].astype(o_ref.dtype)

    return pl.pallas_call(
        body, name=name, grid=(M // tm, R * nb, nk),
        in_specs=[pl.BlockSpec((tm, tk), lambda i, j, k: (i, k)),
                  pl.BlockSpec((1, tk, tn), lambda i, j, k: (j // nb, k, j % nb))],
        out_specs=pl.BlockSpec((tm, tn), lambda i, j, k: (i, j)),
        out_shape=jax.ShapeDtypeStruct((M, R * n), out_dtype),
        scratch_shapes=[pltpu.VMEM((tm, tn), F32)],
        compiler_params=_params(("parallel", "parallel", "arbitrary")),
    )(a, b3)


def mm_nt(a, b3, out_dtype, name):
    M, _ = a.shape
    R, K, n = b3.shape
    tm = _pick(M, (768, 512, 256, 128))
    tko = _pick(K, (1024, 512, 256, 128))
    tc = _pick(n, (1408, 1024, 768, 512, 256, 128))
    ncb = n // tc
    nc = R * ncb

    def body(a_ref, b_ref, o_ref, acc):
        k = pl.program_id(2)

        @pl.when(k == 0)
        def _():
            acc[...] = jnp.zeros_like(acc)

        acc[...] += lax.dot_general(a_ref[...], b_ref[0], (((1,), (1,)), ((), ())),
                                    preferred_element_type=F32)

        @pl.when(k == nc - 1)
        def _():
            o_ref[...] = acc[...].astype(o_ref.dtype)

    return pl.pallas_call(
        body, name=name, grid=(M // tm, K // tko, nc),
        in_specs=[pl.BlockSpec((tm, tc), lambda i, j, k: (i, k)),
                  pl.BlockSpec((1, tko, tc), lambda i, j, k: (k // ncb, j, k % ncb))],
        out_specs=pl.BlockSpec((tm, tko), lambda i, j, k: (i, j)),
        out_shape=jax.ShapeDtypeStruct((M, K), out_dtype),
        scratch_shapes=[pltpu.VMEM((tm, tko), F32)],
        compiler_params=_params(("parallel", "parallel", "arbitrary")),
    )(a, b3)


def mm_tn(a, b, R, out_dtype, name):
    M, K = a.shape
    n = b.shape[1] // R
    tm = _pick(M, (768, 512, 256, 128))
    tk = _pick(K, (1024, 512, 256, 128))
    tn = _pick(n, (1408, 1024, 768, 512, 256, 128))
    nb, nm = n // tn, M // tm

    def body(a_ref, b_ref, o_ref, acc):
        m = pl.program_id(2)

        @pl.when(m == 0)
        def _():
            acc[...] = jnp.zeros_like(acc)

        acc[...] += lax.dot_general(a_ref[...], b_ref[...], (((0,), (0,)), ((), ())),
                                    preferred_element_type=F32)

        @pl.when(m == nm - 1)
        def _():
            o_ref[0] = acc[...].astype(o_ref.dtype)

    return pl.pallas_call(
        body, name=name, grid=(K // tk, R * nb, nm),
        in_specs=[pl.BlockSpec((tm, tk), lambda i, j, m: (m, i)),
                  pl.BlockSpec((tm, tn), lambda i, j, m: (m, j))],
        out_specs=pl.BlockSpec((1, tk, tn), lambda i, j, m: (j // nb, i, j % nb)),
        out_shape=jax.ShapeDtypeStruct((R, K, n), out_dtype),
        scratch_shapes=[pltpu.VMEM((tk, tn), F32)],
        compiler_params=_params(("parallel", "parallel", "arbitrary")),
    )(a, b)


def _seg_spec(D):
    return pl.BlockSpec((None, 1, D), lambda i: (jnp.minimum(i, 1), 0, 0))


def _seg3(a):
    return a.reshape(2, 1, a.shape[-1])


def _row_spec(w, col=0):
    return pl.BlockSpec((ROW_TILE, w), lambda i: (i, col))


def _acc_spec(r, w):
    return pl.BlockSpec((r, w), lambda i: (0, 0))


def _seg_accumulate(ref, i, val):
    ref[0:1, :] += jnp.where(i == 0, val, 0.0)
    ref[1:2, :] += jnp.where(i == 0, 0.0, val)


def rms_mod_fwd(x, g, shift2, scale2, name):
    T, D = x.shape

    def body(x_ref, g_ref, sh_ref, sc_ref, h_ref):
        xv = x_ref[...]
        rstd = lax.rsqrt(jnp.mean(xv * xv, axis=-1, keepdims=True) + EPS)
        h = (xv * rstd * g_ref[...]) * (1.0 + sc_ref[...]) + sh_ref[...]
        h_ref[...] = h.astype(h_ref.dtype)

    return pl.pallas_call(
        body, name=name, grid=(T // ROW_TILE,),
        in_specs=[_row_spec(D), _acc_spec(1, D), _seg_spec(D), _seg_spec(D)],
        out_specs=_row_spec(D),
        out_shape=jax.ShapeDtypeStruct((T, D), BF16),
        compiler_params=_params(("parallel",)),
    )(x, g, _seg3(shift2), _seg3(scale2))


def rms_mod_bwd(x, g, scale2, dh, dres, name):
    T, D = x.shape

    def body(x_ref, g_ref, sc_ref, dh_ref, dres_ref, dx_ref, dg_ref, dsh_ref, dsc_ref):
        i = pl.program_id(0)

        @pl.when(i == 0)
        def _():
            dg_ref[...] = jnp.zeros_like(dg_ref)
            dsh_ref[...] = jnp.zeros_like(dsh_ref)
            dsc_ref[...] = jnp.zeros_like(dsc_ref)

        xv = x_ref[...]
        dh = dh_ref[...].astype(F32)
        gv = g_ref[...]
        rstd = lax.rsqrt(jnp.mean(xv * xv, axis=-1, keepdims=True) + EPS)
        xh = xv * rstd
        u = dh * (1.0 + sc_ref[...])
        dg_ref[...] += jnp.sum(u * xh, axis=0, keepdims=True)
        _seg_accumulate(dsh_ref, i, jnp.sum(dh, axis=0, keepdims=True))
        _seg_accumulate(dsc_ref, i, jnp.sum(dh * xh * gv, axis=0, keepdims=True))
        dxh = u * gv
        dx = rstd * (dxh - xh * jnp.mean(dxh * xh, axis=-1, keepdims=True))
        dx_ref[...] = dres_ref[...] + dx

    return pl.pallas_call(
        body, name=name, grid=(T // ROW_TILE,),
        in_specs=[_row_spec(D), _acc_spec(1, D), _seg_spec(D), _row_spec(D), _row_spec(D)],
        out_specs=[_row_spec(D), _acc_spec(1, D), _acc_spec(2, D), _acc_spec(2, D)],
        out_shape=[jax.ShapeDtypeStruct((T, D), F32), jax.ShapeDtypeStruct((1, D), F32),
                   jax.ShapeDtypeStruct((2, D), F32), jax.ShapeDtypeStruct((2, D), F32)],
        compiler_params=_params(("arbitrary",)),
    )(x, g, _seg3(scale2), dh, dres)


def gate_res_fwd(x, y, gate2, name):
    T, D = x.shape

    def body(x_ref, y_ref, g_ref, o_ref):
        o_ref[...] = x_ref[...] + g_ref[...] * y_ref[...]

    return pl.pallas_call(
        body, name=name, grid=(T // ROW_TILE,),
        in_specs=[_row_spec(D), _row_spec(D), _seg_spec(D)],
        out_specs=_row_spec(D),
        out_shape=jax.ShapeDtypeStruct((T, D), F32),
        compiler_params=_params(("parallel",)),
    )(x, y, _seg3(gate2))


def gate_bwd(dxo, y, gate2, name):
    T, D = dxo.shape

    def body(d_ref, y_ref, g_ref, dy_ref, dg_ref):
        i = pl.program_id(0)

        @pl.when(i == 0)
        def _():
            dg_ref[...] = jnp.zeros_like(dg_ref)

        d = d_ref[...]
        dy_ref[...] = (d * g_ref[...]).astype(dy_ref.dtype)
        _seg_accumulate(dg_ref, i, jnp.sum(d * y_ref[...], axis=0, keepdims=True))

    return pl.pallas_call(
        body, name=name, grid=(T // ROW_TILE,),
        in_specs=[_row_spec(D), _row_spec(D), _seg_spec(D)],
        out_specs=[_row_spec(D), _acc_spec(2, D)],
        out_shape=[jax.ShapeDtypeStruct((T, D), BF16), jax.ShapeDtypeStruct((2, D), F32)],
        compiler_params=_params(("arbitrary",)),
    )(dxo, y, _seg3(gate2))


def final_loss(x, g, target, name):
    T, D = x.shape

    def body(x_ref, g_ref, t_ref, dx_ref, dg_ref, loss_ref):
        i = pl.program_id(0)

        @pl.when(i == 0)
        def _():
            dg_ref[...] = jnp.zeros_like(dg_ref)
            loss_ref[...] = jnp.zeros_like(loss_ref)
            dx_ref[...] = jnp.zeros_like(dx_ref)

        @pl.when(i > 0)
        def _():
            xv = x_ref[...]
            gv = g_ref[...]
            rstd = lax.rsqrt(jnp.mean(xv * xv, axis=-1, keepdims=True) + EPS)
            xh = xv * rstd
            err = xh * gv - t_ref[...]
            loss_ref[...] += 0.5 * jnp.sum(jnp.mean(err * err, axis=-1, keepdims=True))
            dy = err * (1.0 / D)
            dg_ref[...] += jnp.sum(dy * xh, axis=0, keepdims=True)
            dxh = dy * gv
            dx_ref[...] = rstd * (dxh - xh * jnp.mean(dxh * xh, axis=-1, keepdims=True))

    return pl.pallas_call(
        body, name=name, grid=(T // ROW_TILE,),
        in_specs=[_row_spec(D), _acc_spec(1, D),
                  pl.BlockSpec((ROW_TILE, D), lambda i: (jnp.maximum(i - 1, 0), 0))],
        out_specs=[_row_spec(D), _acc_spec(1, D), _acc_spec(8, 128)],
        out_shape=[jax.ShapeDtypeStruct((T, D), F32), jax.ShapeDtypeStruct((1, D), F32),
                   jax.ShapeDtypeStruct((8, 128), F32)],
        compiler_params=_params(("arbitrary",)),
    )(x, g, target)


def _swap32(v):
    lane = lax.broadcasted_iota(jnp.int32, v.shape, 1)
    return jnp.where((lane & 63) < 32, pltpu.roll(v, 96, 1), pltpu.roll(v, 32, 1))


def _rope(v, cos, sin):
    return v * cos + _swap32(v) * sin


def _rope_t(d, cos, sin):
    return d * cos + _swap32(d * sin)


def _chunk_of(step, n_chunks, n_ctx, rev):
    if not rev:
        return step
    return jnp.where(step < n_ctx, n_ctx - 1 - step, n_chunks + n_ctx - 1 - step)


def _dot_t0(a, b):
    return lax.dot_general(a, b, (((0,), (0,)), ((), ())), preferred_element_type=F32)


def _dot_t1(a, b):
    return lax.dot_general(a, b, (((1,), (1,)), ((), ())), preferred_element_type=F32)


def _dot(a, b):
    return jnp.dot(a, b, preferred_element_type=F32)


def ret_fwd(p, cos, sin, tabs, n_heads, n_ctx, rev, name):
    T = p.shape[0]
    C = RET_CHUNK
    H = n_heads
    NC = T // C
    scale = C ** -0.5
    dm, qd, kd, cd = tabs["dm"], tabs["qd"], tabs["kd"], tabs["cd"]

    def body(q_ref, k_ref, v_ref, cos_ref, sin_ref, dm_ref, qd_ref, kd_ref, cd_ref, o_ref, s_ref, S):
        t = pl.program_id(1)

        @pl.when(t == 0)
        def _():
            S[...] = jnp.zeros_like(S)

        cs, sn = cos_ref[...], sin_ref[...]
        q = _rope(q_ref[...], cs, sn)
        k = _rope(k_ref[...], cs, sn) * scale
        qb, kb, vb = q.astype(BF16), k.astype(BF16), v_ref[...].astype(BF16)
        A = _dot_t1(qb, kb) * dm_ref[0]
        s_in = S[...]
        s_ref[0, 0] = s_in
        o_ref[...] = _dot(A.astype(BF16), vb) + _dot(qb, s_in.astype(BF16)) * qd_ref[0]
        S[...] = s_in * cd_ref[0] + _dot_t0((k * kd_ref[0]).astype(BF16), vb)

    cmap = lambda t: _chunk_of(t, NC, n_ctx, rev)
    blk = lambda col: pl.BlockSpec((C, C), lambda h, t: (cmap(t), col * H + h))
    tab = pl.BlockSpec((C, C), lambda h, t: (cmap(t), 0))
    htab = lambda r: pl.BlockSpec((1, r, C), lambda h, t: (h, 0, 0))
    return pl.pallas_call(
        body, name=name, grid=(H, NC),
        in_specs=[blk(0), blk(1), blk(2), tab, tab, htab(C), htab(C), htab(C), htab(1)],
        out_specs=[pl.BlockSpec((C, C), lambda h, t: (cmap(t), h)),
                   pl.BlockSpec((1, 1, C, C), lambda h, t: (h, cmap(t), 0, 0))],
        out_shape=[jax.ShapeDtypeStruct((T, H * C), F32), jax.ShapeDtypeStruct((H, NC, C, C), F32)],
        scratch_shapes=[pltpu.VMEM((C, C), F32)],
        compiler_params=_params(("parallel", "arbitrary")),
    )(p, p, p, cos, sin, dm, qd, kd, cd)


def ret_bwd(p, cos, sin, tabs, do, s_saved, n_heads, n_ctx, rev, name):
    T = p.shape[0]
    C = RET_CHUNK
    H = n_heads
    NC = T // C
    scale = C ** -0.5
    dm, qd, kd, cd, em, eq, ek = (tabs[n] for n in ("dm", "qd", "kd", "cd", "em", "eq", "ek"))

    def body(q_ref, k_ref, v_ref, cos_ref, sin_ref, dm_ref, qd_ref, kd_ref, cd_ref, em_ref, eq_ref,
             ek_ref, do_ref, s_ref, dq_ref, dk_ref, dv_ref, dlg_ref, dS):
        t = pl.program_id(1)

        @pl.when(t == 0)
        def _():
            dS[...] = jnp.zeros_like(dS)
            dlg_ref[...] = jnp.zeros_like(dlg_ref)

        cs, sn = cos_ref[...], sin_ref[...]
        q = _rope(q_ref[...], cs, sn)
        k = _rope(k_ref[...], cs, sn) * scale
        qb, kb, vb = q.astype(BF16), k.astype(BF16), v_ref[...].astype(BF16)
        dmv, qdv, kdv, cdv = dm_ref[0], qd_ref[0], kd_ref[0], cd_ref[0]
        A = _dot_t1(qb, kb) * dmv
        s_in = s_ref[0, 0]
        sb = s_in.astype(BF16)
        ds_out = dS[...]
        dsb = ds_out.astype(BF16)
        dov = do_ref[...]
        dob = dov.astype(BF16)
        dA = _dot_t1(dob, vb)
        dPb = (dA * dmv).astype(BF16)
        doq = dov * qdv
        doqb = doq.astype(BF16)
        kk = k * kdv
        vds = _dot_t1(vb, dsb)
        dq_ref[...] = _dot(dPb, kb) + _dot_t1(doqb, sb)
        dk_ref[...] = _dot_t0(dPb, qb) + vds * kdv
        dv_ref[...] = _dot_t0(A.astype(BF16), dob) + _dot(kk.astype(BF16), dsb)
        dS[...] = ds_out * cdv + _dot_t0(qb, doqb)
        o2 = _dot(qb, sb)
        part = (jnp.sum(dA * A * em_ref[...], axis=0, keepdims=True)
                + jnp.sum(eq_ref[...] * doq * o2, axis=0, keepdims=True)
                + jnp.sum(ek_ref[...] * kk * vds, axis=0, keepdims=True)
                + float(C) * cdv * jnp.sum(s_in * ds_out, axis=0, keepdims=True))
        dlg_ref[0, 0:1, :] += part

    cmap = lambda t: _chunk_of(NC - 1 - t, NC, n_ctx, rev)
    blk = lambda col: pl.BlockSpec((C, C), lambda h, t: (cmap(t), col * H + h))
    tab = pl.BlockSpec((C, C), lambda h, t: (cmap(t), 0))
    const = pl.BlockSpec((C, C), lambda h, t: (0, 0))
    htab = lambda r: pl.BlockSpec((1, r, C), lambda h, t: (h, 0, 0))
    hblk = pl.BlockSpec((C, C), lambda h, t: (cmap(t), h))
    return pl.pallas_call(
        body, name=name, grid=(H, NC),
        in_specs=[blk(0), blk(1), blk(2), tab, tab, htab(C), htab(C), htab(C), htab(1), const, const,
                  const, hblk, pl.BlockSpec((1, 1, C, C), lambda h, t: (h, cmap(t), 0, 0))],
        out_specs=[hblk, hblk, hblk, pl.BlockSpec((1, 8, C), lambda h, t: (h, 0, 0))],
        out_shape=[jax.ShapeDtypeStruct((T, H * C), F32)] * 3 + [jax.ShapeDtypeStruct((H, 8, C), F32)],
        scratch_shapes=[pltpu.VMEM((C, C), F32)],
        compiler_params=_params(("parallel", "arbitrary")),
    )(p, p, p, cos, sin, dm, qd, kd, cd, em, eq, ek, do, s_saved)


def ret_out_fwd(o_f, o_b, p, n_heads, name):
    T, RW = o_f.shape
    C = RET_CHUNK

    def body(of_ref, ob_ref, g_ref, out_ref):
        for h in range(n_heads):
            sl = slice(h * C, (h + 1) * C)
            o = of_ref[:, sl] + ob_ref[:, sl]
            r = o * lax.rsqrt(jnp.mean(o * o, axis=-1, keepdims=True) + EPS)
            g = g_ref[:, sl]
            out_ref[:, sl] = (g * _sigmoid(g) * r).astype(out_ref.dtype)

    return pl.pallas_call(
        body, name=name, grid=(T // ROW_TILE,),
        in_specs=[_row_spec(RW), _row_spec(RW), _row_spec(RW, 3)],
        out_specs=_row_spec(RW),
        out_shape=jax.ShapeDtypeStruct((T, RW), BF16),
        compiler_params=_params(("parallel",)),
    )(o_f, o_b, p)


def ret_out_bwd(o_f, o_b, p, dmix, n_heads, name):
    T, RW = o_f.shape
    C = RET_CHUNK

    def body(of_ref, ob_ref, g_ref, d_ref, do_ref, dg_ref):
        for h in range(n_heads):
            sl = slice(h * C, (h + 1) * C)
            o = of_ref[:, sl] + ob_ref[:, sl]
            rstd = lax.rsqrt(jnp.mean(o * o, axis=-1, keepdims=True) + EPS)
            r = o * rstd
            g = g_ref[:, sl]
            sg = _sigmoid(g)
            d = d_ref[:, sl].astype(F32)
            dg_ref[:, sl] = (d * r * sg * (1.0 + g * (1.0 - sg))).astype(dg_ref.dtype)
            dr = d * g * sg
            do_ref[:, sl] = rstd * (dr - r * jnp.mean(dr * r, axis=-1, keepdims=True))

    return pl.pallas_call(
        body, name=name, grid=(T // ROW_TILE,),
        in_specs=[_row_spec(RW), _row_spec(RW), _row_spec(RW, 3), _row_spec(RW, 0)],
        out_specs=[_row_spec(RW), _row_spec(RW)],
        out_shape=[jax.ShapeDtypeStruct((T, RW), F32), jax.ShapeDtypeStruct((T, RW), BF16)],
        compiler_params=_params(("parallel",)),
    )(o_f, o_b, p, dmix)


def ret_qkv_grad(dqf, dqb, dkf, dkb, dvf, dvb, cos, sin, n_heads, name):
    T, RW = dqf.shape
    C = RET_CHUNK
    scale = C ** -0.5

    def body(qf, qb, kf, kb, vf, vb, cos_ref, sin_ref, dq_ref, dk_ref, dv_ref):
        cs, sn = cos_ref[...], sin_ref[...]
        for h in range(n_heads):
            sl = slice(h * C, (h + 1) * C)
            dq_ref[:, sl] = _rope_t(qf[:, sl] + qb[:, sl], cs, sn).astype(dq_ref.dtype)
            dk_ref[:, sl] = _rope_t((kf[:, sl] + kb[:, sl]) * scale, cs, sn).astype(dk_ref.dtype)
            dv_ref[:, sl] = (vf[:, sl] + vb[:, sl]).astype(dv_ref.dtype)

    return pl.pallas_call(
        body, name=name, grid=(T // ROW_TILE,),
        in_specs=[_row_spec(RW)] * 6 + [_row_spec(C), _row_spec(C)],
        out_specs=[_row_spec(RW)] * 3,
        out_shape=[jax.ShapeDtypeStruct((T, RW), BF16)] * 3,
        compiler_params=_params(("parallel",)),
    )(dqf, dqb, dkf, dkb, dvf, dvb, cos, sin)


def _halo_specs(width, col, halo, n_rows):
    per = ROW_TILE // halo
    last = n_rows // halo - 1
    return [pl.BlockSpec((halo, width), lambda i: (jnp.maximum(i * per - 1, 0), col)),
            pl.BlockSpec((ROW_TILE, width), lambda i: (i, col)),
            pl.BlockSpec((halo, width), lambda i: (jnp.minimum((i + 1) * per, last), col))]


def _halo_valid(i, n_tiles):
    return i >= 2, jnp.logical_and(i >= 1, i <= n_tiles - 2)


def conv_fwd(p, w, bias, ln_g, ln_b, name):
    T = p.shape[0]
    CW = w.shape[1]
    NT = T // ROW_TILE
    HL = CONV_HALO
    PAD = CONV_K // 2

    def body(ap, ac, an, bp, bc, bn, w_ref, b_ref, g_ref, be_ref, u2_ref, out_ref, U):
        i = pl.program_id(0)
        vp, vn = _halo_valid(i, NT)
        U[0:HL, :] = jnp.where(vp, ap[...] * _sigmoid(bp[...]), 0.0)
        U[HL:HL + ROW_TILE, :] = ac[...] * _sigmoid(bc[...])
        U[HL + ROW_TILE:, :] = jnp.where(vn, an[...] * _sigmoid(bn[...]), 0.0)
        acc = jnp.zeros((ROW_TILE, CW), F32) + b_ref[...]
        for j in range(CONV_K):
            acc = acc + w_ref[j:j + 1, :] * U[pl.ds(HL - PAD + j, ROW_TILE), :]
        u2_ref[...] = acc
        mu = jnp.mean(acc, axis=-1, keepdims=True)
        xc = acc - mu
        rstd = lax.rsqrt(jnp.mean(xc * xc, axis=-1, keepdims=True) + EPS)
        ln = xc * rstd * g_ref[...] + be_ref[...]
        out_ref[...] = (ln * _sigmoid(ln)).astype(out_ref.dtype)

    vec = _acc_spec(1, CW)
    return pl.pallas_call(
        body, name=name, grid=(NT,),
        in_specs=_halo_specs(CW, 4, HL, T) + _halo_specs(CW, 5, HL, T) + [_acc_spec(CONV_K, CW), vec, vec, vec],
        out_specs=[_row_spec(CW), _row_spec(CW)],
        out_shape=[jax.ShapeDtypeStruct((T, CW), F32), jax.ShapeDtypeStruct((T, CW), BF16)],
        scratch_shapes=[pltpu.VMEM((ROW_TILE + 2 * HL, CW), F32)],
        compiler_params=_params(("parallel",)),
    )(p, p, p, p, p, p, w, bias, ln_g, ln_b)


def conv_bwd_ln(u2, dmix, ln_g, ln_b, name):
    T, CW = u2.shape

    def body(u_ref, d_ref, g_ref, be_ref, du_ref, dg_ref, db_ref, dbias_ref):
        i = pl.program_id(0)

        @pl.when(i == 0)
        def _():
            dg_ref[...] = jnp.zeros_like(dg_ref)
            db_ref[...] = jnp.zeros_like(db_ref)
            dbias_ref[...] = jnp.zeros_like(dbias_ref)

        u = u_ref[...]
        gv = g_ref[...]
        mu = jnp.mean(u, axis=-1, keepdims=True)
        xc = u - mu
        rstd = lax.rsqrt(jnp.mean(xc * xc, axis=-1, keepdims=True) + EPS)
        xh = xc * rstd
        ln = xh * gv + be_ref[...]
        sg = _sigmoid(ln)
        dln = d_ref[...].astype(F32) * sg * (1.0 + ln * (1.0 - sg))
        dg_ref[...] += jnp.sum(dln * xh, axis=0, keepdims=True)
        db_ref[...] += jnp.sum(dln, axis=0, keepdims=True)
        dxh = dln * gv
        du = rstd * (dxh - jnp.mean(dxh, axis=-1, keepdims=True)
                     - xh * jnp.mean(dxh * xh, axis=-1, keepdims=True))
        du_ref[...] = du
        dbias_ref[...] += jnp.sum(du, axis=0, keepdims=True)

    vec = _acc_spec(1, CW)
    return pl.pallas_call(
        body, name=name, grid=(T // ROW_TILE,),
        in_specs=[_row_spec(CW), _row_spec(CW, 1), vec, vec],
        out_specs=[_row_spec(CW), vec, vec, vec],
        out_shape=[jax.ShapeDtypeStruct((T, CW), F32)] + [jax.ShapeDtypeStruct((1, CW), F32)] * 3,
        compiler_params=_params(("arbitrary",)),
    )(u2, dmix, ln_g, ln_b)


def conv_bwd_taps(p, du2, w, name):
    T = p.shape[0]
    CW = w.shape[1]
    NT = T // ROW_TILE
    HL = CONV_HALO
    PAD = CONV_K // 2

    def body(ap, ac, an, bp, bc, bn, dp, dc, dn, w_ref, da_ref, db_ref, dw_ref, U, DU):
        i = pl.program_id(0)

        @pl.when(i == 0)
        def _():
            dw_ref[...] = jnp.zeros_like(dw_ref)

        vp, vn = _halo_valid(i, NT)
        a = ac[...]
        sg = _sigmoid(bc[...])
        U[0:HL, :] = jnp.where(vp, ap[...] * _sigmoid(bp[...]), 0.0)
        U[HL:HL + ROW_TILE, :] = a * sg
        U[HL + ROW_TILE:, :] = jnp.where(vn, an[...] * _sigmoid(bn[...]), 0.0)
        d = dc[...]
        DU[0:HL, :] = jnp.where(vp, dp[...], 0.0)
        DU[HL:HL + ROW_TILE, :] = d
        DU[HL + ROW_TILE:, :] = jnp.where(vn, dn[...], 0.0)
        du = jnp.zeros((ROW_TILE, CW), F32)
        for j in range(CONV_K):
            du = du + w_ref[j:j + 1, :] * DU[pl.ds(HL + PAD - j, ROW_TILE), :]
            dw_ref[j:j + 1, :] += jnp.sum(d * U[pl.ds(HL - PAD + j, ROW_TILE), :], axis=0, keepdims=True)
        da_ref[...] = (du * sg).astype(da_ref.dtype)
        db_ref[...] = (du * a * sg * (1.0 - sg)).astype(db_ref.dtype)

    return pl.pallas_call(
        body, name=name, grid=(NT,),
        in_specs=(_halo_specs(CW, 4, HL, T) + _halo_specs(CW, 5, HL, T) + _halo_specs(CW, 0, HL, T)
                  + [_acc_spec(CONV_K, CW)]),
        out_specs=[_row_spec(CW), _row_spec(CW), _acc_spec(CONV_K, CW)],
        out_shape=[jax.ShapeDtypeStruct((T, CW), BF16), jax.ShapeDtypeStruct((T, CW), BF16),
                   jax.ShapeDtypeStruct((CONV_K, CW), F32)],
        scratch_shapes=[pltpu.VMEM((ROW_TILE + 2 * HL, CW), F32)] * 2,
        compiler_params=_params(("arbitrary",)),
    )(p, p, p, p, p, p, du2, du2, du2, w)


def _ffn_halo_specs(tc, col0, n_rows):
    per = ROW_TILE // FFN_HALO
    last = n_rows // FFN_HALO - 1
    return [pl.BlockSpec((FFN_HALO, tc), lambda cb, i: (jnp.maximum(i * per - 1, 0), col0 + cb)),
            pl.BlockSpec((ROW_TILE, tc), lambda cb, i: (i, col0 + cb)),
            pl.BlockSpec((FFN_HALO, tc), lambda cb, i: (jnp.minimum((i + 1) * per, last), col0 + cb))]


def _ffn_taps(i, shape):
    is_lat = i >= 1
    col = lax.broadcasted_iota(jnp.int32, shape, 0) & (GRID_W - 1)
    not_first = jnp.logical_or(col >= 1, jnp.logical_not(is_lat))
    not_last = jnp.logical_or(col <= GRID_W - 2, jnp.logical_not(is_lat))
    lat = jnp.where(is_lat, 1.0, 0.0)
    taps = []
    for di in range(3):
        for dj in range(3):
            off = (di - 1) * GRID_W + (dj - 1)
            taps.append((off, 1.0 if di == 1 else lat, (not_first, None, not_last)[dj]))
    return taps


def ffn_conv_fwd(up, w9, bias, name):
    T = up.shape[0]
    DFF = w9.shape[1]
    tc = _pick(DFF, (512, 256, 128))
    ncb = DFF // tc
    NT = T // ROW_TILE
    HL = FFN_HALO

    def body(gp, gc, gn, val_ref, w_ref, b_ref, cg_ref, act_ref, G):
        i = pl.program_id(1)
        vp, vn = _halo_valid(i, NT)
        G[0:HL, :] = jnp.where(vp, gp[...], 0.0)
        G[HL:HL + ROW_TILE, :] = gc[...]
        G[HL + ROW_TILE:, :] = jnp.where(vn, gn[...], 0.0)
        acc = jnp.zeros((ROW_TILE, tc), F32) + b_ref[...]
        for t, (off, fac, mask) in enumerate(_ffn_taps(i, (ROW_TILE, tc))):
            tap = G[pl.ds(HL + off, ROW_TILE), :]
            if mask is not None:
                tap = jnp.where(mask, tap, 0.0)
            acc = acc + (w_ref[t:t + 1, :] * fac) * tap
        cg_ref[...] = acc
        act_ref[...] = (acc * _sigmoid(acc) * val_ref[...]).astype(act_ref.dtype)

    tile = pl.BlockSpec((ROW_TILE, tc), lambda cb, i: (i, cb))
    return pl.pallas_call(
        body, name=name, grid=(ncb, NT),
        in_specs=_ffn_halo_specs(tc, 0, T) + [pl.BlockSpec((ROW_TILE, tc), lambda cb, i: (i, ncb + cb)),
                                              pl.BlockSpec((9, tc), lambda cb, i: (0, cb)),
                                              pl.BlockSpec((1, tc), lambda cb, i: (0, cb))],
        out_specs=[tile, tile],
        out_shape=[jax.ShapeDtypeStruct((T, DFF), F32), jax.ShapeDtypeStruct((T, DFF), BF16)],
        scratch_shapes=[pltpu.VMEM((ROW_TILE + 2 * HL, tc), F32)],
        compiler_params=_params(("parallel", "parallel")),
    )(up, up, up, up, w9, bias)


def ffn_conv_bwd_act(cg, up, dact, name):
    T, DFF = cg.shape
    tc = _pick(DFF, (512, 256, 128))
    ncb = DFF // tc

    def body(cg_ref, val_ref, d_ref, dcg_ref, dval_ref, db_ref):
        i = pl.program_id(1)

        @pl.when(i == 0)
        def _():
            db_ref[...] = jnp.zeros_like(db_ref)

        c = cg_ref[...]
        sg = _sigmoid(c)
        d = d_ref[...].astype(F32)
        dval_ref[...] = (d * c * sg).astype(dval_ref.dtype)
        dcg = d * val_ref[...] * sg * (1.0 + c * (1.0 - sg))
        dcg_ref[...] = dcg
        db_ref[...] += jnp.sum(dcg, axis=0, keepdims=True)

    tile = pl.BlockSpec((ROW_TILE, tc), lambda cb, i: (i, cb))
    return pl.pallas_call(
        body, name=name, grid=(ncb, T // ROW_TILE),
        in_specs=[tile, pl.BlockSpec((ROW_TILE, tc), lambda cb, i: (i, ncb + cb)), tile],
        out_specs=[tile, tile, pl.BlockSpec((1, tc), lambda cb, i: (0, cb))],
        out_shape=[jax.ShapeDtypeStruct((T, DFF), F32), jax.ShapeDtypeStruct((T, DFF), BF16),
                   jax.ShapeDtypeStruct((1, DFF), F32)],
        compiler_params=_params(("parallel", "arbitrary")),
    )(cg, up, dact)


def ffn_conv_bwd_taps(up, dcg, w9, name):
    T, DFF = dcg.shape
    tc = _pick(DFF, (512, 256, 128))
    ncb = DFF // tc
    NT = T // ROW_TILE
    HL = FFN_HALO

    def body(gp, gc, gn, dp, dc, dn, w_ref, dgate_ref, dw_ref, G, DC):
        i = pl.program_id(1)

        @pl.when(i == 0)
        def _():
            dw_ref[...] = jnp.zeros_like(dw_ref)

        vp, vn = _halo_valid(i, NT)
        G[0:HL, :] = jnp.where(vp, gp[...], 0.0)
        G[HL:HL + ROW_TILE, :] = gc[...]
        G[HL + ROW_TILE:, :] = jnp.where(vn, gn[...], 0.0)
        d = dc[...]
        DC[0:HL, :] = jnp.where(vp, dp[...], 0.0)
        DC[HL:HL + ROW_TILE, :] = d
        DC[HL + ROW_TILE:, :] = jnp.where(vn, dn[...], 0.0)
        dg = jnp.zeros((ROW_TILE, tc), F32)
        for t, (off, fac, mask) in enumerate(_ffn_taps(i, (ROW_TILE, tc))):
            wt = w_ref[t:t + 1, :] * fac
            src = DC[pl.ds(HL - off, ROW_TILE), :]
            dm = d
            if mask is not None:
                shifted = (lax.broadcasted_iota(jnp.int32, (ROW_TILE, tc), 0) - off) & (GRID_W - 1)
                is_lat = i >= 1
                ok = (shifted >= 1) if off % GRID_W == GRID_W - 1 else (shifted <= GRID_W - 2)
                src = jnp.where(jnp.logical_or(ok, jnp.logical_not(is_lat)), src, 0.0)
                dm = jnp.where(mask, d, 0.0)
            dg = dg + wt * src
            dw_ref[t:t + 1, :] += fac * jnp.sum(dm * G[pl.ds(HL + off, ROW_TILE), :], axis=0, keepdims=True)
        dgate_ref[...] = dg.astype(dgate_ref.dtype)

    tile = pl.BlockSpec((ROW_TILE, tc), lambda cb, i: (i, cb))
    return pl.pallas_call(
        body, name=name, grid=(ncb, NT),
        in_specs=_ffn_halo_specs(tc, 0, T) + _ffn_halo_specs(tc, 0, T) + [pl.BlockSpec((9, tc), lambda cb, i: (0, cb))],
        out_specs=[tile, pl.BlockSpec((9, tc), lambda cb, i: (0, cb))],
        out_shape=[jax.ShapeDtypeStruct((T, DFF), BF16), jax.ShapeDtypeStruct((9, DFF), F32)],
        scratch_shapes=[pltpu.VMEM((ROW_TILE + 2 * HL, tc), F32)] * 2,
        compiler_params=_params(("parallel", "arbitrary")),
    )(up, up, up, dcg, dcg, dcg, w9)


def adamw(parts, w, m, v, name):
    P, R, C = parts.shape
    fits = lambda t: 2 * (P + 7) * t * C * 4 <= ADAM_VMEM_BYTES
    tr = R if fits(R) else _pick(R, [t for t in (1024, 512, 256, 128, 64, 32, 16, 8) if fits(t)])
    c1 = 1.0 - ADAM_B1 ** ADAM_STEP
    c2 = 1.0 - ADAM_B2 ** ADAM_STEP

    def body(p_ref, w_ref, m_ref, v_ref, g_ref, d_ref, nm_ref, nv_ref):
        g = p_ref[0]
        for k in range(1, P):
            g = g + p_ref[k]
        nm = ADAM_B1 * m_ref[...] + (1.0 - ADAM_B1) * g
        nv = ADAM_B2 * v_ref[...] + (1.0 - ADAM_B2) * (g * g)
        g_ref[...] = g
        nm_ref[...] = nm
        nv_ref[...] = nv
        d_ref[...] = -ADAM_LR * ((nm / c1) / (jnp.sqrt(nv / c2) + ADAM_EPS) + ADAM_WD * w_ref[...])

    tile = pl.BlockSpec((tr, C), lambda i: (i, 0))
    return pl.pallas_call(
        body, name=name, grid=(R // tr,),
        in_specs=[pl.BlockSpec((P, tr, C), lambda i: (0, i, 0)), tile, tile, tile],
        out_specs=[tile] * 4,
        out_shape=[jax.ShapeDtypeStruct((R, C), F32)] * 4,
        compiler_params=_params(("parallel",)),
    )(parts, w, m, v)


def _my_rank():
    return 4 * lax.axis_index("x") + 2 * lax.axis_index("y") + lax.axis_index("c")


def _peer(j):
    x, y, c = lax.axis_index("x"), lax.axis_index("y"), lax.axis_index("c")
    px = 1 - x if j & 4 else x
    py = 1 - y if j & 2 else y
    pc = 1 - c if j & 1 else c
    return (px, py, pc), 4 * px + 2 * py + pc


def all_gather(arrs, name):
    n = len(arrs)

    def body(*refs):
        ins, outs = refs[:n], refs[n:2 * n]
        send, recv, lsem = refs[2 * n:]
        me = _my_rank()
        started = []
        for a in range(n):
            loc = pltpu.make_async_copy(ins[a], outs[a].at[me], lsem.at[a])
            loc.start()
            started.append(loc)
        for a in range(n):
            for j in range(1, N_DEV):
                dev, _ = _peer(j)
                s = a * (N_DEV - 1) + j - 1
                cp = pltpu.make_async_remote_copy(src_ref=ins[a], dst_ref=outs[a].at[me], send_sem=send.at[s],
                                                  recv_sem=recv.at[s], device_id=dev, device_id_type=MESH)
                cp.start()
        for a in range(n):
            for j in range(1, N_DEV):
                dev, rank = _peer(j)
                s = a * (N_DEV - 1) + j - 1
                cp = pltpu.make_async_remote_copy(src_ref=ins[a], dst_ref=outs[a].at[rank], send_sem=send.at[s],
                                                  recv_sem=recv.at[s], device_id=dev, device_id_type=MESH)
                cp.wait_recv()
                cp.wait_send()
        for loc in started:
            loc.wait()

    return pl.pallas_call(
        body, name=name,
        in_specs=[ANY] * n, out_specs=[ANY] * n,
        out_shape=[jax.ShapeDtypeStruct((N_DEV,) + a.shape, a.dtype) for a in arrs],
        scratch_shapes=[pltpu.SemaphoreType.DMA((n * (N_DEV - 1),)), pltpu.SemaphoreType.DMA((n * (N_DEV - 1),)),
                        pltpu.SemaphoreType.DMA((n,))],
    )(*arrs)


def exchange(parts, name):
    n = len(parts)

    def body(*refs):
        ins, outs = refs[:n], refs[n:2 * n]
        send, recv, lsem = refs[2 * n:]
        me = _my_rank()
        started = []
        for a in range(n):
            loc = pltpu.make_async_copy(ins[a].at[me], outs[a].at[me], lsem.at[a])
            loc.start()
            started.append(loc)
        for a in range(n):
            for j in range(1, N_DEV):
                dev, rank = _peer(j)
                s = a * (N_DEV - 1) + j - 1
                cp = pltpu.make_async_remote_copy(src_ref=ins[a].at[rank], dst_ref=outs[a].at[me], send_sem=send.at[s],
                                                  recv_sem=recv.at[s], device_id=dev, device_id_type=MESH)
                cp.start()
        for a in range(n):
            for j in range(1, N_DEV):
                dev, rank = _peer(j)
                s = a * (N_DEV - 1) + j - 1
                cp = pltpu.make_async_remote_copy(src_ref=ins[a].at[rank], dst_ref=outs[a].at[rank], send_sem=send.at[s],
                                                  recv_sem=recv.at[s], device_id=dev, device_id_type=MESH)
                cp.wait_recv()
                cp.wait_send()
        for loc in started:
            loc.wait()

    return pl.pallas_call(
        body, name=name,
        in_specs=[ANY] * n, out_specs=[ANY] * n,
        out_shape=[jax.ShapeDtypeStruct(a.shape, a.dtype) for a in parts],
        scratch_shapes=[pltpu.SemaphoreType.DMA((n * (N_DEV - 1),)), pltpu.SemaphoreType.DMA((n * (N_DEV - 1),)),
                        pltpu.SemaphoreType.DMA((n,))],
    )(*parts)


def _rope_tables(seq, ctx):
    t = jnp.arange(seq)
    quarter = RET_CHUNK // 4
    inv_freq = 1.0 / (ROPE_THETA ** (jnp.arange(0, quarter, dtype=F32) / quarter))
    ang_r = (t // GRID_W).astype(F32)[:, None] * inv_freq[None, :]
    ang_c = (t % GRID_W).astype(F32)[:, None] * inv_freq[None, :]
    cr, sr, cc, sc = jnp.cos(ang_r), jnp.sin(ang_r), jnp.cos(ang_c), jnp.sin(ang_c)
    cos = jnp.concatenate([cr, cr, cc, cc], axis=-1)
    sin = jnp.concatenate([-sr, sr, -sc, sc], axis=-1)
    cos = jnp.concatenate([jnp.ones((ctx, RET_CHUNK), F32), cos], axis=0)
    sin = jnp.concatenate([jnp.zeros((ctx, RET_CHUNK), F32), sin], axis=0)
    return cos, sin


def _decay_tables(decay_logit, rev):
    C = RET_CHUNK
    lg = jax.nn.log_sigmoid(decay_logit.astype(F32))
    idx = jnp.arange(C, dtype=F32)
    diff = idx[:, None] - idx[None, :]
    if rev:
        diff = -diff
        eq, ek = C - idx, idx
    else:
        eq, ek = idx + 1.0, C - 1.0 - idx
    keep = diff >= 0
    em = jnp.where(keep, diff, 0.0)
    bc = lambda e: jnp.broadcast_to(e[:, None], (C, C))
    return {
        "dm": jnp.where(keep[None], jnp.exp(lg[:, None, None] * em[None]), 0.0),
        "qd": jnp.broadcast_to(jnp.exp(lg[:, None] * eq[None, :])[:, :, None], (lg.shape[0], C, C)),
        "kd": jnp.broadcast_to(jnp.exp(lg[:, None] * ek[None, :])[:, :, None], (lg.shape[0], C, C)),
        "cd": jnp.broadcast_to(jnp.exp(lg * C)[:, None, None], (lg.shape[0], 1, C)),
        "em": em, "eq": bc(eq), "ek": bc(ek),
    }


def _silu(z):
    return z * jax.nn.sigmoid(z)


def _dsilu(z):
    s = jax.nn.sigmoid(z)
    return s * (1.0 + z * (1.0 - s))


def kernel(x, c, ctx, c_ctx, w_mod, b_mod, norm1_g, norm2_g, w_in, ret_decay_f, ret_decay_b, conv_dw_w, conv_dw_b, conv_ln_g, conv_ln_b, w_out, ffn_w_up, ffn_dw_w, ffn_dw_b, ffn_w_down, final_norm_g, loss_target, m_c_ctx, m_w_mod, m_b_mod, m_norm1_g, m_norm2_g, m_w_in, m_ret_decay_f, m_ret_decay_b, m_conv_dw_w, m_conv_dw_b, m_conv_ln_g, m_conv_ln_b, m_w_out, m_ffn_w_up, m_ffn_dw_w, m_ffn_dw_b, m_ffn_w_down, m_final_norm_g, v_c_ctx, v_w_mod, v_b_mod, v_norm1_g, v_norm2_g, v_w_in, v_ret_decay_f, v_ret_decay_b, v_conv_dw_w, v_conv_dw_b, v_conv_ln_g, v_conv_ln_b, v_w_out, v_ffn_w_up, v_ffn_dw_w, v_ffn_dw_b, v_ffn_w_down, v_final_norm_g):
    L, D, _ = w_mod.shape
    SEQ, CTX = x.shape[1], ctx.shape[1]
    T = SEQ + CTX
    RW = D // 2
    CW = D - RW
    H = RW // RET_CHUNK
    DFF = ffn_dw_b.shape[1]
    NMOD = b_mod.shape[1] // D
    n_ctx = CTX // RET_CHUNK
    assert CTX == ROW_TILE and RW == CW and SEQ % ROW_TILE == 0 and NMOD == 6
    me = _my_rank()
    wm_n, wi_n, wu_n = w_mod.shape[2], w_in.shape[2], ffn_w_up.shape[2]
    wo_k, wd_k = w_out.shape[1], ffn_w_down.shape[1]
    cw_n, fw_n = conv_dw_w.shape[2], ffn_dw_w.shape[3]

    w_mod_b = w_mod.astype(BF16)
    g_in, g_out, g_up, g_down, g_cw, g_fw, g_c = all_gather(
        [w_in.astype(BF16), w_out.astype(BF16), ffn_w_up.astype(BF16), ffn_w_down.astype(BF16),
         conv_dw_w, ffn_dw_w, _silu(c)], "gather_weights")
    w_in_l = [g_in[:, l] for l in range(L)]
    w_up_l = [g_up[:, l] for l in range(L)]
    w_out_l = [g_out[:, l].reshape(1, N_DEV * wo_k, D) for l in range(L)]
    w_down_l = [g_down[:, l].reshape(1, N_DEV * wd_k, D) for l in range(L)]
    conv_w_l = [jnp.moveaxis(g_cw[:, l], 0, 1).reshape(CONV_K, CW) for l in range(L)]
    ffn_w9_l = [jnp.moveaxis(g_fw[:, l], 0, 2).reshape(9, DFF) for l in range(L)]

    s_cond = jnp.concatenate([g_c.reshape(N_DEV, D), jnp.broadcast_to(_silu(c_ctx)[None], (N_DEV, D))], axis=0)
    s_cond_b = s_cond.astype(BF16)
    mod_shard = mm_nn(s_cond_b, w_mod_b, F32, "mod_fwd")
    (g_mod,) = all_gather([mod_shard], "gather_mod")
    mod_all = jnp.transpose(g_mod.reshape(N_DEV, 2 * N_DEV, L, wm_n), (2, 1, 0, 3)).reshape(L, 2 * N_DEV, NMOD * D)
    mod_all = mod_all + b_mod[:, None, :]
    mod_lat = lax.dynamic_index_in_dim(mod_all, me, axis=1, keepdims=False)
    mod_ctx = mod_all[:, N_DEV]
    mod2 = jnp.stack([mod_ctx, mod_lat], axis=1).reshape(L, 2, NMOD, D)

    cos, sin = _rope_tables(SEQ, CTX)
    xs = jnp.concatenate([ctx[0], x[0]], axis=0)

    saved = []
    for l in range(L):
        sh1, sc1, g1, sh2, sc2, g2 = (mod2[l, :, k] for k in range(NMOD))
        tf = _decay_tables(ret_decay_f[l], False)
        tb = _decay_tables(ret_decay_b[l], True)
        h = rms_mod_fwd(xs, norm1_g[l][None], sh1, sc1, "norm1_fwd")
        p = mm_nn(h, w_in_l[l], F32, "in_proj")
        o_f, s_f = ret_fwd(p, cos, sin, tf, H, n_ctx, False, "ret_fwd_f")
        o_b, s_b = ret_fwd(p, cos, sin, tb, H, n_ctx, True, "ret_fwd_b")
        mix_r = ret_out_fwd(o_f, o_b, p, H, "ret_out_fwd")
        u2, mix_c = conv_fwd(p, conv_w_l[l], conv_dw_b[l][None], conv_ln_g[l][None], conv_ln_b[l][None], "conv_fwd")
        mix = jnp.concatenate([mix_r, mix_c], axis=1)
        y1 = mm_nn(mix, w_out_l[l], F32, "out_proj")
        x2 = gate_res_fwd(xs, y1, g1, "res1_fwd")
        h2 = rms_mod_fwd(x2, norm2_g[l][None], sh2, sc2, "norm2_fwd")
        up = mm_nn(h2, w_up_l[l], F32, "ffn_up")
        cg, act = ffn_conv_fwd(up, ffn_w9_l[l], ffn_dw_b[l][None], "ffn_conv_fwd")
        y2 = mm_nn(act, w_down_l[l], F32, "ffn_down")
        x3 = gate_res_fwd(x2, y2, g2, "res2_fwd")
        saved.append(dict(x1=xs, h=h, p=p, o_f=o_f, o_b=o_b, s_f=s_f, s_b=s_b, u2=u2, mix=mix, y1=y1, x2=x2,
                          h2=h2, up=up, cg=cg, act=act, y2=y2, tf=tf, tb=tb))
        xs = x3

    dxs, d_final_g, loss_part = final_loss(xs, final_norm_g[None], loss_target[0], "final_loss")
    loss = lax.psum(loss_part[0, 0], ("x", "y", "c"))

    big = {n: [None] * L for n in ("w_in", "w_out", "ffn_w_up", "ffn_w_down")}
    small = {n: [None] * L for n in ("norm1_g", "norm2_g", "ret_decay_f", "ret_decay_b", "conv_dw_w", "conv_dw_b",
                                     "conv_ln_g", "conv_ln_b", "ffn_dw_w", "ffn_dw_b")}
    dmod2 = [None] * L
    for l in reversed(range(L)):
        sv = saved[l]
        sh1, sc1, g1, sh2, sc2, g2 = (mod2[l, :, k] for k in range(NMOD))
        dy2, dg2 = gate_bwd(dxs, sv["y2"], g2, "res2_bwd")
        dact = mm_nt(dy2, w_down_l[l], F32, "ffn_down_dx")
        big["ffn_w_down"][l] = mm_tn(sv["act"], dy2, 1, F32, "ffn_down_dw").reshape(N_DEV, wd_k, D)
        dcg, dval, small["ffn_dw_b"][l] = ffn_conv_bwd_act(sv["cg"], sv["up"], dact, "ffn_conv_bwd_act")
        dgate, small["ffn_dw_w"][l] = ffn_conv_bwd_taps(sv["up"], dcg, ffn_w9_l[l], "ffn_conv_bwd_taps")
        dup = jnp.concatenate([dgate, dval], axis=1)
        dh2 = mm_nt(dup, w_up_l[l], F32, "ffn_up_dx")
        big["ffn_w_up"][l] = mm_tn(sv["h2"], dup, N_DEV, F32, "ffn_up_dw")
        dx2, small["norm2_g"][l], dsh2, dsc2 = rms_mod_bwd(sv["x2"], norm2_g[l][None], sc2, dh2, dxs, "norm2_bwd")
        dy1, dg1 = gate_bwd(dx2, sv["y1"], g1, "res1_bwd")
        dmix = mm_nt(dy1, w_out_l[l], BF16, "out_proj_dx")
        big["w_out"][l] = mm_tn(sv["mix"], dy1, 1, F32, "out_proj_dw").reshape(N_DEV, wo_k, D)
        do, dgt = ret_out_bwd(sv["o_f"], sv["o_b"], sv["p"], dmix, H, "ret_out_bwd")
        dqf, dkf, dvf, dlg_f = ret_bwd(sv["p"], cos, sin, sv["tf"], do, sv["s_f"], H, n_ctx, False, "ret_bwd_f")
        dqb, dkb, dvb, dlg_b = ret_bwd(sv["p"], cos, sin, sv["tb"], do, sv["s_b"], H, n_ctx, True, "ret_bwd_b")
        dq, dk, dv = ret_qkv_grad(dqf, dqb, dkf, dkb, dvf, dvb, cos, sin, H, "ret_qkv_grad")
        small["ret_decay_f"][l] = jnp.sum(dlg_f[:, 0, :], axis=-1) * jax.nn.sigmoid(-ret_decay_f[l])
        small["ret_decay_b"][l] = jnp.sum(dlg_b[:, 0, :], axis=-1) * jax.nn.sigmoid(-ret_decay_b[l])
        du2, small["conv_ln_g"][l], small["conv_ln_b"][l], small["conv_dw_b"][l] = conv_bwd_ln(
            sv["u2"], dmix, conv_ln_g[l][None], conv_ln_b[l][None], "conv_bwd_ln")
        da, dbg, small["conv_dw_w"][l] = conv_bwd_taps(sv["p"], du2, conv_w_l[l], "conv_bwd_taps")
        dp = jnp.concatenate([dq, dk, dv, dgt, da, dbg], axis=1)
        dh = mm_nt(dp, w_in_l[l], F32, "in_proj_dx")
        big["w_in"][l] = mm_tn(sv["h"], dp, N_DEV, F32, "in_proj_dw")
        dxs, small["norm1_g"][l], dsh1, dsc1 = rms_mod_bwd(sv["x1"], norm1_g[l][None], sc1, dh, dx2, "norm1_bwd")
        dmod2[l] = jnp.concatenate([dsh1, dsc1, dg1, dsh2, dsc2, dg2], axis=1)

    grad_x = dxs[CTX:][None]

    dmod2 = jnp.stack(dmod2)
    (g_dmod,) = all_gather([dmod2], "gather_dmod")
    dmod_all = jnp.concatenate([jnp.moveaxis(g_dmod[:, :, 1], 0, 1), jnp.moveaxis(g_dmod[:, :, 0], 0, 1)], axis=1)
    dmod_sh = lax.dynamic_slice_in_dim(dmod_all, me * wm_n, wm_n, axis=2)
    dmod_sh = jnp.moveaxis(dmod_sh, 0, 1).reshape(2 * N_DEV, L * wm_n).astype(BF16)
    g_w_mod = mm_tn(s_cond_b, dmod_sh, L, F32, "mod_dw")
    d_cond = mm_nt(dmod_sh, w_mod_b, F32, "mod_dx")
    g_c_ctx_part = jnp.sum(d_cond[N_DEV:], axis=0) * _dsilu(c_ctx)
    g_b_mod_part = dmod2[:, 0] + dmod2[:, 1]

    pad128 = lambda a: jnp.pad(a.reshape(-1), (0, (-a.size) % 128))
    rep_names = ["c_ctx", "b_mod", "norm1_g", "norm2_g", "ret_decay_f", "ret_decay_b", "conv_dw_b", "conv_ln_g",
                 "conv_ln_b", "ffn_dw_b", "final_norm_g"]
    given = dict(c_ctx=(c_ctx, m_c_ctx, v_c_ctx), b_mod=(b_mod, m_b_mod, v_b_mod),
                 norm1_g=(norm1_g, m_norm1_g, v_norm1_g), norm2_g=(norm2_g, m_norm2_g, v_norm2_g),
                 ret_decay_f=(ret_decay_f, m_ret_decay_f, v_ret_decay_f),
                 ret_decay_b=(ret_decay_b, m_ret_decay_b, v_ret_decay_b),
                 conv_dw_b=(conv_dw_b, m_conv_dw_b, v_conv_dw_b), conv_ln_g=(conv_ln_g, m_conv_ln_g, v_conv_ln_g),
                 conv_ln_b=(conv_ln_b, m_conv_ln_b, v_conv_ln_b), ffn_dw_b=(ffn_dw_b, m_ffn_dw_b, v_ffn_dw_b),
                 final_norm_g=(final_norm_g, m_final_norm_g, v_final_norm_g))
    rep_part = dict(c_ctx=g_c_ctx_part, b_mod=g_b_mod_part, final_norm_g=d_final_g)
    for nme in rep_names:
        if nme not in rep_part:
            rep_part[nme] = jnp.stack([a.reshape(-1) for a in small[nme]])
    rep_sizes = [((-given[nme][0].size) % 128) + given[nme][0].size for nme in rep_names]
    n_rep = sum(rep_sizes)
    cw_part = jnp.stack(small["conv_dw_w"])
    fw_part = jnp.stack(small["ffn_dw_w"])
    packed = jnp.concatenate([pad128(rep_part[nme]) for nme in rep_names] + [cw_part.reshape(-1), fw_part.reshape(-1)])
    (g_small,) = all_gather([packed.reshape(-1, 128)], "gather_small")
    g_small = g_small.reshape(N_DEV, -1)
    rep_w, rep_m, rep_v = (jnp.concatenate([pad128(given[nme][k]) for nme in rep_names]).reshape(-1, 128) for k in range(3))
    rep_out = adamw(g_small[:, :n_rep].reshape(N_DEV, -1, 128), rep_w, rep_m, rep_v, "adamw_small")
    res = {}
    off = 0
    for nme, sz in zip(rep_names, rep_sizes):
        shape = given[nme][0].shape
        res[nme] = [o.reshape(-1)[off:off + given[nme][0].size].reshape(shape) for o in rep_out]
        off += sz

    cw_all = g_small[:, n_rep:n_rep + cw_part.size].reshape(N_DEV, L * CONV_K, CW)
    cw_mine = lax.dynamic_slice_in_dim(cw_all, me * cw_n, cw_n, axis=2)
    res["conv_dw_w"] = [o.reshape(conv_dw_w.shape) for o in adamw(
        cw_mine, conv_dw_w.reshape(L * CONV_K, cw_n), m_conv_dw_w.reshape(L * CONV_K, cw_n),
        v_conv_dw_w.reshape(L * CONV_K, cw_n), "adamw_conv_w")]
    fw_all = g_small[:, n_rep + cw_part.size:].reshape(N_DEV, L * 9, DFF)
    fw_mine = lax.dynamic_slice_in_dim(fw_all, me * fw_n, fw_n, axis=2)
    res["ffn_dw_w"] = [o.reshape(ffn_dw_w.shape) for o in adamw(
        fw_mine, ffn_dw_w.reshape(L * 9, fw_n), m_ffn_dw_w.reshape(L * 9, fw_n),
        v_ffn_dw_w.reshape(L * 9, fw_n), "adamw_ffn_w")]

    res["w_mod"] = [o.reshape(w_mod.shape) for o in adamw(
        g_w_mod.reshape(1, L * D, wm_n), w_mod.reshape(L * D, wm_n), m_w_mod.reshape(L * D, wm_n),
        v_w_mod.reshape(L * D, wm_n), "adamw_w_mod")]

    big_given = dict(w_in=(w_in, m_w_in, v_w_in), w_out=(w_out, m_w_out, v_w_out),
                     ffn_w_up=(ffn_w_up, m_ffn_w_up, v_ffn_w_up), ffn_w_down=(ffn_w_down, m_ffn_w_down, v_ffn_w_down))
    big_names = ["w_in", "w_out", "ffn_w_up", "ffn_w_down"]
    big_out = {nme: [] for nme in big_names}
    for l in range(L):
        landed = exchange([big[nme][l] for nme in big_names], "exchange_grads")
        for nme, parts in zip(big_names, landed):
            w, m, v = (a[l] for a in big_given[nme])
            big_out[nme].append(adamw(parts, w, m, v, "adamw_" + nme))
    for nme in big_names:
        res[nme] = [jnp.stack([big_out[nme][l][k] for l in range(L)]) for k in range(4)]

    order = ["c_ctx", "w_mod", "b_mod", "norm1_g", "norm2_g", "w_in", "ret_decay_f", "ret_decay_b", "conv_dw_w",
             "conv_dw_b", "conv_ln_g", "conv_ln_b", "w_out", "ffn_w_up", "ffn_dw_w", "ffn_dw_b", "ffn_w_down",
             "final_norm_g"]
    return (loss, grad_x, *[res[nme][0] for nme in order], *[res[nme][1] for nme in order],
            *[res[nme][2] for nme in order], *[res[nme][3] for nme in order])
```

```python
import functools

import jax
import jax.numpy as jnp
from jax import lax
from jax.experimental import pallas as pl
from jax.experimental.pallas import tpu as pltpu

F32 = jnp.float32
BF16 = jnp.bfloat16
EPS = 1e-6
N_DEV = 8
ROW_TILE = 256
RET_CHUNK = 128
GRID_W = 64
CONV_K = 31
CONV_HALO = 16
FFN_HALO = 128
ROPE_THETA = 10000.0
ADAM_LR = 0.001
ADAM_B1 = 0.9
ADAM_B2 = 0.999
ADAM_EPS = 1e-08
ADAM_WD = 0.01
ADAM_STEP = 10
VMEM_LIMIT = 56 * 1024 * 1024
ADAM_VMEM_BYTES = 24 * 1024 * 1024
MESH = pl.DeviceIdType.MESH
ANY = pl.BlockSpec(memory_space=pl.ANY)


def _pick(n, cands):
    for t in cands:
        if n % t == 0:
            return t
    return n


def _sigmoid(z):
    return 1.0 / (1.0 + jnp.exp(-z))


def _my_rank():
    return 4 * lax.axis_index("x") + 2 * lax.axis_index("y") + lax.axis_index("c")


def _peer(j):
    x, y, c = lax.axis_index("x"), lax.axis_index("y"), lax.axis_index("c")
    px = 1 - x if j & 4 else x
    py = 1 - y if j & 2 else y
    pc = 1 - c if j & 1 else c
    return (px, py, pc), 4 * px + 2 * py + pc


class Comm:
    def __init__(self, kind, arrs):
        assert kind in ("gather", "exchange")
        self.kind, self.arrs, self.n = kind, list(arrs), len(arrs)
        self.in_specs = [ANY] * self.n
        self.out_specs = [ANY] * self.n
        lead = (N_DEV,) if kind == "gather" else ()
        self.out_shape = [jax.ShapeDtypeStruct(lead + a.shape, a.dtype) for a in self.arrs]
        per = self.n * (N_DEV - 1)
        self.scratch = [pltpu.SemaphoreType.DMA((per,)), pltpu.SemaphoreType.DMA((per,)),
                        pltpu.SemaphoreType.DMA((self.n,))]

    def _src(self, ref, rank):
        return ref if self.kind == "gather" else ref.at[rank]

    def _local(self, ins, outs, sems, a):
        me = _my_rank()
        return pltpu.make_async_copy(self._src(ins[a], me), outs[a].at[me], sems[2].at[a])

    def _remote(self, ins, outs, sems, a, j, receive):
        dev, rank = _peer(j)
        s = a * (N_DEV - 1) + j - 1
        slot = rank if receive else _my_rank()
        return pltpu.make_async_remote_copy(src_ref=self._src(ins[a], rank), dst_ref=outs[a].at[slot],
                                            send_sem=sems[0].at[s], recv_sem=sems[1].at[s],
                                            device_id=dev, device_id_type=MESH)

    def start(self, ins, outs, sems):
        for a in range(self.n):
            self._local(ins, outs, sems, a).start()
        for a in range(self.n):
            for j in range(1, N_DEV):
                self._remote(ins, outs, sems, a, j, False).start()

    def wait(self, ins, outs, sems):
        for a in range(self.n):
            for j in range(1, N_DEV):
                cp = self._remote(ins, outs, sems, a, j, True)
                cp.wait_recv()
                cp.wait_send()
        for a in range(self.n):
            self._local(ins, outs, sems, a).wait()


def _call(compute, *, name, grid, in_specs, out_specs, out_shape, operands, sem, scratch=(), comm=None):
    n_in, n_out, n_sc = len(in_specs), len(out_specs), len(scratch)
    k = comm.n if comm else 0

    def body(*refs):
        ins, cin = refs[:n_in], refs[n_in:n_in + k]
        o0 = n_in + k
        outs, cout = refs[o0:o0 + n_out], refs[o0 + n_out:o0 + n_out + k]
        s0 = o0 + n_out + k
        sc, sems = refs[s0:s0 + n_sc], refs[s0 + n_sc:]
        if comm:
            ids = [pl.program_id(d) for d in range(len(grid))]
            first = functools.reduce(jnp.logical_and, [i == 0 for i in ids])
            last = functools.reduce(jnp.logical_and, [i == g - 1 for i, g in zip(ids, grid)])

            @pl.when(first)
            def _():
                comm.start(cin, cout, sems)

        compute(*ins, *outs, *sc)

        if comm:
            @pl.when(last)
            def _():
                comm.wait(cin, cout, sems)

    semantics = ("arbitrary",) * len(grid) if comm else sem
    res = pl.pallas_call(
        body, name=name, grid=grid,
        in_specs=list(in_specs) + (comm.in_specs if comm else []),
        out_specs=list(out_specs) + (comm.out_specs if comm else []),
        out_shape=list(out_shape) + (comm.out_shape if comm else []),
        scratch_shapes=list(scratch) + (comm.scratch if comm else []),
        compiler_params=pltpu.CompilerParams(dimension_semantics=semantics, vmem_limit_bytes=VMEM_LIMIT),
    )(*operands, *(comm.arrs if comm else []))
    if comm:
        return list(res[:n_out]), list(res[n_out:])
    return list(res)


def run_comm(kind, arrs, name):
    comm = Comm(kind, arrs)

    def body(*refs):
        ins, outs, sems = refs[:comm.n], refs[comm.n:2 * comm.n], refs[2 * comm.n:]
        comm.start(ins, outs, sems)
        comm.wait(ins, outs, sems)

    return list(pl.pallas_call(body, name=name, in_specs=comm.in_specs, out_specs=comm.out_specs,
                               out_shape=comm.out_shape, scratch_shapes=comm.scratch)(*comm.arrs))


M_TILES = (768, 512, 256, 128)
WIDE_TILES = (2048, 1408, 1024, 768, 512, 256, 128)
MID_TILES = (1408, 1024, 768, 512, 256, 128)


def _mm_body(dot, n_steps, axis):
    if n_steps == 1:
        def compute(a_ref, b_ref, o_ref):
            o_ref[...] = dot(a_ref, b_ref).astype(o_ref.dtype).reshape(o_ref.shape)
        return compute, []

    def compute(a_ref, b_ref, o_ref, acc):
        k = pl.program_id(axis)

        @pl.when(k == 0)
        def _():
            acc[...] = jnp.zeros_like(acc)

        acc[...] += dot(a_ref, b_ref)

        @pl.when(k == n_steps - 1)
        def _():
            o_ref[...] = acc[...].astype(o_ref.dtype).reshape(o_ref.shape)

    return compute, None


def mm_nn(a, b3, out_dtype, name, comm=None):
    M, K = a.shape
    R, _, n = b3.shape
    tm, tk, tn = _pick(M, M_TILES), _pick(K, WIDE_TILES), _pick(n, MID_TILES)
    nb, nk = n // tn, K // tk
    compute, scratch = _mm_body(lambda a_ref, b_ref: jnp.dot(a_ref[...], b_ref[0], preferred_element_type=F32), nk, 2)
    return _call(
        compute, name=name, grid=(M // tm, R * nb, nk),
        in_specs=[pl.BlockSpec((tm, tk), lambda i, j, k: (i, k)),
                  pl.BlockSpec((1, tk, tn), lambda i, j, k: (j // nb, k, j % nb))],
        out_specs=[pl.BlockSpec((tm, tn), lambda i, j, k: (i, j))],
        out_shape=[jax.ShapeDtypeStruct((M, R * n), out_dtype)],
        scratch=scratch if scratch is not None else [pltpu.VMEM((tm, tn), F32)],
        operands=(a, b3), sem=("parallel", "parallel", "arbitrary"), comm=comm)


def mm_nt(a, b3, out_dtype, name, comm=None):
    M, _ = a.shape
    R, K, n = b3.shape
    tm, tko, tc = _pick(M, M_TILES), _pick(K, WIDE_TILES), _pick(n, WIDE_TILES)
    ncb = n // tc
    nc = R * ncb
    dot = lambda a_ref, b_ref: lax.dot_general(a_ref[...], b_ref[0], (((1,), (1,)), ((), ())),
                                               preferred_element_type=F32)
    compute, scratch = _mm_body(dot, nc, 2)
    return _call(
        compute, name=name, grid=(M // tm, K // tko, nc),
        in_specs=[pl.BlockSpec((tm, tc), lambda i, j, k: (i, k)),
                  pl.BlockSpec((1, tko, tc), lambda i, j, k: (k // ncb, j, k % ncb))],
        out_specs=[pl.BlockSpec((tm, tko), lambda i, j, k: (i, j))],
        out_shape=[jax.ShapeDtypeStruct((M, K), out_dtype)],
        scratch=scratch if scratch is not None else [pltpu.VMEM((tm, tko), F32)],
        operands=(a, b3), sem=("parallel", "parallel", "arbitrary"), comm=comm)


def mm_tn(a, b, R, out_dtype, name):
    M, K = a.shape
    n = b.shape[1] // R
    tm, tk, tn = _pick(M, (1408,) + M_TILES), _pick(K, MID_TILES), _pick(n, MID_TILES)
    nb, nm = n // tn, M // tm
    dot = lambda a_ref, b_ref: lax.dot_general(a_ref[...], b_ref[...], (((0,), (0,)), ((), ())),
                                               preferred_element_type=F32)
    compute, scratch = _mm_body(dot, nm, 2)
    return _call(
        compute, name=name, grid=(K // tk, R * nb, nm),
        in_specs=[pl.BlockSpec((tm, tk), lambda i, j, m: (m, i)),
                  pl.BlockSpec((tm, tn), lambda i, j, m: (m, j))],
        out_specs=[pl.BlockSpec((1, tk, tn), lambda i, j, m: (j // nb, i, j % nb))],
        out_shape=[jax.ShapeDtypeStruct((R, K, n), out_dtype)],
        scratch=scratch if scratch is not None else [pltpu.VMEM((tk, tn), F32)],
        operands=(a, b), sem=("parallel", "parallel", "arbitrary"))[0]


def _seg_spec(D):
    return pl.BlockSpec((None, 1, D), lambda i: (jnp.minimum(i, 1), 0, 0))


def _seg3(a):
    return a.reshape(2, 1, a.shape[-1])


def _row_spec(w, col=0):
    return pl.BlockSpec((ROW_TILE, w), lambda i: (i, col))


def _acc_spec(r, w):
    return pl.BlockSpec((r, w), lambda i: (0, 0))


def _seg_accumulate(ref, i, val):
    ref[0:1, :] += jnp.where(i == 0, val, 0.0)
    ref[1:2, :] += jnp.where(i == 0, 0.0, val)


def rms_mod_fwd(x, g, shift2, scale2, name):
    T, D = x.shape

    def compute(x_ref, g_ref, sh_ref, sc_ref, h_ref):
        xv = x_ref[...]
        rstd = lax.rsqrt(jnp.mean(xv * xv, axis=-1, keepdims=True) + EPS)
        h = (xv * rstd * g_ref[...]) * (1.0 + sc_ref[...]) + sh_ref[...]
        h_ref[...] = h.astype(h_ref.dtype)

    return _call(compute, name=name, grid=(T // ROW_TILE,),
                 in_specs=[_row_spec(D), _acc_spec(1, D), _seg_spec(D), _seg_spec(D)],
                 out_specs=[_row_spec(D)], out_shape=[jax.ShapeDtypeStruct((T, D), BF16)],
                 operands=(x, g, _seg3(shift2), _seg3(scale2)), sem=("parallel",))[0]


def rms_mod_bwd(x, g, scale2, dh, dres, name):
    T, D = x.shape

    def compute(x_ref, g_ref, sc_ref, dh_ref, dres_ref, dx_ref, dg_ref, dsh_ref, dsc_ref):
        i = pl.program_id(0)

        @pl.when(i == 0)
        def _():
            dg_ref[...] = jnp.zeros_like(dg_ref)
            dsh_ref[...] = jnp.zeros_like(dsh_ref)
            dsc_ref[...] = jnp.zeros_like(dsc_ref)

        xv = x_ref[...]
        dh = dh_ref[...].astype(F32)
        gv = g_ref[...]
        rstd = lax.rsqrt(jnp.mean(xv * xv, axis=-1, keepdims=True) + EPS)
        xh = xv * rstd
        u = dh * (1.0 + sc_ref[...])
        dg_ref[...] += jnp.sum(u * xh, axis=0, keepdims=True)
        _seg_accumulate(dsh_ref, i, jnp.sum(dh, axis=0, keepdims=True))
        _seg_accumulate(dsc_ref, i, jnp.sum(dh * xh * gv, axis=0, keepdims=True))
        dxh = u * gv
        dx = rstd * (dxh - xh * jnp.mean(dxh * xh, axis=-1, keepdims=True))
        dx_ref[...] = dres_ref[...] + dx

    return _call(compute, name=name, grid=(T // ROW_TILE,),
                 in_specs=[_row_spec(D), _acc_spec(1, D), _seg_spec(D), _row_spec(D), _row_spec(D)],
                 out_specs=[_row_spec(D), _acc_spec(1, D), _acc_spec(2, D), _acc_spec(2, D)],
                 out_shape=[jax.ShapeDtypeStruct((T, D), F32), jax.ShapeDtypeStruct((1, D), F32),
                            jax.ShapeDtypeStruct((2, D), F32), jax.ShapeDtypeStruct((2, D), F32)],
                 operands=(x, g, _seg3(scale2), dh, dres), sem=("arbitrary",))


def gate_res_fwd(x, y, gate2, name):
    T, D = x.shape

    def compute(x_ref, y_ref, g_ref, o_ref):
        o_ref[...] = x_ref[...] + g_ref[...] * y_ref[...]

    return _call(compute, name=name, grid=(T // ROW_TILE,),
                 in_specs=[_row_spec(D), _row_spec(D), _seg_spec(D)],
                 out_specs=[_row_spec(D)], out_shape=[jax.ShapeDtypeStruct((T, D), F32)],
                 operands=(x, y, _seg3(gate2)), sem=("parallel",))[0]


def gate_bwd(dxo, y, gate2, name):
    T, D = dxo.shape

    def compute(d_ref, y_ref, g_ref, dy_ref, dg_ref):
        i = pl.program_id(0)

        @pl.when(i == 0)
        def _():
            dg_ref[...] = jnp.zeros_like(dg_ref)

        d = d_ref[...]
        dy_ref[...] = (d * g_ref[...]).astype(dy_ref.dtype)
        _seg_accumulate(dg_ref, i, jnp.sum(d * y_ref[...], axis=0, keepdims=True))

    return _call(compute, name=name, grid=(T // ROW_TILE,),
                 in_specs=[_row_spec(D), _row_spec(D), _seg_spec(D)],
                 out_specs=[_row_spec(D), _acc_spec(2, D)],
                 out_shape=[jax.ShapeDtypeStruct((T, D), BF16), jax.ShapeDtypeStruct((2, D), F32)],
                 operands=(dxo, y, _seg3(gate2)), sem=("arbitrary",))


def final_loss(x, g, target, name):
    T, D = x.shape

    def compute(x_ref, g_ref, t_ref, dx_ref, dg_ref, loss_ref):
        i = pl.program_id(0)

        @pl.when(i == 0)
        def _():
            dg_ref[...] = jnp.zeros_like(dg_ref)
            loss_ref[...] = jnp.zeros_like(loss_ref)
            dx_ref[...] = jnp.zeros_like(dx_ref)

        @pl.when(i > 0)
        def _():
            xv = x_ref[...]
            gv = g_ref[...]
            rstd = lax.rsqrt(jnp.mean(xv * xv, axis=-1, keepdims=True) + EPS)
            xh = xv * rstd
            err = xh * gv - t_ref[...]
            loss_ref[...] += 0.5 * jnp.sum(jnp.mean(err * err, axis=-1, keepdims=True))
            dy = err * (1.0 / D)
            dg_ref[...] += jnp.sum(dy * xh, axis=0, keepdims=True)
            dxh = dy * gv
            dx_ref[...] = rstd * (dxh - xh * jnp.mean(dxh * xh, axis=-1, keepdims=True))

    return _call(compute, name=name, grid=(T // ROW_TILE,),
                 in_specs=[_row_spec(D), _acc_spec(1, D),
                           pl.BlockSpec((ROW_TILE, D), lambda i: (jnp.maximum(i - 1, 0), 0))],
                 out_specs=[_row_spec(D), _acc_spec(1, D), _acc_spec(8, 128)],
                 out_shape=[jax.ShapeDtypeStruct((T, D), F32), jax.ShapeDtypeStruct((1, D), F32),
                            jax.ShapeDtypeStruct((8, 128), F32)],
                 operands=(x, g, target), sem=("arbitrary",))


def _swap32(v):
    lane = lax.broadcasted_iota(jnp.int32, v.shape, 1)
    return jnp.where((lane & 63) < 32, pltpu.roll(v, 96, 1), pltpu.roll(v, 32, 1))


def _rope(v, cos, sin):
    return v * cos + _swap32(v) * sin


def _rope_t(d, cos, sin):
    return d * cos + _swap32(d * sin)


def _chunk_of(step, n_chunks, n_ctx, rev):
    if not rev:
        return step
    return jnp.where(step < n_ctx, n_ctx - 1 - step, n_chunks + n_ctx - 1 - step)


def _dot_t0(a, b):
    return lax.dot_general(a, b, (((0,), (0,)), ((), ())), preferred_element_type=F32)


def _dot_t1(a, b):
    return lax.dot_general(a, b, (((1,), (1,)), ((), ())), preferred_element_type=F32)


def _dot(a, b):
    return jnp.dot(a, b, preferred_element_type=F32)


def ret_fwd(p, cos, sin, tabs, n_heads, n_ctx, rev, name):
    T = p.shape[0]
    C = RET_CHUNK
    H = n_heads
    NC = T // C
    scale = C ** -0.5

    def compute(q_ref, k_ref, v_ref, cos_ref, sin_ref, dm_ref, qd_ref, kd_ref, cd_ref, o_ref, s_ref, S):
        t = pl.program_id(1)

        @pl.when(t == 0)
        def _():
            S[...] = jnp.zeros_like(S)

        cs, sn = cos_ref[...], sin_ref[...]
        q = _rope(q_ref[...], cs, sn)
        k = _rope(k_ref[...], cs, sn) * scale
        qb, kb, vb = q.astype(BF16), k.astype(BF16), v_ref[...].astype(BF16)
        A = _dot_t1(qb, kb) * dm_ref[0]
        s_in = S[...]
        s_ref[0, 0] = s_in
        o_ref[...] = _dot(A.astype(BF16), vb) + _dot(qb, s_in.astype(BF16)) * qd_ref[0]
        S[...] = s_in * cd_ref[0] + _dot_t0((k * kd_ref[0]).astype(BF16), vb)

    cmap = lambda t: _chunk_of(t, NC, n_ctx, rev)
    blk = lambda col: pl.BlockSpec((C, C), lambda h, t: (cmap(t), col * H + h))
    tab = pl.BlockSpec((C, C), lambda h, t: (cmap(t), 0))
    htab = lambda r: pl.BlockSpec((1, r, C), lambda h, t: (h, 0, 0))
    return _call(
        compute, name=name, grid=(H, NC),
        in_specs=[blk(0), blk(1), blk(2), tab, tab, htab(C), htab(C), htab(C), htab(1)],
        out_specs=[pl.BlockSpec((C, C), lambda h, t: (cmap(t), h)),
                   pl.BlockSpec((1, 1, C, C), lambda h, t: (h, cmap(t), 0, 0))],
        out_shape=[jax.ShapeDtypeStruct((T, H * C), F32), jax.ShapeDtypeStruct((H, NC, C, C), F32)],
        scratch=[pltpu.VMEM((C, C), F32)],
        operands=(p, p, p, cos, sin, tabs["dm"], tabs["qd"], tabs["kd"], tabs["cd"]),
        sem=("parallel", "arbitrary"))


def ret_bwd(p, cos, sin, tabs, do, s_saved, n_heads, n_ctx, rev, name, comm=None):
    T = p.shape[0]
    C = RET_CHUNK
    H = n_heads
    NC = T // C
    scale = C ** -0.5

    def compute(q_ref, k_ref, v_ref, cos_ref, sin_ref, dm_ref, qd_ref, kd_ref, cd_ref, em_ref, eq_ref,
                ek_ref, do_ref, s_ref, dq_ref, dk_ref, dv_ref, dlg_ref, dS):
        t = pl.program_id(1)

        @pl.when(t == 0)
        def _():
            dS[...] = jnp.zeros_like(dS)
            dlg_ref[...] = jnp.zeros_like(dlg_ref)

        cs, sn = cos_ref[...], sin_ref[...]
        q = _rope(q_ref[...], cs, sn)
        k = _rope(k_ref[...], cs, sn) * scale
        qb, kb, vb = q.astype(BF16), k.astype(BF16), v_ref[...].astype(BF16)
        dmv, qdv, kdv, cdv = dm_ref[0], qd_ref[0], kd_ref[0], cd_ref[0]
        A = _dot_t1(qb, kb) * dmv
        s_in = s_ref[0, 0]
        sb = s_in.astype(BF16)
        ds_out = dS[...]
        dsb = ds_out.astype(BF16)
        dov = do_ref[...]
        dob = dov.astype(BF16)
        dA = _dot_t1(dob, vb)
        dPb = (dA * dmv).astype(BF16)
        doq = dov * qdv
        doqb = doq.astype(BF16)
        kk = k * kdv
        vds = _dot_t1(vb, dsb)
        dq_ref[...] = _dot(dPb, kb) + _dot_t1(doqb, sb)
        dk_ref[...] = _dot_t0(dPb, qb) + vds * kdv
        dv_ref[...] = _dot_t0(A.astype(BF16), dob) + _dot(kk.astype(BF16), dsb)
        dS[...] = ds_out * cdv + _dot_t0(qb, doqb)
        o2 = _dot(qb, sb)
        part = (jnp.sum(dA * A * em_ref[...], axis=0, keepdims=True)
                + jnp.sum(eq_ref[...] * doq * o2, axis=0, keepdims=True)
                + jnp.sum(ek_ref[...] * kk * vds, axis=0, keepdims=True)
                + float(C) * cdv * jnp.sum(s_in * ds_out, axis=0, keepdims=True))
        dlg_ref[0, 0:1, :] += part

    cmap = lambda t: _chunk_of(NC - 1 - t, NC, n_ctx, rev)
    blk = lambda col: pl.BlockSpec((C, C), lambda h, t: (cmap(t), col * H + h))
    tab = pl.BlockSpec((C, C), lambda h, t: (cmap(t), 0))
    const = pl.BlockSpec((C, C), lambda h, t: (0, 0))
    htab = lambda r: pl.BlockSpec((1, r, C), lambda h, t: (h, 0, 0))
    hblk = pl.BlockSpec((C, C), lambda h, t: (cmap(t), h))
    return _call(
        compute, name=name, grid=(H, NC),
        in_specs=[blk(0), blk(1), blk(2), tab, tab, htab(C), htab(C), htab(C), htab(1), const, const,
                  const, hblk, pl.BlockSpec((1, 1, C, C), lambda h, t: (h, cmap(t), 0, 0))],
        out_specs=[hblk, hblk, hblk, pl.BlockSpec((1, 8, C), lambda h, t: (h, 0, 0))],
        out_shape=[jax.ShapeDtypeStruct((T, H * C), F32)] * 3 + [jax.ShapeDtypeStruct((H, 8, C), F32)],
        scratch=[pltpu.VMEM((C, C), F32)],
        operands=(p, p, p, cos, sin, tabs["dm"], tabs["qd"], tabs["kd"], tabs["cd"], tabs["em"], tabs["eq"],
                  tabs["ek"], do, s_saved),
        sem=("parallel", "arbitrary"), comm=comm)


def ret_out_fwd(o_f, o_b, p, n_heads, name):
    T, RW = o_f.shape
    C = RET_CHUNK

    def compute(of_ref, ob_ref, g_ref, out_ref):
        for h in range(n_heads):
            sl = slice(h * C, (h + 1) * C)
            o = of_ref[:, sl] + ob_ref[:, sl]
            r = o * lax.rsqrt(jnp.mean(o * o, axis=-1, keepdims=True) + EPS)
            g = g_ref[:, sl]
            out_ref[:, sl] = (g * _sigmoid(g) * r).astype(out_ref.dtype)

    return _call(compute, name=name, grid=(T // ROW_TILE,),
                 in_specs=[_row_spec(RW), _row_spec(RW), _row_spec(RW, 3)],
                 out_specs=[_row_spec(RW)], out_shape=[jax.ShapeDtypeStruct((T, RW), BF16)],
                 operands=(o_f, o_b, p), sem=("parallel",))[0]


def ret_out_bwd(o_f, o_b, p, dmix, n_heads, name):
    T, RW = o_f.shape
    C = RET_CHUNK

    def compute(of_ref, ob_ref, g_ref, d_ref, do_ref, dg_ref):
        for h in range(n_heads):
            sl = slice(h * C, (h + 1) * C)
            o = of_ref[:, sl] + ob_ref[:, sl]
            rstd = lax.rsqrt(jnp.mean(o * o, axis=-1, keepdims=True) + EPS)
            r = o * rstd
            g = g_ref[:, sl]
            sg = _sigmoid(g)
            d = d_ref[:, sl].astype(F32)
            dg_ref[:, sl] = (d * r * sg * (1.0 + g * (1.0 - sg))).astype(dg_ref.dtype)
            dr = d * g * sg
            do_ref[:, sl] = rstd * (dr - r * jnp.mean(dr * r, axis=-1, keepdims=True))

    return _call(compute, name=name, grid=(T // ROW_TILE,),
                 in_specs=[_row_spec(RW), _row_spec(RW), _row_spec(RW, 3), _row_spec(RW, 0)],
                 out_specs=[_row_spec(RW), _row_spec(RW)],
                 out_shape=[jax.ShapeDtypeStruct((T, RW), F32), jax.ShapeDtypeStruct((T, RW), BF16)],
                 operands=(o_f, o_b, p, dmix), sem=("parallel",))


def ret_qkv_grad(dqf, dqb, dkf, dkb, dvf, dvb, cos, sin, n_heads, name):
    T, RW = dqf.shape
    C = RET_CHUNK
    scale = C ** -0.5

    def compute(qf, qb, kf, kb, vf, vb, cos_ref, sin_ref, dq_ref, dk_ref, dv_ref):
        cs, sn = cos_ref[...], sin_ref[...]
        for h in range(n_heads):
            sl = slice(h * C, (h + 1) * C)
            dq_ref[:, sl] = _rope_t(qf[:, sl] + qb[:, sl], cs, sn).astype(dq_ref.dtype)
            dk_ref[:, sl] = _rope_t((kf[:, sl] + kb[:, sl]) * scale, cs, sn).astype(dk_ref.dtype)
            dv_ref[:, sl] = (vf[:, sl] + vb[:, sl]).astype(dv_ref.dtype)

    return _call(compute, name=name, grid=(T // ROW_TILE,),
                 in_specs=[_row_spec(RW)] * 6 + [_row_spec(C), _row_spec(C)],
                 out_specs=[_row_spec(RW)] * 3, out_shape=[jax.ShapeDtypeStruct((T, RW), BF16)] * 3,
                 operands=(dqf, dqb, dkf, dkb, dvf, dvb, cos, sin), sem=("parallel",))


def _halo_specs(width, col, halo, n_rows):
    per = ROW_TILE // halo
    last = n_rows // halo - 1
    return [pl.BlockSpec((halo, width), lambda i: (jnp.maximum(i * per - 1, 0), col)),
            pl.BlockSpec((ROW_TILE, width), lambda i: (i, col)),
            pl.BlockSpec((halo, width), lambda i: (jnp.minimum((i + 1) * per, last), col))]


def _halo_valid(i, n_tiles):
    return i >= 2, jnp.logical_and(i >= 1, i <= n_tiles - 2)


def conv_fwd(p, w, bias, ln_g, ln_b, name):
    T = p.shape[0]
    CW = w.shape[1]
    NT = T // ROW_TILE
    HL = CONV_HALO
    PAD = CONV_K // 2

    def compute(ap, ac, an, bp, bc, bn, w_ref, b_ref, g_ref, be_ref, u2_ref, out_ref, U):
        i = pl.program_id(0)
        vp, vn = _halo_valid(i, NT)
        U[0:HL, :] = jnp.where(vp, ap[...] * _sigmoid(bp[...]), 0.0)
        U[HL:HL + ROW_TILE, :] = ac[...] * _sigmoid(bc[...])
        U[HL + ROW_TILE:, :] = jnp.where(vn, an[...] * _sigmoid(bn[...]), 0.0)
        acc = jnp.zeros((ROW_TILE, CW), F32) + b_ref[...]
        for j in range(CONV_K):
            acc = acc + w_ref[j:j + 1, :] * U[pl.ds(HL - PAD + j, ROW_TILE), :]
        u2_ref[...] = acc
        mu = jnp.mean(acc, axis=-1, keepdims=True)
        xc = acc - mu
        rstd = lax.rsqrt(jnp.mean(xc * xc, axis=-1, keepdims=True) + EPS)
        ln = xc * rstd * g_ref[...] + be_ref[...]
        out_ref[...] = (ln * _sigmoid(ln)).astype(out_ref.dtype)

    vec = _acc_spec(1, CW)
    return _call(compute, name=name, grid=(NT,),
                 in_specs=_halo_specs(CW, 4, HL, T) + _halo_specs(CW, 5, HL, T) + [_acc_spec(CONV_K, CW), vec, vec, vec],
                 out_specs=[_row_spec(CW), _row_spec(CW)],
                 out_shape=[jax.ShapeDtypeStruct((T, CW), F32), jax.ShapeDtypeStruct((T, CW), BF16)],
                 scratch=[pltpu.VMEM((ROW_TILE + 2 * HL, CW), F32)],
                 operands=(p, p, p, p, p, p, w, bias, ln_g, ln_b), sem=("parallel",))


def conv_bwd_ln(u2, dmix, ln_g, ln_b, name):
    T, CW = u2.shape

    def compute(u_ref, d_ref, g_ref, be_ref, du_ref, dg_ref, db_ref, dbias_ref):
        i = pl.program_id(0)

        @pl.when(i == 0)
        def _():
            dg_ref[...] = jnp.zeros_like(dg_ref)
            db_ref[...] = jnp.zeros_like(db_ref)
            dbias_ref[...] = jnp.zeros_like(dbias_ref)

        u = u_ref[...]
        gv = g_ref[...]
        mu = jnp.mean(u, axis=-1, keepdims=True)
        xc = u - mu
        rstd = lax.rsqrt(jnp.mean(xc * xc, axis=-1, keepdims=True) + EPS)
        xh = xc * rstd
        ln = xh * gv + be_ref[...]
        sg = _sigmoid(ln)
        dln = d_ref[...].astype(F32) * sg * (1.0 + ln * (1.0 - sg))
        dg_ref[...] += jnp.sum(dln * xh, axis=0, keepdims=True)
        db_ref[...] += jnp.sum(dln, axis=0, keepdims=True)
        dxh = dln * gv
        du = rstd * (dxh - jnp.mean(dxh, axis=-1, keepdims=True)
                     - xh * jnp.mean(dxh * xh, axis=-1, keepdims=True))
        du_ref[...] = du
        dbias_ref[...] += jnp.sum(du, axis=0, keepdims=True)

    vec = _acc_spec(1, CW)
    return _call(compute, name=name, grid=(T // ROW_TILE,),
                 in_specs=[_row_spec(CW), _row_spec(CW, 1), vec, vec],
                 out_specs=[_row_spec(CW), vec, vec, vec],
                 out_shape=[jax.ShapeDtypeStruct((T, CW), F32)] + [jax.ShapeDtypeStruct((1, CW), F32)] * 3,
                 operands=(u2, dmix, ln_g, ln_b), sem=("arbitrary",))


def conv_bwd_taps(p, du2, w, name):
    T = p.shape[0]
    CW = w.shape[1]
    NT = T // ROW_TILE
    HL = CONV_HALO
    PAD = CONV_K // 2

    def compute(ap, ac, an, bp, bc, bn, dp, dc, dn, w_ref, da_ref, db_ref, dw_ref, U, DU):
        i = pl.program_id(0)

        @pl.when(i == 0)
        def _():
            dw_ref[...] = jnp.zeros_like(dw_ref)

        vp, vn = _halo_valid(i, NT)
        a = ac[...]
        sg = _sigmoid(bc[...])
        U[0:HL, :] = jnp.where(vp, ap[...] * _sigmoid(bp[...]), 0.0)
        U[HL:HL + ROW_TILE, :] = a * sg
        U[HL + ROW_TILE:, :] = jnp.where(vn, an[...] * _sigmoid(bn[...]), 0.0)
        d = dc[...]
        DU[0:HL, :] = jnp.where(vp, dp[...], 0.0)
        DU[HL:HL + ROW_TILE, :] = d
        DU[HL + ROW_TILE:, :] = jnp.where(vn, dn[...], 0.0)
        du = jnp.zeros((ROW_TILE, CW), F32)
        for j in range(CONV_K):
            du = du + w_ref[j:j + 1, :] * DU[pl.ds(HL + PAD - j, ROW_TILE), :]
            dw_ref[j:j + 1, :] += jnp.sum(d * U[pl.ds(HL - PAD + j, ROW_TILE), :], axis=0, keepdims=True)
        da_ref[...] = (du * sg).astype(da_ref.dtype)
        db_ref[...] = (du * a * sg * (1.0 - sg)).astype(db_ref.dtype)

    return _call(compute, name=name, grid=(NT,),
                 in_specs=(_halo_specs(CW, 4, HL, T) + _halo_specs(CW, 5, HL, T) + _halo_specs(CW, 0, HL, T)
                           + [_acc_spec(CONV_K, CW)]),
                 out_specs=[_row_spec(CW), _row_spec(CW), _acc_spec(CONV_K, CW)],
                 out_shape=[jax.ShapeDtypeStruct((T, CW), BF16), jax.ShapeDtypeStruct((T, CW), BF16),
                            jax.ShapeDtypeStruct((CONV_K, CW), F32)],
                 scratch=[pltpu.VMEM((ROW_TILE + 2 * HL, CW), F32)] * 2,
                 operands=(p, p, p, p, p, p, du2, du2, du2, w), sem=("arbitrary",))


def _ffn_halo_specs(tc, col0, n_rows):
    per = ROW_TILE // FFN_HALO
    last = n_rows // FFN_HALO - 1
    return [pl.BlockSpec((FFN_HALO, tc), lambda cb, i: (jnp.maximum(i * per - 1, 0), col0 + cb)),
            pl.BlockSpec((ROW_TILE, tc), lambda cb, i: (i, col0 + cb)),
            pl.BlockSpec((FFN_HALO, tc), lambda cb, i: (jnp.minimum((i + 1) * per, last), col0 + cb))]


def _ffn_taps(i, shape):
    is_lat = i >= 1
    col = lax.broadcasted_iota(jnp.int32, shape, 0) & (GRID_W - 1)
    not_first = jnp.logical_or(col >= 1, jnp.logical_not(is_lat))
    not_last = jnp.logical_or(col <= GRID_W - 2, jnp.logical_not(is_lat))
    lat = jnp.where(is_lat, 1.0, 0.0)
    taps = []
    for di in range(3):
        for dj in range(3):
            off = (di - 1) * GRID_W + (dj - 1)
            taps.append((off, 1.0 if di == 1 else lat, (not_first, None, not_last)[dj]))
    return taps


def ffn_conv_fwd(up, w9, bias, name):
    T = up.shape[0]
    DFF = w9.shape[1]
    tc = _pick(DFF, (512, 256, 128))
    ncb = DFF // tc
    NT = T // ROW_TILE
    HL = FFN_HALO

    def compute(gp, gc, gn, val_ref, w_ref, b_ref, cg_ref, act_ref, G):
        i = pl.program_id(1)
        vp, vn = _halo_valid(i, NT)
        G[0:HL, :] = jnp.where(vp, gp[...], 0.0)
        G[HL:HL + ROW_TILE, :] = gc[...]
        G[HL + ROW_TILE:, :] = jnp.where(vn, gn[...], 0.0)
        acc = jnp.zeros((ROW_TILE, tc), F32) + b_ref[...]
        for t, (off, fac, mask) in enumerate(_ffn_taps(i, (ROW_TILE, tc))):
            tap = G[pl.ds(HL + off, ROW_TILE), :]
            if mask is not None:
                tap = jnp.where(mask, tap, 0.0)
            acc = acc + (w_ref[t:t + 1, :] * fac) * tap
        cg_ref[...] = acc
        act_ref[...] = (acc * _sigmoid(acc) * val_ref[...]).astype(act_ref.dtype)

    tile = pl.BlockSpec((ROW_TILE, tc), lambda cb, i: (i, cb))
    return _call(compute, name=name, grid=(ncb, NT),
                 in_specs=_ffn_halo_specs(tc, 0, T) + [pl.BlockSpec((ROW_TILE, tc), lambda cb, i: (i, ncb + cb)),
                                                       pl.BlockSpec((9, tc), lambda cb, i: (0, cb)),
                                                       pl.BlockSpec((1, tc), lambda cb, i: (0, cb))],
                 out_specs=[tile, tile],
                 out_shape=[jax.ShapeDtypeStruct((T, DFF), F32), jax.ShapeDtypeStruct((T, DFF), BF16)],
                 scratch=[pltpu.VMEM((ROW_TILE + 2 * HL, tc), F32)],
                 operands=(up, up, up, up, w9, bias), sem=("parallel", "parallel"))


def ffn_conv_bwd_act(cg, up, dact, name):
    T, DFF = cg.shape
    tc = _pick(DFF, (512, 256, 128))
    ncb = DFF // tc

    def compute(cg_ref, val_ref, d_ref, dcg_ref, dval_ref, db_ref):
        i = pl.program_id(1)

        @pl.when(i == 0)
        def _():
            db_ref[...] = jnp.zeros_like(db_ref)

        c = cg_ref[...]
        sg = _sigmoid(c)
        d = d_ref[...].astype(F32)
        dval_ref[...] = (d * c * sg).astype(dval_ref.dtype)
        dcg = d * val_ref[...] * sg * (1.0 + c * (1.0 - sg))
        dcg_ref[...] = dcg
        db_ref[...] += jnp.sum(dcg, axis=0, keepdims=True)

    tile = pl.BlockSpec((ROW_TILE, tc), lambda cb, i: (i, cb))
    return _call(compute, name=name, grid=(ncb, T // ROW_TILE),
                 in_specs=[tile, pl.BlockSpec((ROW_TILE, tc), lambda cb, i: (i, ncb + cb)), tile],
                 out_specs=[tile, tile, pl.BlockSpec((1, tc), lambda cb, i: (0, cb))],
                 out_shape=[jax.ShapeDtypeStruct((T, DFF), F32), jax.ShapeDtypeStruct((T, DFF), BF16),
                            jax.ShapeDtypeStruct((1, DFF), F32)],
                 operands=(cg, up, dact), sem=("parallel", "arbitrary"))


def ffn_conv_bwd_taps(up, dcg, w9, name, comm=None):
    T, DFF = dcg.shape
    tc = _pick(DFF, (512, 256, 128))
    ncb = DFF // tc
    NT = T // ROW_TILE
    HL = FFN_HALO

    def compute(gp, gc, gn, dp, dc, dn, w_ref, dgate_ref, dw_ref, G, DC):
        i = pl.program_id(1)

        @pl.when(i == 0)
        def _():
            dw_ref[...] = jnp.zeros_like(dw_ref)

        vp, vn = _halo_valid(i, NT)
        is_lat = i >= 1
        G[0:HL, :] = jnp.where(vp, gp[...], 0.0)
        G[HL:HL + ROW_TILE, :] = gc[...]
        G[HL + ROW_TILE:, :] = jnp.where(vn, gn[...], 0.0)
        d = dc[...]
        DC[0:HL, :] = jnp.where(vp, dp[...], 0.0)
        DC[HL:HL + ROW_TILE, :] = d
        DC[HL + ROW_TILE:, :] = jnp.where(vn, dn[...], 0.0)
        row = lax.broadcasted_iota(jnp.int32, (ROW_TILE, tc), 0)
        dg = jnp.zeros((ROW_TILE, tc), F32)
        for t, (off, fac, mask) in enumerate(_ffn_taps(i, (ROW_TILE, tc))):
            src = DC[pl.ds(HL - off, ROW_TILE), :]
            dm = d
            if mask is not None:
                out_col = (row - off) & (GRID_W - 1)
                ok = (out_col >= 1) if off % GRID_W == GRID_W - 1 else (out_col <= GRID_W - 2)
                src = jnp.where(jnp.logical_or(ok, jnp.logical_not(is_lat)), src, 0.0)
                dm = jnp.where(mask, d, 0.0)
            dg = dg + (w_ref[t:t + 1, :] * fac) * src
            dw_ref[t:t + 1, :] += fac * jnp.sum(dm * G[pl.ds(HL + off, ROW_TILE), :], axis=0, keepdims=True)
        dgate_ref[...] = dg.astype(dgate_ref.dtype)

    tile = pl.BlockSpec((ROW_TILE, tc), lambda cb, i: (i, cb))
    return _call(compute, name=name, grid=(ncb, NT),
                 in_specs=_ffn_halo_specs(tc, 0, T) + _ffn_halo_specs(tc, 0, T) + [pl.BlockSpec((9, tc), lambda cb, i: (0, cb))],
                 out_specs=[tile, pl.BlockSpec((9, tc), lambda cb, i: (0, cb))],
                 out_shape=[jax.ShapeDtypeStruct((T, DFF), BF16), jax.ShapeDtypeStruct((9, DFF), F32)],
                 scratch=[pltpu.VMEM((ROW_TILE + 2 * HL, tc), F32)] * 2,
                 operands=(up, up, up, dcg, dcg, dcg, w9), sem=("parallel", "arbitrary"), comm=comm)


def _adamw_update(g, w_ref, m_ref, v_ref, g_ref, d_ref, nm_ref, nv_ref):
    c1 = 1.0 - ADAM_B1 ** ADAM_STEP
    c2 = 1.0 - ADAM_B2 ** ADAM_STEP
    nm = ADAM_B1 * m_ref[...] + (1.0 - ADAM_B1) * g
    nv = ADAM_B2 * v_ref[...] + (1.0 - ADAM_B2) * (g * g)
    g_ref[...] = g
    nm_ref[...] = nm
    nv_ref[...] = nv
    d_ref[...] = -ADAM_LR * ((nm / c1) / (jnp.sqrt(nv / c2) + ADAM_EPS) + ADAM_WD * w_ref[...])


def _sum_parts(p_ref, P):
    g = p_ref[0].astype(F32)
    for k in range(1, P):
        g = g + p_ref[k].astype(F32)
    return g


def adamw(parts, w, m, v, name):
    P, R, C = parts.shape
    fits = lambda t: 2 * (P + 7) * t * C * 4 <= ADAM_VMEM_BYTES
    tr = R if fits(R) else _pick(R, [t for t in (1024, 512, 256, 128, 64, 32, 16, 8) if fits(t)])

    def compute(p_ref, w_ref, m_ref, v_ref, g_ref, d_ref, nm_ref, nv_ref):
        _adamw_update(_sum_parts(p_ref, P), w_ref, m_ref, v_ref, g_ref, d_ref, nm_ref, nv_ref)

    tile = pl.BlockSpec((tr, C), lambda i: (i, 0))
    return _call(compute, name=name, grid=(R // tr,),
                 in_specs=[pl.BlockSpec((P, tr, C), lambda i: (0, i, 0)), tile, tile, tile],
                 out_specs=[tile] * 4, out_shape=[jax.ShapeDtypeStruct((R, C), F32)] * 4,
                 operands=(parts, w, m, v), sem=("parallel",))


def adamw_layers(parts_l, w, m, v, name):
    L = len(parts_l)
    P, R, C = parts_l[0].shape
    psize = parts_l[0].dtype.itemsize
    fits = lambda t: 2 * t * C * (L * P * psize + 7 * 4) <= ADAM_VMEM_BYTES
    tr = _pick(R, [t for t in (1024, 512, 256, 128, 64, 32, 16, 8) if fits(t)])

    def compute(*refs):
        p_refs, (w_ref, m_ref, v_ref), outs = refs[:L], refs[L:L + 3], refs[L + 3:]
        for l in range(L):
            @pl.when(pl.program_id(0) == l)
            def _(l=l):
                _adamw_update(_sum_parts(p_refs[l], P), w_ref, m_ref, v_ref, *outs)

    tile = pl.BlockSpec((None, tr, C), lambda l, i: (l, i, 0))
    part = lambda k: pl.BlockSpec((P, tr, C), lambda l, i: (0, jnp.where(l == k, i, 0), 0))
    return _call(compute, name=name, grid=(L, R // tr),
                 in_specs=[part(k) for k in range(L)] + [tile] * 3,
                 out_specs=[tile] * 4, out_shape=[jax.ShapeDtypeStruct((L, R, C), F32)] * 4,
                 operands=(*parts_l, w, m, v), sem=("arbitrary", "arbitrary"))


def _rope_tables(seq, ctx):
    t = jnp.arange(seq)
    quarter = RET_CHUNK // 4
    inv_freq = 1.0 / (ROPE_THETA ** (jnp.arange(0, quarter, dtype=F32) / quarter))
    ang_r = (t // GRID_W).astype(F32)[:, None] * inv_freq[None, :]
    ang_c = (t % GRID_W).astype(F32)[:, None] * inv_freq[None, :]
    cr, sr, cc, sc = jnp.cos(ang_r), jnp.sin(ang_r), jnp.cos(ang_c), jnp.sin(ang_c)
    cos = jnp.concatenate([cr, cr, cc, cc], axis=-1)
    sin = jnp.concatenate([-sr, sr, -sc, sc], axis=-1)
    cos = jnp.concatenate([jnp.ones((ctx, RET_CHUNK), F32), cos], axis=0)
    sin = jnp.concatenate([jnp.zeros((ctx, RET_CHUNK), F32), sin], axis=0)
    return cos, sin


def _decay_tables(decay_logit, rev):
    C = RET_CHUNK
    lg = jax.nn.log_sigmoid(decay_logit.astype(F32))
    idx = jnp.arange(C, dtype=F32)
    diff = idx[:, None] - idx[None, :]
    if rev:
        diff = -diff
        eq, ek = C - idx, idx
    else:
        eq, ek = idx + 1.0, C - 1.0 - idx
    keep = diff >= 0
    em = jnp.where(keep, diff, 0.0)
    bc = lambda e: jnp.broadcast_to(e[:, None], (C, C))
    return {
        "dm": jnp.where(keep[None], jnp.exp(lg[:, None, None] * em[None]), 0.0),
        "qd": jnp.broadcast_to(jnp.exp(lg[:, None] * eq[None, :])[:, :, None], (lg.shape[0], C, C)),
        "kd": jnp.broadcast_to(jnp.exp(lg[:, None] * ek[None, :])[:, :, None], (lg.shape[0], C, C)),
        "cd": jnp.broadcast_to(jnp.exp(lg * C)[:, None, None], (lg.shape[0], 1, C)),
        "em": em, "eq": bc(eq), "ek": bc(ek),
    }


def _silu(z):
    return z * jax.nn.sigmoid(z)


def _dsilu(z):
    s = jax.nn.sigmoid(z)
    return s * (1.0 + z * (1.0 - s))


def kernel(x, c, ctx, c_ctx, w_mod, b_mod, norm1_g, norm2_g, w_in, ret_decay_f, ret_decay_b, conv_dw_w, conv_dw_b, conv_ln_g, conv_ln_b, w_out, ffn_w_up, ffn_dw_w, ffn_dw_b, ffn_w_down, final_norm_g, loss_target, m_c_ctx, m_w_mod, m_b_mod, m_norm1_g, m_norm2_g, m_w_in, m_ret_decay_f, m_ret_decay_b, m_conv_dw_w, m_conv_dw_b, m_conv_ln_g, m_conv_ln_b, m_w_out, m_ffn_w_up, m_ffn_dw_w, m_ffn_dw_b, m_ffn_w_down, m_final_norm_g, v_c_ctx, v_w_mod, v_b_mod, v_norm1_g, v_norm2_g, v_w_in, v_ret_decay_f, v_ret_decay_b, v_conv_dw_w, v_conv_dw_b, v_conv_ln_g, v_conv_ln_b, v_w_out, v_ffn_w_up, v_ffn_dw_w, v_ffn_dw_b, v_ffn_w_down, v_final_norm_g):
    L, D, _ = w_mod.shape
    SEQ, CTX = x.shape[1], ctx.shape[1]
    T = SEQ + CTX
    RW = D // 2
    CW = D - RW
    H = RW // RET_CHUNK
    DFF = ffn_dw_b.shape[1]
    NMOD = b_mod.shape[1] // D
    n_ctx = CTX // RET_CHUNK
    assert CTX == ROW_TILE and RW == CW and SEQ % ROW_TILE == 0 and NMOD == 6
    me = _my_rank()
    wm_n = w_mod.shape[2]
    wo_k, wd_k = w_out.shape[1], ffn_w_down.shape[1]
    cw_n, fw_n = conv_dw_w.shape[2], ffn_dw_w.shape[3]

    w_mod_b = w_mod.astype(BF16)
    w_in_b, w_out_b, w_up_b, w_down_b = (a.astype(BF16) for a in (w_in, w_out, ffn_w_up, ffn_w_down))
    as_rows = lambda g: g.reshape(1, -1, D)
    g_in, g_out, g_up, g_down, g_cw, g_fw, g_c = run_comm(
        "gather", [w_in_b[0], w_out_b[0], w_up_b[0], w_down_b[0], conv_dw_w, ffn_dw_w, _silu(c)], "gather_first")
    w_in_l, w_up_l, w_out_l, w_down_l = [g_in], [g_up], [as_rows(g_out)], [as_rows(g_down)]
    conv_w_l = [jnp.moveaxis(g_cw[:, l], 0, 1).reshape(CONV_K, CW) for l in range(L)]
    ffn_w9_l = [jnp.moveaxis(g_fw[:, l], 0, 2).reshape(9, DFF) for l in range(L)]

    s_cond = jnp.concatenate([g_c.reshape(N_DEV, D), jnp.broadcast_to(_silu(c_ctx)[None], (N_DEV, D))], axis=0)
    s_cond_b = s_cond.astype(BF16)
    mod_shard = mm_nn(s_cond_b, w_mod_b, F32, "mod_fwd")[0]
    (g_mod,) = run_comm("gather", [mod_shard], "gather_mod")
    mod_all = jnp.transpose(g_mod.reshape(N_DEV, 2 * N_DEV, L, wm_n), (2, 1, 0, 3)).reshape(L, 2 * N_DEV, NMOD * D)
    mod_all = mod_all + b_mod[:, None, :]
    mod_lat = lax.dynamic_index_in_dim(mod_all, me, axis=1, keepdims=False)
    mod_ctx = mod_all[:, N_DEV]
    mod2 = jnp.stack([mod_ctx, mod_lat], axis=1).reshape(L, 2, NMOD, D)

    cos, sin = _rope_tables(SEQ, CTX)
    xs = jnp.concatenate([ctx[0], x[0]], axis=0)

    saved = []
    for l in range(L):
        nxt = l + 1 < L
        sh1, sc1, g1, sh2, sc2, g2 = (mod2[l, :, k] for k in range(NMOD))
        tf = _decay_tables(ret_decay_f[l], False)
        tb = _decay_tables(ret_decay_b[l], True)
        h = rms_mod_fwd(xs, norm1_g[l][None], sh1, sc1, "norm1_fwd")
        if nxt:
            (p,), (gi, go) = mm_nn(h, w_in_l[l], F32, "in_proj_g", Comm("gather", [w_in_b[l + 1], w_out_b[l + 1]]))
            w_in_l.append(gi)
            w_out_l.append(as_rows(go))
        else:
            (p,) = mm_nn(h, w_in_l[l], F32, "in_proj")
        o_f, s_f = ret_fwd(p, cos, sin, tf, H, n_ctx, False, "ret_fwd_f")
        o_b, s_b = ret_fwd(p, cos, sin, tb, H, n_ctx, True, "ret_fwd_b")
        mix_r = ret_out_fwd(o_f, o_b, p, H, "ret_out_fwd")
        u2, mix_c = conv_fwd(p, conv_w_l[l], conv_dw_b[l][None], conv_ln_g[l][None], conv_ln_b[l][None], "conv_fwd")
        mix = jnp.concatenate([mix_r, mix_c], axis=1)
        (y1,) = mm_nn(mix, w_out_l[l], F32, "out_proj")
        x2 = gate_res_fwd(xs, y1, g1, "res1_fwd")
        h2 = rms_mod_fwd(x2, norm2_g[l][None], sh2, sc2, "norm2_fwd")
        if nxt:
            (up,), (gu,) = mm_nn(h2, w_up_l[l], F32, "ffn_up_g", Comm("gather", [w_up_b[l + 1]]))
            w_up_l.append(gu)
        else:
            (up,) = mm_nn(h2, w_up_l[l], F32, "ffn_up")
        cg, act = ffn_conv_fwd(up, ffn_w9_l[l], ffn_dw_b[l][None], "ffn_conv_fwd")
        if nxt:
            (y2,), (gd,) = mm_nn(act, w_down_l[l], F32, "ffn_down_g", Comm("gather", [w_down_b[l + 1]]))
            w_down_l.append(as_rows(gd))
        else:
            (y2,) = mm_nn(act, w_down_l[l], F32, "ffn_down")
        x3 = gate_res_fwd(x2, y2, g2, "res2_fwd")
        saved.append(dict(x1=xs, h=h, p=p, o_f=o_f, o_b=o_b, s_f=s_f, s_b=s_b, u2=u2, mix=mix, y1=y1, x2=x2,
                          h2=h2, up=up, cg=cg, act=act, y2=y2, tf=tf, tb=tb))
        xs = x3

    dxs, d_final_g, loss_part = final_loss(xs, final_norm_g[None], loss_target[0], "final_loss")
    loss = lax.psum(loss_part[0, 0], ("x", "y", "c"))

    landed = {n: [None] * L for n in ("w_in", "w_out", "ffn_w_up", "ffn_w_down")}
    small = {n: [None] * L for n in ("norm1_g", "norm2_g", "ret_decay_f", "ret_decay_b", "conv_dw_w", "conv_dw_b",
                                     "conv_ln_g", "conv_ln_b", "ffn_dw_w", "ffn_dw_b")}
    dmod2 = [None] * L
    g_in_prev = None
    for l in reversed(range(L)):
        sv = saved[l]
        sh1, sc1, g1, sh2, sc2, g2 = (mod2[l, :, k] for k in range(NMOD))
        dy2, dg2 = gate_bwd(dxs, sv["y2"], g2, "res2_bwd")
        if g_in_prev is not None:
            (dact,), (landed["w_in"][l + 1],) = mm_nt(dy2, w_down_l[l], F32, "ffn_down_dx_x", Comm("exchange", [g_in_prev]))
        else:
            (dact,) = mm_nt(dy2, w_down_l[l], F32, "ffn_down_dx")
        g_down = mm_tn(sv["act"], dy2, 1, BF16, "ffn_down_dw").reshape(N_DEV, wd_k, D)
        dcg, dval, small["ffn_dw_b"][l] = ffn_conv_bwd_act(sv["cg"], sv["up"], dact, "ffn_conv_bwd_act")
        (dgate, small["ffn_dw_w"][l]), (landed["ffn_w_down"][l],) = ffn_conv_bwd_taps(
            sv["up"], dcg, ffn_w9_l[l], "ffn_conv_bwd_taps", Comm("exchange", [g_down]))
        dup = jnp.concatenate([dgate, dval], axis=1)
        (dh2,) = mm_nt(dup, w_up_l[l], F32, "ffn_up_dx")
        g_up = mm_tn(sv["h2"], dup, N_DEV, BF16, "ffn_up_dw")
        dx2, small["norm2_g"][l], dsh2, dsc2 = rms_mod_bwd(sv["x2"], norm2_g[l][None], sc2, dh2, dxs, "norm2_bwd")
        dy1, dg1 = gate_bwd(dx2, sv["y1"], g1, "res1_bwd")
        (dmix,) = mm_nt(dy1, w_out_l[l], BF16, "out_proj_dx")
        g_out = mm_tn(sv["mix"], dy1, 1, BF16, "out_proj_dw").reshape(N_DEV, wo_k, D)
        do, dgt = ret_out_bwd(sv["o_f"], sv["o_b"], sv["p"], dmix, H, "ret_out_bwd")
        (dqf, dkf, dvf, dlg_f), (landed["ffn_w_up"][l],) = ret_bwd(
            sv["p"], cos, sin, sv["tf"], do, sv["s_f"], H, n_ctx, False, "ret_bwd_f", Comm("exchange", [g_up]))
        (dqb, dkb, dvb, dlg_b), (landed["w_out"][l],) = ret_bwd(
            sv["p"], cos, sin, sv["tb"], do, sv["s_b"], H, n_ctx, True, "ret_bwd_b", Comm("exchange", [g_out]))
        dq, dk, dv = ret_qkv_grad(dqf, dqb, dkf, dkb, dvf, dvb, cos, sin, H, "ret_qkv_grad")
        small["ret_decay_f"][l] = jnp.sum(dlg_f[:, 0, :], axis=-1) * jax.nn.sigmoid(-ret_decay_f[l])
        small["ret_decay_b"][l] = jnp.sum(dlg_b[:, 0, :], axis=-1) * jax.nn.sigmoid(-ret_decay_b[l])
        du2, small["conv_ln_g"][l], small["conv_ln_b"][l], small["conv_dw_b"][l] = conv_bwd_ln(
            sv["u2"], dmix, conv_ln_g[l][None], conv_ln_b[l][None], "conv_bwd_ln")
        da, dbg, small["conv_dw_w"][l] = conv_bwd_taps(sv["p"], du2, conv_w_l[l], "conv_bwd_taps")
        dp = jnp.concatenate([dq, dk, dv, dgt, da, dbg], axis=1)
        (dh,) = mm_nt(dp, w_in_l[l], F32, "in_proj_dx")
        g_in_prev = mm_tn(sv["h"], dp, N_DEV, BF16, "in_proj_dw")
        dxs, small["norm1_g"][l], dsh1, dsc1 = rms_mod_bwd(sv["x1"], norm1_g[l][None], sc1, dh, dx2, "norm1_bwd")
        dmod2[l] = jnp.concatenate([dsh1, dsc1, dg1, dsh2, dsc2, dg2], axis=1)
    (landed["w_in"][0],) = run_comm("exchange", [g_in_prev], "exchange_last")

    grad_x = dxs[CTX:][None]

    dmod2 = jnp.stack(dmod2)
    (g_dmod,) = run_comm("gather", [dmod2], "gather_dmod")
    dmod_all = jnp.concatenate([jnp.moveaxis(g_dmod[:, :, 1], 0, 1), jnp.moveaxis(g_dmod[:, :, 0], 0, 1)], axis=1)
    dmod_sh = lax.dynamic_slice_in_dim(dmod_all, me * wm_n, wm_n, axis=2)
    dmod_sh = jnp.moveaxis(dmod_sh, 0, 1).reshape(2 * N_DEV, L * wm_n).astype(BF16)
    g_w_mod = mm_tn(s_cond_b, dmod_sh, L, F32, "mod_dw")
    (d_cond,) = mm_nt(dmod_sh, w_mod_b, F32, "mod_dx")
    g_c_ctx_part = jnp.sum(d_cond[N_DEV:], axis=0) * _dsilu(c_ctx)
    g_b_mod_part = dmod2[:, 0] + dmod2[:, 1]

    pad128 = lambda a: jnp.pad(a.reshape(-1), (0, (-a.size) % 128))
    rep_names = ["c_ctx", "b_mod", "norm1_g", "norm2_g", "ret_decay_f", "ret_decay_b", "conv_dw_b", "conv_ln_g",
                 "conv_ln_b", "ffn_dw_b", "final_norm_g"]
    given = dict(c_ctx=(c_ctx, m_c_ctx, v_c_ctx), b_mod=(b_mod, m_b_mod, v_b_mod),
                 norm1_g=(norm1_g, m_norm1_g, v_norm1_g), norm2_g=(norm2_g, m_norm2_g, v_norm2_g),
                 ret_decay_f=(ret_decay_f, m_ret_decay_f, v_ret_decay_f),
                 ret_decay_b=(ret_decay_b, m_ret_decay_b, v_ret_decay_b),
                 conv_dw_b=(conv_dw_b, m_conv_dw_b, v_conv_dw_b), conv_ln_g=(conv_ln_g, m_conv_ln_g, v_conv_ln_g),
                 conv_ln_b=(conv_ln_b, m_conv_ln_b, v_conv_ln_b), ffn_dw_b=(ffn_dw_b, m_ffn_dw_b, v_ffn_dw_b),
                 final_norm_g=(final_norm_g, m_final_norm_g, v_final_norm_g))
    rep_part = dict(c_ctx=g_c_ctx_part, b_mod=g_b_mod_part, final_norm_g=d_final_g)
    for nme in rep_names:
        if nme not in rep_part:
            rep_part[nme] = jnp.stack([a.reshape(-1) for a in small[nme]])
    rep_sizes = [((-given[nme][0].size) % 128) + given[nme][0].size for nme in rep_names]
    n_rep = sum(rep_sizes)
    cw_part = jnp.stack(small["conv_dw_w"])
    fw_part = jnp.stack(small["ffn_dw_w"])
    packed = jnp.concatenate([pad128(rep_part[nme]) for nme in rep_names] + [cw_part.reshape(-1), fw_part.reshape(-1)])
    (g_small,) = run_comm("gather", [packed.reshape(-1, 128)], "gather_small")
    g_small = g_small.reshape(N_DEV, -1)
    rep_w, rep_m, rep_v = (jnp.concatenate([pad128(given[nme][k]) for nme in rep_names]).reshape(-1, 128) for k in range(3))
    rep_out = adamw(g_small[:, :n_rep].reshape(N_DEV, -1, 128), rep_w, rep_m, rep_v, "adamw_small")
    res = {}
    off = 0
    for nme, sz in zip(rep_names, rep_sizes):
        shape = given[nme][0].shape
        res[nme] = [o.reshape(-1)[off:off + given[nme][0].size].reshape(shape) for o in rep_out]
        off += sz

    cw_all = g_small[:, n_rep:n_rep + cw_part.size].reshape(N_DEV, L * CONV_K, CW)
    cw_mine = lax.dynamic_slice_in_dim(cw_all, me * cw_n, cw_n, axis=2)
    res["conv_dw_w"] = [o.reshape(conv_dw_w.shape) for o in adamw(
        cw_mine, conv_dw_w.reshape(L * CONV_K, cw_n), m_conv_dw_w.reshape(L * CONV_K, cw_n),
        v_conv_dw_w.reshape(L * CONV_K, cw_n), "adamw_conv_w")]
    fw_all = g_small[:, n_rep + cw_part.size:].reshape(N_DEV, L * 9, DFF)
    fw_mine = lax.dynamic_slice_in_dim(fw_all, me * fw_n, fw_n, axis=2)
    res["ffn_dw_w"] = [o.reshape(ffn_dw_w.shape) for o in adamw(
        fw_mine, ffn_dw_w.reshape(L * 9, fw_n), m_ffn_dw_w.reshape(L * 9, fw_n),
        v_ffn_dw_w.reshape(L * 9, fw_n), "adamw_ffn_w")]

    res["w_mod"] = [o.reshape(w_mod.shape) for o in adamw(
        g_w_mod.reshape(1, L * D, wm_n), w_mod.reshape(L * D, wm_n), m_w_mod.reshape(L * D, wm_n),
        v_w_mod.reshape(L * D, wm_n), "adamw_w_mod")]

    res["w_in"] = adamw_layers(landed["w_in"], w_in, m_w_in, v_w_in, "adamw_w_in")
    res["w_out"] = adamw_layers(landed["w_out"], w_out, m_w_out, v_w_out, "adamw_w_out")
    res["ffn_w_up"] = adamw_layers(landed["ffn_w_up"], ffn_w_up, m_ffn_w_up, v_ffn_w_up, "adamw_ffn_w_up")
    res["ffn_w_down"] = adamw_layers(landed["ffn_w_down"], ffn_w_down, m_ffn_w_down, v_ffn_w_down, "adamw_ffn_w_down")

    order = ["c_ctx", "w_mod", "b_mod", "norm1_g", "norm2_g", "w_in", "ret_decay_f", "ret_decay_b", "conv_dw_w",
             "conv_dw_b", "conv_ln_g", "conv_ln_b", "w_out", "ffn_w_up", "ffn_dw_w", "ffn_dw_b", "ffn_w_down",
             "final_norm_g"]
    return (loss, grad_x, *[res[nme][0] for nme in order], *[res[nme][1] for nme in order],
            *[res[nme][2] for nme in order], *[res[nme][3] for nme in order])
```

```python
import functools

import jax
import jax.numpy as jnp
from jax import lax
from jax.experimental import pallas as pl
from jax.experimental.pallas import tpu as pltpu

F32 = jnp.float32
BF16 = jnp.bfloat16
EPS = 1e-6
N_DEV = 8
ROW_TILE = 256
RET_CHUNK = 128
GRID_W = 64
CONV_K = 31
CONV_HALO = 16
CONV_ROWS = 32
FFN_ROWS = 64
FFN_HALO = 128
ROPE_THETA = 10000.0
ADAM_LR = 0.001
ADAM_B1 = 0.9
ADAM_B2 = 0.999
ADAM_EPS = 1e-08
ADAM_WD = 0.01
ADAM_STEP = 10
VMEM_LIMIT = 56 * 1024 * 1024
ADAM_VMEM_BYTES = 24 * 1024 * 1024
MESH = pl.DeviceIdType.MESH
ANY = pl.BlockSpec(memory_space=pl.ANY)


def _pick(n, cands):
    for t in cands:
        if n % t == 0:
            return t
    return n


def _sigmoid(z):
    return 1.0 / (1.0 + jnp.exp(-z))


def _my_rank():
    return 4 * lax.axis_index("x") + 2 * lax.axis_index("y") + lax.axis_index("c")


def _peer(j):
    x, y, c = lax.axis_index("x"), lax.axis_index("y"), lax.axis_index("c")
    px = 1 - x if j & 4 else x
    py = 1 - y if j & 2 else y
    pc = 1 - c if j & 1 else c
    return (px, py, pc), 4 * px + 2 * py + pc


class Comm:
    def __init__(self, kind, arrs):
        assert kind in ("gather", "exchange")
        self.kind, self.arrs, self.n = kind, list(arrs), len(arrs)
        self.in_specs = [ANY] * self.n
        self.out_specs = [ANY] * self.n
        lead = (N_DEV,) if kind == "gather" else ()
        self.out_shape = [jax.ShapeDtypeStruct(lead + a.shape, a.dtype) for a in self.arrs]
        per = self.n * (N_DEV - 1)
        self.scratch = [pltpu.SemaphoreType.DMA((per,)), pltpu.SemaphoreType.DMA((per,)),
                        pltpu.SemaphoreType.DMA((self.n,))]

    def _src(self, ref, rank):
        return ref if self.kind == "gather" else ref.at[rank]

    def _local(self, ins, outs, sems, a):
        me = _my_rank()
        return pltpu.make_async_copy(self._src(ins[a], me), outs[a].at[me], sems[2].at[a])

    def _remote(self, ins, outs, sems, a, j, receive):
        dev, rank = _peer(j)
        s = a * (N_DEV - 1) + j - 1
        slot = rank if receive else _my_rank()
        return pltpu.make_async_remote_copy(src_ref=self._src(ins[a], rank), dst_ref=outs[a].at[slot],
                                            send_sem=sems[0].at[s], recv_sem=sems[1].at[s],
                                            device_id=dev, device_id_type=MESH)

    def start(self, ins, outs, sems):
        for a in range(self.n):
            self._local(ins, outs, sems, a).start()
        for a in range(self.n):
            for j in range(1, N_DEV):
                self._remote(ins, outs, sems, a, j, False).start()

    def wait(self, ins, outs, sems):
        for a in range(self.n):
            for j in range(1, N_DEV):
                cp = self._remote(ins, outs, sems, a, j, True)
                cp.wait_recv()
                cp.wait_send()
        for a in range(self.n):
            self._local(ins, outs, sems, a).wait()


def _call(compute, *, name, grid, in_specs, out_specs, out_shape, operands, sem, scratch=(), comm=None):
    n_in, n_out, n_sc = len(in_specs), len(out_specs), len(scratch)
    k = comm.n if comm else 0

    def body(*refs):
        ins, cin = refs[:n_in], refs[n_in:n_in + k]
        o0 = n_in + k
        outs, cout = refs[o0:o0 + n_out], refs[o0 + n_out:o0 + n_out + k]
        s0 = o0 + n_out + k
        sc, sems = refs[s0:s0 + n_sc], refs[s0 + n_sc:]
        if comm:
            ids = [pl.program_id(d) for d in range(len(grid))]
            first = functools.reduce(jnp.logical_and, [i == 0 for i in ids])
            last = functools.reduce(jnp.logical_and, [i == g - 1 for i, g in zip(ids, grid)])

            @pl.when(first)
            def _():
                comm.start(cin, cout, sems)

        compute(*ins, *outs, *sc)

        if comm:
            @pl.when(last)
            def _():
                comm.wait(cin, cout, sems)

    semantics = ("arbitrary",) * len(grid) if comm else sem
    res = pl.pallas_call(
        body, name=name, grid=grid,
        in_specs=list(in_specs) + (comm.in_specs if comm else []),
        out_specs=list(out_specs) + (comm.out_specs if comm else []),
        out_shape=list(out_shape) + (comm.out_shape if comm else []),
        scratch_shapes=list(scratch) + (comm.scratch if comm else []),
        compiler_params=pltpu.CompilerParams(dimension_semantics=semantics, vmem_limit_bytes=VMEM_LIMIT),
    )(*operands, *(comm.arrs if comm else []))
    if comm:
        return list(res[:n_out]), list(res[n_out:])
    return list(res)


def run_comm(kind, arrs, name):
    comm = Comm(kind, arrs)

    def body(*refs):
        ins, outs, sems = refs[:comm.n], refs[comm.n:2 * comm.n], refs[2 * comm.n:]
        comm.start(ins, outs, sems)
        comm.wait(ins, outs, sems)

    return list(pl.pallas_call(body, name=name, in_specs=comm.in_specs, out_specs=comm.out_specs,
                               out_shape=comm.out_shape, scratch_shapes=comm.scratch)(*comm.arrs))


M_TILES = (768, 512, 256, 128)
WIDE_TILES = (2048, 1408, 1024, 768, 512, 256, 128)
MID_TILES = (1408, 1024, 768, 512, 256, 128)


def _mm_body(dot, n_steps, axis):
    if n_steps == 1:
        def compute(a_ref, b_ref, o_ref):
            o_ref[...] = dot(a_ref, b_ref).astype(o_ref.dtype).reshape(o_ref.shape)
        return compute, []

    def compute(a_ref, b_ref, o_ref, acc):
        k = pl.program_id(axis)

        @pl.when(k == 0)
        def _():
            acc[...] = jnp.zeros_like(acc)

        acc[...] += dot(a_ref, b_ref)

        @pl.when(k == n_steps - 1)
        def _():
            o_ref[...] = acc[...].astype(o_ref.dtype).reshape(o_ref.shape)

    return compute, None


def mm_nn(a, b3, out_dtype, name, comm=None):
    M, K = a.shape
    R, _, n = b3.shape
    tm, tk, tn = _pick(M, M_TILES), _pick(K, WIDE_TILES), _pick(n, MID_TILES)
    nb, nk = n // tn, K // tk
    compute, scratch = _mm_body(lambda a_ref, b_ref: jnp.dot(a_ref[...], b_ref[0], preferred_element_type=F32), nk, 2)
    return _call(
        compute, name=name, grid=(M // tm, R * nb, nk),
        in_specs=[pl.BlockSpec((tm, tk), lambda i, j, k: (i, k)),
                  pl.BlockSpec((1, tk, tn), lambda i, j, k: (j // nb, k, j % nb))],
        out_specs=[pl.BlockSpec((tm, tn), lambda i, j, k: (i, j))],
        out_shape=[jax.ShapeDtypeStruct((M, R * n), out_dtype)],
        scratch=scratch if scratch is not None else [pltpu.VMEM((tm, tn), F32)],
        operands=(a, b3), sem=("parallel", "parallel", "arbitrary"), comm=comm)


def mm_nt(a, b3, out_dtype, name, comm=None):
    M, _ = a.shape
    R, K, n = b3.shape
    tm, tko, tc = _pick(M, M_TILES), _pick(K, WIDE_TILES), _pick(n, WIDE_TILES)
    ncb = n // tc
    nc = R * ncb
    dot = lambda a_ref, b_ref: lax.dot_general(a_ref[...], b_ref[0], (((1,), (1,)), ((), ())),
                                               preferred_element_type=F32)
    compute, scratch = _mm_body(dot, nc, 2)
    return _call(
        compute, name=name, grid=(M // tm, K // tko, nc),
        in_specs=[pl.BlockSpec((tm, tc), lambda i, j, k: (i, k)),
                  pl.BlockSpec((1, tko, tc), lambda i, j, k: (k // ncb, j, k % ncb))],
        out_specs=[pl.BlockSpec((tm, tko), lambda i, j, k: (i, j))],
        out_shape=[jax.ShapeDtypeStruct((M, K), out_dtype)],
        scratch=scratch if scratch is not None else [pltpu.VMEM((tm, tko), F32)],
        operands=(a, b3), sem=("parallel", "parallel", "arbitrary"), comm=comm)


def mm_tn(a, b, R, out_dtype, name):
    M, K = a.shape
    n = b.shape[1] // R
    tm, tk, tn = _pick(M, (1408,) + M_TILES), _pick(K, MID_TILES), _pick(n, MID_TILES)
    nb, nm = n // tn, M // tm
    dot = lambda a_ref, b_ref: lax.dot_general(a_ref[...], b_ref[...], (((0,), (0,)), ((), ())),
                                               preferred_element_type=F32)
    compute, scratch = _mm_body(dot, nm, 2)
    return _call(
        compute, name=name, grid=(K // tk, R * nb, nm),
        in_specs=[pl.BlockSpec((tm, tk), lambda i, j, m: (m, i)),
                  pl.BlockSpec((tm, tn), lambda i, j, m: (m, j))],
        out_specs=[pl.BlockSpec((1, tk, tn), lambda i, j, m: (j // nb, i, j % nb))],
        out_shape=[jax.ShapeDtypeStruct((R, K, n), out_dtype)],
        scratch=scratch if scratch is not None else [pltpu.VMEM((tk, tn), F32)],
        operands=(a, b), sem=("parallel", "parallel", "arbitrary"))[0]


def _seg_spec(D):
    return pl.BlockSpec((None, 1, D), lambda i: (jnp.minimum(i, 1), 0, 0))


def _seg3(a):
    return a.reshape(2, 1, a.shape[-1])


def _row_spec(w, col=0):
    return pl.BlockSpec((ROW_TILE, w), lambda i: (i, col))


def _acc_spec(r, w):
    return pl.BlockSpec((r, w), lambda i: (0, 0))


def _seg_accumulate(ref, i, val):
    ref[0:1, :] += jnp.where(i == 0, val, 0.0)
    ref[1:2, :] += jnp.where(i == 0, 0.0, val)


def rms_mod_fwd(x, g, shift2, scale2, name):
    T, D = x.shape

    def compute(x_ref, g_ref, sh_ref, sc_ref, h_ref):
        xv = x_ref[...]
        rstd = lax.rsqrt(jnp.mean(xv * xv, axis=-1, keepdims=True) + EPS)
        h = (xv * rstd * g_ref[...]) * (1.0 + sc_ref[...]) + sh_ref[...]
        h_ref[...] = h.astype(h_ref.dtype)

    return _call(compute, name=name, grid=(T // ROW_TILE,),
                 in_specs=[_row_spec(D), _acc_spec(1, D), _seg_spec(D), _seg_spec(D)],
                 out_specs=[_row_spec(D)], out_shape=[jax.ShapeDtypeStruct((T, D), BF16)],
                 operands=(x, g, _seg3(shift2), _seg3(scale2)), sem=("parallel",))[0]


def rms_mod_bwd(x, g, scale2, dh, dres, name):
    T, D = x.shape

    def compute(x_ref, g_ref, sc_ref, dh_ref, dres_ref, dx_ref, dg_ref, dsh_ref, dsc_ref):
        i = pl.program_id(0)

        @pl.when(i == 0)
        def _():
            dg_ref[...] = jnp.zeros_like(dg_ref)
            dsh_ref[...] = jnp.zeros_like(dsh_ref)
            dsc_ref[...] = jnp.zeros_like(dsc_ref)

        xv = x_ref[...]
        dh = dh_ref[...].astype(F32)
        gv = g_ref[...]
        rstd = lax.rsqrt(jnp.mean(xv * xv, axis=-1, keepdims=True) + EPS)
        xh = xv * rstd
        u = dh * (1.0 + sc_ref[...])
        dg_ref[...] += jnp.sum(u * xh, axis=0, keepdims=True)
        _seg_accumulate(dsh_ref, i, jnp.sum(dh, axis=0, keepdims=True))
        _seg_accumulate(dsc_ref, i, jnp.sum(dh * xh * gv, axis=0, keepdims=True))
        dxh = u * gv
        dx = rstd * (dxh - xh * jnp.mean(dxh * xh, axis=-1, keepdims=True))
        dx_ref[...] = dres_ref[...] + dx

    return _call(compute, name=name, grid=(T // ROW_TILE,),
                 in_specs=[_row_spec(D), _acc_spec(1, D), _seg_spec(D), _row_spec(D), _row_spec(D)],
                 out_specs=[_row_spec(D), _acc_spec(1, D), _acc_spec(2, D), _acc_spec(2, D)],
                 out_shape=[jax.ShapeDtypeStruct((T, D), F32), jax.ShapeDtypeStruct((1, D), F32),
                            jax.ShapeDtypeStruct((2, D), F32), jax.ShapeDtypeStruct((2, D), F32)],
                 operands=(x, g, _seg3(scale2), dh, dres), sem=("arbitrary",))


def gate_res_fwd(x, y, gate2, name):
    T, D = x.shape

    def compute(x_ref, y_ref, g_ref, o_ref):
        o_ref[...] = x_ref[...] + g_ref[...] * y_ref[...]

    return _call(compute, name=name, grid=(T // ROW_TILE,),
                 in_specs=[_row_spec(D), _row_spec(D), _seg_spec(D)],
                 out_specs=[_row_spec(D)], out_shape=[jax.ShapeDtypeStruct((T, D), F32)],
                 operands=(x, y, _seg3(gate2)), sem=("parallel",))[0]


def gate_bwd(dxo, y, gate2, name):
    T, D = dxo.shape

    def compute(d_ref, y_ref, g_ref, dy_ref, dg_ref):
        i = pl.program_id(0)

        @pl.when(i == 0)
        def _():
            dg_ref[...] = jnp.zeros_like(dg_ref)

        d = d_ref[...]
        dy_ref[...] = (d * g_ref[...]).astype(dy_ref.dtype)
        _seg_accumulate(dg_ref, i, jnp.sum(d * y_ref[...], axis=0, keepdims=True))

    return _call(compute, name=name, grid=(T // ROW_TILE,),
                 in_specs=[_row_spec(D), _row_spec(D), _seg_spec(D)],
                 out_specs=[_row_spec(D), _acc_spec(2, D)],
                 out_shape=[jax.ShapeDtypeStruct((T, D), BF16), jax.ShapeDtypeStruct((2, D), F32)],
                 operands=(dxo, y, _seg3(gate2)), sem=("arbitrary",))


def final_loss(x, g, target, name):
    T, D = x.shape

    def compute(x_ref, g_ref, t_ref, dx_ref, dg_ref, loss_ref):
        i = pl.program_id(0)

        @pl.when(i == 0)
        def _():
            dg_ref[...] = jnp.zeros_like(dg_ref)
            loss_ref[...] = jnp.zeros_like(loss_ref)
            dx_ref[...] = jnp.zeros_like(dx_ref)

        @pl.when(i > 0)
        def _():
            xv = x_ref[...]
            gv = g_ref[...]
            rstd = lax.rsqrt(jnp.mean(xv * xv, axis=-1, keepdims=True) + EPS)
            xh = xv * rstd
            err = xh * gv - t_ref[...]
            loss_ref[...] += 0.5 * jnp.sum(jnp.mean(err * err, axis=-1, keepdims=True))
            dy = err * (1.0 / D)
            dg_ref[...] += jnp.sum(dy * xh, axis=0, keepdims=True)
            dxh = dy * gv
            dx_ref[...] = rstd * (dxh - xh * jnp.mean(dxh * xh, axis=-1, keepdims=True))

    return _call(compute, name=name, grid=(T // ROW_TILE,),
                 in_specs=[_row_spec(D), _acc_spec(1, D),
                           pl.BlockSpec((ROW_TILE, D), lambda i: (jnp.maximum(i - 1, 0), 0))],
                 out_specs=[_row_spec(D), _acc_spec(1, D), _acc_spec(8, 128)],
                 out_shape=[jax.ShapeDtypeStruct((T, D), F32), jax.ShapeDtypeStruct((1, D), F32),
                            jax.ShapeDtypeStruct((8, 128), F32)],
                 operands=(x, g, target), sem=("arbitrary",))


def _swap32(v):
    lane = lax.broadcasted_iota(jnp.int32, v.shape, 1)
    return jnp.where((lane & 63) < 32, pltpu.roll(v, 96, 1), pltpu.roll(v, 32, 1))


def _rope(v, cos, sin):
    return v * cos + _swap32(v) * sin


def _rope_t(d, cos, sin):
    return d * cos + _swap32(d * sin)


def _chunk_of(step, n_chunks, n_ctx, rev):
    if not rev:
        return step
    return jnp.where(step < n_ctx, n_ctx - 1 - step, n_chunks + n_ctx - 1 - step)


def _dot_t0(a, b):
    return lax.dot_general(a, b, (((0,), (0,)), ((), ())), preferred_element_type=F32)


def _dot_t1(a, b):
    return lax.dot_general(a, b, (((1,), (1,)), ((), ())), preferred_element_type=F32)


def _dot(a, b):
    return jnp.dot(a, b, preferred_element_type=F32)


def ret_fwd(p, cos, sin, tabs, n_heads, n_ctx, rev, name):
    T = p.shape[0]
    C = RET_CHUNK
    H = n_heads
    NC = T // C
    scale = C ** -0.5

    def compute(q_ref, k_ref, v_ref, cos_ref, sin_ref, dm_ref, qd_ref, kd_ref, cd_ref, o_ref, s_ref, S):
        t = pl.program_id(0)

        @pl.when(t == 0)
        def _():
            S[...] = jnp.zeros_like(S)

        cs, sn = cos_ref[...], sin_ref[...]
        for h in range(H):
            sl = slice(h * C, (h + 1) * C)
            q = _rope(q_ref[:, sl], cs, sn)
            k = _rope(k_ref[:, sl], cs, sn) * scale
            qb, kb, vb = q.astype(BF16), k.astype(BF16), v_ref[:, sl].astype(BF16)
            A = _dot_t1(qb, kb) * dm_ref[h]
            s_in = S[h]
            s_ref[h, 0] = s_in
            o_ref[:, sl] = _dot(A.astype(BF16), vb) + _dot(qb, s_in.astype(BF16)) * qd_ref[h]
            S[h] = s_in * cd_ref[h] + _dot_t0((k * kd_ref[h]).astype(BF16), vb)

    cmap = lambda t: _chunk_of(t, NC, n_ctx, rev)
    blk = lambda col: pl.BlockSpec((C, H * C), lambda t: (cmap(t), col))
    tab = pl.BlockSpec((C, C), lambda t: (cmap(t), 0))
    htab = lambda r: pl.BlockSpec((H, r, C), lambda t: (0, 0, 0))
    return _call(
        compute, name=name, grid=(NC,),
        in_specs=[blk(0), blk(1), blk(2), tab, tab, htab(C), htab(C), htab(C), htab(1)],
        out_specs=[blk(0), pl.BlockSpec((H, 1, C, C), lambda t: (0, cmap(t), 0, 0))],
        out_shape=[jax.ShapeDtypeStruct((T, H * C), F32), jax.ShapeDtypeStruct((H, NC, C, C), F32)],
        scratch=[pltpu.VMEM((H, C, C), F32)],
        operands=(p, p, p, cos, sin, tabs["dm"], tabs["qd"], tabs["kd"], tabs["cd"]),
        sem=("arbitrary",))


def ret_bwd(p, cos, sin, tabs, do, s_saved, n_heads, n_ctx, rev, name, comm=None):
    T = p.shape[0]
    C = RET_CHUNK
    H = n_heads
    NC = T // C
    scale = C ** -0.5

    def compute(q_ref, k_ref, v_ref, cos_ref, sin_ref, dm_ref, qd_ref, kd_ref, cd_ref, em_ref, eq_ref,
                ek_ref, do_ref, s_ref, dq_ref, dk_ref, dv_ref, dlg_ref, dS):
        t = pl.program_id(0)

        @pl.when(t == 0)
        def _():
            dS[...] = jnp.zeros_like(dS)
            dlg_ref[...] = jnp.zeros_like(dlg_ref)

        cs, sn = cos_ref[...], sin_ref[...]
        for h in range(H):
            sl = slice(h * C, (h + 1) * C)
            q = _rope(q_ref[:, sl], cs, sn)
            k = _rope(k_ref[:, sl], cs, sn) * scale
            qb, kb, vb = q.astype(BF16), k.astype(BF16), v_ref[:, sl].astype(BF16)
            dmv, qdv, kdv, cdv = dm_ref[h], qd_ref[h], kd_ref[h], cd_ref[h]
            A = _dot_t1(qb, kb) * dmv
            s_in = s_ref[h, 0]
            sb = s_in.astype(BF16)
            ds_out = dS[h]
            dsb = ds_out.astype(BF16)
            dov = do_ref[:, sl]
            dob = dov.astype(BF16)
            dA = _dot_t1(dob, vb)
            dPb = (dA * dmv).astype(BF16)
            doq = dov * qdv
            doqb = doq.astype(BF16)
            kk = k * kdv
            vds = _dot_t1(vb, dsb)
            dq_ref[:, sl] = _dot(dPb, kb) + _dot_t1(doqb, sb)
            dk_ref[:, sl] = _dot_t0(dPb, qb) + vds * kdv
            dv_ref[:, sl] = _dot_t0(A.astype(BF16), dob) + _dot(kk.astype(BF16), dsb)
            dS[h] = ds_out * cdv + _dot_t0(qb, doqb)
            o2 = _dot(qb, sb)
            part = (jnp.sum(dA * A * em_ref[...], axis=0, keepdims=True)
                    + jnp.sum(eq_ref[...] * doq * o2, axis=0, keepdims=True)
                    + jnp.sum(ek_ref[...] * kk * vds, axis=0, keepdims=True)
                    + float(C) * cdv * jnp.sum(s_in * ds_out, axis=0, keepdims=True))
            dlg_ref[h, 0:1, :] += part

    cmap = lambda t: _chunk_of(NC - 1 - t, NC, n_ctx, rev)
    blk = lambda col: pl.BlockSpec((C, H * C), lambda t: (cmap(t), col))
    tab = pl.BlockSpec((C, C), lambda t: (cmap(t), 0))
    const = pl.BlockSpec((C, C), lambda t: (0, 0))
    htab = lambda r: pl.BlockSpec((H, r, C), lambda t: (0, 0, 0))
    return _call(
        compute, name=name, grid=(NC,),
        in_specs=[blk(0), blk(1), blk(2), tab, tab, htab(C), htab(C), htab(C), htab(1), const, const,
                  const, blk(0), pl.BlockSpec((H, 1, C, C), lambda t: (0, cmap(t), 0, 0))],
        out_specs=[blk(0), blk(0), blk(0), pl.BlockSpec((H, 8, C), lambda t: (0, 0, 0))],
        out_shape=[jax.ShapeDtypeStruct((T, H * C), F32)] * 3 + [jax.ShapeDtypeStruct((H, 8, C), F32)],
        scratch=[pltpu.VMEM((H, C, C), F32)],
        operands=(p, p, p, cos, sin, tabs["dm"], tabs["qd"], tabs["kd"], tabs["cd"], tabs["em"], tabs["eq"],
                  tabs["ek"], do, s_saved),
        sem=("arbitrary",), comm=comm)


def ret_out_fwd(o_f, o_b, p, n_heads, name):
    T, RW = o_f.shape
    C = RET_CHUNK

    def compute(of_ref, ob_ref, g_ref, out_ref):
        for h in range(n_heads):
            sl = slice(h * C, (h + 1) * C)
            o = of_ref[:, sl] + ob_ref[:, sl]
            r = o * lax.rsqrt(jnp.mean(o * o, axis=-1, keepdims=True) + EPS)
            g = g_ref[:, sl]
            out_ref[:, sl] = (g * _sigmoid(g) * r).astype(out_ref.dtype)

    return _call(compute, name=name, grid=(T // ROW_TILE,),
                 in_specs=[_row_spec(RW), _row_spec(RW), _row_spec(RW, 3)],
                 out_specs=[_row_spec(RW)], out_shape=[jax.ShapeDtypeStruct((T, RW), BF16)],
                 operands=(o_f, o_b, p), sem=("parallel",))[0]


def ret_out_bwd(o_f, o_b, p, dmix, n_heads, name):
    T, RW = o_f.shape
    C = RET_CHUNK

    def compute(of_ref, ob_ref, g_ref, d_ref, do_ref, dg_ref):
        for h in range(n_heads):
            sl = slice(h * C, (h + 1) * C)
            o = of_ref[:, sl] + ob_ref[:, sl]
            rstd = lax.rsqrt(jnp.mean(o * o, axis=-1, keepdims=True) + EPS)
            r = o * rstd
            g = g_ref[:, sl]
            sg = _sigmoid(g)
            d = d_ref[:, sl].astype(F32)
            dg_ref[:, sl] = (d * r * sg * (1.0 + g * (1.0 - sg))).astype(dg_ref.dtype)
            dr = d * g * sg
            do_ref[:, sl] = rstd * (dr - r * jnp.mean(dr * r, axis=-1, keepdims=True))

    return _call(compute, name=name, grid=(T // ROW_TILE,),
                 in_specs=[_row_spec(RW), _row_spec(RW), _row_spec(RW, 3), _row_spec(RW, 0)],
                 out_specs=[_row_spec(RW), _row_spec(RW)],
                 out_shape=[jax.ShapeDtypeStruct((T, RW), F32), jax.ShapeDtypeStruct((T, RW), BF16)],
                 operands=(o_f, o_b, p, dmix), sem=("parallel",))


def ret_qkv_grad(dqf, dqb, dkf, dkb, dvf, dvb, cos, sin, n_heads, name):
    T, RW = dqf.shape
    C = RET_CHUNK
    scale = C ** -0.5

    def compute(qf, qb, kf, kb, vf, vb, cos_ref, sin_ref, dq_ref, dk_ref, dv_ref):
        cs, sn = cos_ref[...], sin_ref[...]
        for h in range(n_heads):
            sl = slice(h * C, (h + 1) * C)
            dq_ref[:, sl] = _rope_t(qf[:, sl] + qb[:, sl], cs, sn).astype(dq_ref.dtype)
            dk_ref[:, sl] = _rope_t((kf[:, sl] + kb[:, sl]) * scale, cs, sn).astype(dk_ref.dtype)
            dv_ref[:, sl] = (vf[:, sl] + vb[:, sl]).astype(dv_ref.dtype)

    return _call(compute, name=name, grid=(T // ROW_TILE,),
                 in_specs=[_row_spec(RW)] * 6 + [_row_spec(C), _row_spec(C)],
                 out_specs=[_row_spec(RW)] * 3, out_shape=[jax.ShapeDtypeStruct((T, RW), BF16)] * 3,
                 operands=(dqf, dqb, dkf, dkb, dvf, dvb, cos, sin), sem=("parallel",))


def _halo_specs(width, col, halo, n_rows):
    per = ROW_TILE // halo
    last = n_rows // halo - 1
    return [pl.BlockSpec((halo, width), lambda i: (jnp.maximum(i * per - 1, 0), col)),
            pl.BlockSpec((ROW_TILE, width), lambda i: (i, col)),
            pl.BlockSpec((halo, width), lambda i: (jnp.minimum((i + 1) * per, last), col))]


def _halo_valid(i, n_tiles):
    return i >= 2, jnp.logical_and(i >= 1, i <= n_tiles - 2)


def _shifted_copies(S):
    base = S[0]
    rows = base.shape[0]
    for s in range(1, 8):
        S[s] = pltpu.roll(base, rows - s, 0)


def _shifted_rows(S, start, n):
    s = start % 8
    return S[s, start - s:start - s + n, :]


def conv_fwd(p, w, bias, ln_g, ln_b, name):
    T = p.shape[0]
    CW = w.shape[1]
    NT = T // ROW_TILE
    HL = CONV_HALO
    PAD = CONV_K // 2

    def compute(ap, ac, an, bp, bc, bn, w_ref, b_ref, g_ref, be_ref, u2_ref, out_ref, US):
        i = pl.program_id(0)
        vp, vn = _halo_valid(i, NT)
        US[0, 0:HL, :] = jnp.where(vp, ap[...] * _sigmoid(bp[...]), 0.0)
        US[0, HL:HL + ROW_TILE, :] = ac[...] * _sigmoid(bc[...])
        US[0, HL + ROW_TILE:, :] = jnp.where(vn, an[...] * _sigmoid(bn[...]), 0.0)
        _shifted_copies(US)
        for r0 in range(0, ROW_TILE, CONV_ROWS):
            acc = jnp.zeros((CONV_ROWS, CW), F32) + b_ref[...]
            for j in range(CONV_K):
                acc = acc + w_ref[j:j + 1, :] * _shifted_rows(US, HL - PAD + j + r0, CONV_ROWS)
            u2_ref[r0:r0 + CONV_ROWS, :] = acc
            mu = jnp.mean(acc, axis=-1, keepdims=True)
            xc = acc - mu
            rstd = lax.rsqrt(jnp.mean(xc * xc, axis=-1, keepdims=True) + EPS)
            ln = xc * rstd * g_ref[...] + be_ref[...]
            out_ref[r0:r0 + CONV_ROWS, :] = (ln * _sigmoid(ln)).astype(out_ref.dtype)

    vec = _acc_spec(1, CW)
    return _call(compute, name=name, grid=(NT,),
                 in_specs=_halo_specs(CW, 4, HL, T) + _halo_specs(CW, 5, HL, T) + [_acc_spec(CONV_K, CW), vec, vec, vec],
                 out_specs=[_row_spec(CW), _row_spec(CW)],
                 out_shape=[jax.ShapeDtypeStruct((T, CW), F32), jax.ShapeDtypeStruct((T, CW), BF16)],
                 scratch=[pltpu.VMEM((8, ROW_TILE + 2 * HL, CW), F32)],
                 operands=(p, p, p, p, p, p, w, bias, ln_g, ln_b), sem=("parallel",))


def conv_bwd_ln(u2, dmix, ln_g, ln_b, name):
    T, CW = u2.shape

    def compute(u_ref, d_ref, g_ref, be_ref, du_ref, dg_ref, db_ref, dbias_ref):
        i = pl.program_id(0)

        @pl.when(i == 0)
        def _():
            dg_ref[...] = jnp.zeros_like(dg_ref)
            db_ref[...] = jnp.zeros_like(db_ref)
            dbias_ref[...] = jnp.zeros_like(dbias_ref)

        u = u_ref[...]
        gv = g_ref[...]
        mu = jnp.mean(u, axis=-1, keepdims=True)
        xc = u - mu
        rstd = lax.rsqrt(jnp.mean(xc * xc, axis=-1, keepdims=True) + EPS)
        xh = xc * rstd
        ln = xh * gv + be_ref[...]
        sg = _sigmoid(ln)
        dln = d_ref[...].astype(F32) * sg * (1.0 + ln * (1.0 - sg))
        dg_ref[...] += jnp.sum(dln * xh, axis=0, keepdims=True)
        db_ref[...] += jnp.sum(dln, axis=0, keepdims=True)
        dxh = dln * gv
        du = rstd * (dxh - jnp.mean(dxh, axis=-1, keepdims=True)
                     - xh * jnp.mean(dxh * xh, axis=-1, keepdims=True))
        du_ref[...] = du
        dbias_ref[...] += jnp.sum(du, axis=0, keepdims=True)

    vec = _acc_spec(1, CW)
    return _call(compute, name=name, grid=(T // ROW_TILE,),
                 in_specs=[_row_spec(CW), _row_spec(CW, 1), vec, vec],
                 out_specs=[_row_spec(CW), vec, vec, vec],
                 out_shape=[jax.ShapeDtypeStruct((T, CW), F32)] + [jax.ShapeDtypeStruct((1, CW), F32)] * 3,
                 operands=(u2, dmix, ln_g, ln_b), sem=("arbitrary",))


def conv_bwd_taps(p, du2, w, name):
    T = p.shape[0]
    CW = w.shape[1]
    NT = T // ROW_TILE
    HL = CONV_HALO
    PAD = CONV_K // 2

    def compute(ap, ac, an, bp, bc, bn, dp, dc, dn, w_ref, da_ref, db_ref, dw_ref, US, DUS):
        i = pl.program_id(0)

        @pl.when(i == 0)
        def _():
            dw_ref[...] = jnp.zeros_like(dw_ref)

        vp, vn = _halo_valid(i, NT)
        US[0, 0:HL, :] = jnp.where(vp, ap[...] * _sigmoid(bp[...]), 0.0)
        US[0, HL:HL + ROW_TILE, :] = ac[...] * _sigmoid(bc[...])
        US[0, HL + ROW_TILE:, :] = jnp.where(vn, an[...] * _sigmoid(bn[...]), 0.0)
        DUS[0, 0:HL, :] = jnp.where(vp, dp[...], 0.0)
        DUS[0, HL:HL + ROW_TILE, :] = dc[...]
        DUS[0, HL + ROW_TILE:, :] = jnp.where(vn, dn[...], 0.0)
        _shifted_copies(US)
        _shifted_copies(DUS)
        for r0 in range(0, ROW_TILE, CONV_ROWS):
            du = jnp.zeros((CONV_ROWS, CW), F32)
            for j in range(CONV_K):
                du = du + w_ref[j:j + 1, :] * _shifted_rows(DUS, HL + PAD - j + r0, CONV_ROWS)
            a = ac[r0:r0 + CONV_ROWS, :]
            sg = _sigmoid(bc[r0:r0 + CONV_ROWS, :])
            da_ref[r0:r0 + CONV_ROWS, :] = (du * sg).astype(da_ref.dtype)
            db_ref[r0:r0 + CONV_ROWS, :] = (du * a * sg * (1.0 - sg)).astype(db_ref.dtype)
        for j in range(CONV_K):
            acc = jnp.zeros((CONV_ROWS, CW), F32)
            for r0 in range(0, ROW_TILE, CONV_ROWS):
                acc = acc + dc[r0:r0 + CONV_ROWS, :] * _shifted_rows(US, HL - PAD + j + r0, CONV_ROWS)
            dw_ref[j:j + 1, :] += jnp.sum(acc, axis=0, keepdims=True)

    return _call(compute, name=name, grid=(NT,),
                 in_specs=(_halo_specs(CW, 4, HL, T) + _halo_specs(CW, 5, HL, T) + _halo_specs(CW, 0, HL, T)
                           + [_acc_spec(CONV_K, CW)]),
                 out_specs=[_row_spec(CW), _row_spec(CW), _acc_spec(CONV_K, CW)],
                 out_shape=[jax.ShapeDtypeStruct((T, CW), BF16), jax.ShapeDtypeStruct((T, CW), BF16),
                            jax.ShapeDtypeStruct((CONV_K, CW), F32)],
                 scratch=[pltpu.VMEM((8, ROW_TILE + 2 * HL, CW), F32)] * 2,
                 operands=(p, p, p, p, p, p, du2, du2, du2, w), sem=("arbitrary",))


def _ffn_halo_specs(tc, col0, n_rows):
    per = ROW_TILE // FFN_HALO
    last = n_rows // FFN_HALO - 1
    return [pl.BlockSpec((FFN_HALO, tc), lambda cb, i: (jnp.maximum(i * per - 1, 0), col0 + cb)),
            pl.BlockSpec((ROW_TILE, tc), lambda cb, i: (i, col0 + cb)),
            pl.BlockSpec((FFN_HALO, tc), lambda cb, i: (jnp.minimum((i + 1) * per, last), col0 + cb))]


def _ffn_fill(S, prev, cur, nxt, i, n_tiles):
    vp, vn = _halo_valid(i, n_tiles)
    HL = FFN_HALO
    S[1, 0:HL, :] = jnp.where(vp, prev[...], 0.0)
    S[1, HL:HL + ROW_TILE, :] = cur[...]
    S[1, HL + ROW_TILE:, :] = jnp.where(vn, nxt[...], 0.0)
    base = S[1]
    rows = base.shape[0]
    is_lat = i >= 1
    col = lax.broadcasted_iota(jnp.int32, base.shape, 0) & (GRID_W - 1)
    S[0] = jnp.where(jnp.logical_and(is_lat, col == 0), 0.0, pltpu.roll(base, 1, 0))
    S[2] = jnp.where(jnp.logical_and(is_lat, col == GRID_W - 1), 0.0, pltpu.roll(base, rows - 1, 0))


def _ffn_row_factor(i, di):
    return 1.0 if di == 1 else jnp.where(i >= 1, 1.0, 0.0)


def ffn_conv_fwd(up, w9, bias, name):
    T = up.shape[0]
    DFF = w9.shape[1]
    tc = _pick(DFF, (512, 256, 128))
    ncb = DFF // tc
    NT = T // ROW_TILE
    HL = FFN_HALO

    def compute(gp, gc, gn, val_ref, w_ref, b_ref, cg_ref, act_ref, G):
        i = pl.program_id(1)
        _ffn_fill(G, gp, gc, gn, i, NT)
        for r0 in range(0, ROW_TILE, FFN_ROWS):
            acc = jnp.zeros((FFN_ROWS, tc), F32) + b_ref[...]
            for di in range(3):
                for dj in range(3):
                    wt = w_ref[3 * di + dj:3 * di + dj + 1, :] * _ffn_row_factor(i, di)
                    lo = HL + r0 + (di - 1) * GRID_W
                    acc = acc + wt * G[dj, lo:lo + FFN_ROWS, :]
            cg_ref[r0:r0 + FFN_ROWS, :] = acc
            act_ref[r0:r0 + FFN_ROWS, :] = (acc * _sigmoid(acc) * val_ref[r0:r0 + FFN_ROWS, :]).astype(act_ref.dtype)

    tile = pl.BlockSpec((ROW_TILE, tc), lambda cb, i: (i, cb))
    return _call(compute, name=name, grid=(ncb, NT),
                 in_specs=_ffn_halo_specs(tc, 0, T) + [pl.BlockSpec((ROW_TILE, tc), lambda cb, i: (i, ncb + cb)),
                                                       pl.BlockSpec((9, tc), lambda cb, i: (0, cb)),
                                                       pl.BlockSpec((1, tc), lambda cb, i: (0, cb))],
                 out_specs=[tile, tile],
                 out_shape=[jax.ShapeDtypeStruct((T, DFF), F32), jax.ShapeDtypeStruct((T, DFF), BF16)],
                 scratch=[pltpu.VMEM((3, ROW_TILE + 2 * HL, tc), F32)],
                 operands=(up, up, up, up, w9, bias), sem=("parallel", "parallel"))


def ffn_conv_bwd_act(cg, up, dact, name):
    T, DFF = cg.shape
    tc = _pick(DFF, (512, 256, 128))
    ncb = DFF // tc

    def compute(cg_ref, val_ref, d_ref, dcg_ref, dval_ref, db_ref):
        i = pl.program_id(1)

        @pl.when(i == 0)
        def _():
            db_ref[...] = jnp.zeros_like(db_ref)

        c = cg_ref[...]
        sg = _sigmoid(c)
        d = d_ref[...].astype(F32)
        dval_ref[...] = (d * c * sg).astype(dval_ref.dtype)
        dcg = d * val_ref[...] * sg * (1.0 + c * (1.0 - sg))
        dcg_ref[...] = dcg
        db_ref[...] += jnp.sum(dcg, axis=0, keepdims=True)

    tile = pl.BlockSpec((ROW_TILE, tc), lambda cb, i: (i, cb))
    return _call(compute, name=name, grid=(ncb, T // ROW_TILE),
                 in_specs=[tile, pl.BlockSpec((ROW_TILE, tc), lambda cb, i: (i, ncb + cb)), tile],
                 out_specs=[tile, tile, pl.BlockSpec((1, tc), lambda cb, i: (0, cb))],
                 out_shape=[jax.ShapeDtypeStruct((T, DFF), F32), jax.ShapeDtypeStruct((T, DFF), BF16),
                            jax.ShapeDtypeStruct((1, DFF), F32)],
                 operands=(cg, up, dact), sem=("parallel", "arbitrary"))


def ffn_conv_bwd_taps(up, dcg, w9, name, comm=None):
    T, DFF = dcg.shape
    tc = _pick(DFF, (512, 256, 128))
    ncb = DFF // tc
    NT = T // ROW_TILE
    HL = FFN_HALO

    def compute(gp, gc, gn, dp, dc, dn, w_ref, dgate_ref, dw_ref, G, DC):
        i = pl.program_id(1)

        @pl.when(i == 0)
        def _():
            dw_ref[...] = jnp.zeros_like(dw_ref)

        _ffn_fill(G, gp, gc, gn, i, NT)
        _ffn_fill(DC, dp, dc, dn, i, NT)
        for r0 in range(0, ROW_TILE, FFN_ROWS):
            dg = jnp.zeros((FFN_ROWS, tc), F32)
            for di in range(3):
                for dj in range(3):
                    wt = w_ref[3 * di + dj:3 * di + dj + 1, :] * _ffn_row_factor(i, di)
                    lo = HL + r0 - (di - 1) * GRID_W
                    dg = dg + wt * DC[2 - dj, lo:lo + FFN_ROWS, :]
            dgate_ref[r0:r0 + FFN_ROWS, :] = dg.astype(dgate_ref.dtype)
        for di in range(3):
            for dj in range(3):
                acc = jnp.zeros((FFN_ROWS, tc), F32)
                for r0 in range(0, ROW_TILE, FFN_ROWS):
                    lo = HL + r0 + (di - 1) * GRID_W
                    acc = acc + dc[r0:r0 + FFN_ROWS, :] * G[dj, lo:lo + FFN_ROWS, :]
                t = 3 * di + dj
                dw_ref[t:t + 1, :] += _ffn_row_factor(i, di) * jnp.sum(acc, axis=0, keepdims=True)

    tile = pl.BlockSpec((ROW_TILE, tc), lambda cb, i: (i, cb))
    return _call(compute, name=name, grid=(ncb, NT),
                 in_specs=_ffn_halo_specs(tc, 0, T) + _ffn_halo_specs(tc, 0, T) + [pl.BlockSpec((9, tc), lambda cb, i: (0, cb))],
                 out_specs=[tile, pl.BlockSpec((9, tc), lambda cb, i: (0, cb))],
                 out_shape=[jax.ShapeDtypeStruct((T, DFF), BF16), jax.ShapeDtypeStruct((9, DFF), F32)],
                 scratch=[pltpu.VMEM((3, ROW_TILE + 2 * HL, tc), F32)] * 2,
                 operands=(up, up, up, dcg, dcg, dcg, w9), sem=("parallel", "arbitrary"), comm=comm)


def _adamw_update(g, w_ref, m_ref, v_ref, g_ref, d_ref, nm_ref, nv_ref):
    c1 = 1.0 - ADAM_B1 ** ADAM_STEP
    c2 = 1.0 - ADAM_B2 ** ADAM_STEP
    nm = ADAM_B1 * m_ref[...] + (1.0 - ADAM_B1) * g
    nv = ADAM_B2 * v_ref[...] + (1.0 - ADAM_B2) * (g * g)
    g_ref[...] = g
    nm_ref[...] = nm
    nv_ref[...] = nv
    d_ref[...] = -ADAM_LR * ((nm / c1) / (jnp.sqrt(nv / c2) + ADAM_EPS) + ADAM_WD * w_ref[...])


def _sum_parts(p_ref, P):
    g = p_ref[0].astype(F32)
    for k in range(1, P):
        g = g + p_ref[k].astype(F32)
    return g


def adamw(parts, w, m, v, name):
    P, R, C = parts.shape
    fits = lambda t: 2 * (P + 7) * t * C * 4 <= ADAM_VMEM_BYTES
    tr = R if fits(R) else _pick(R, [t for t in (1024, 512, 256, 128, 64, 32, 16, 8) if fits(t)])

    def compute(p_ref, w_ref, m_ref, v_ref, g_ref, d_ref, nm_ref, nv_ref):
        _adamw_update(_sum_parts(p_ref, P), w_ref, m_ref, v_ref, g_ref, d_ref, nm_ref, nv_ref)

    tile = pl.BlockSpec((tr, C), lambda i: (i, 0))
    return _call(compute, name=name, grid=(R // tr,),
                 in_specs=[pl.BlockSpec((P, tr, C), lambda i: (0, i, 0)), tile, tile, tile],
                 out_specs=[tile] * 4, out_shape=[jax.ShapeDtypeStruct((R, C), F32)] * 4,
                 operands=(parts, w, m, v), sem=("parallel",))


def adamw_layers(parts_l, w, m, v, name):
    L = len(parts_l)
    P, R, C = parts_l[0].shape
    psize = parts_l[0].dtype.itemsize
    fits = lambda t: 2 * t * C * (L * P * psize + 7 * 4) <= ADAM_VMEM_BYTES
    tr = _pick(R, [t for t in (1024, 512, 256, 128, 64, 32, 16, 8) if fits(t)])

    def compute(*refs):
        p_refs, (w_ref, m_ref, v_ref), outs = refs[:L], refs[L:L + 3], refs[L + 3:]
        for l in range(L):
            @pl.when(pl.program_id(0) == l)
            def _(l=l):
                _adamw_update(_sum_parts(p_refs[l], P), w_ref, m_ref, v_ref, *outs)

    tile = pl.BlockSpec((None, tr, C), lambda l, i: (l, i, 0))
    part = lambda k: pl.BlockSpec((P, tr, C), lambda l, i: (0, jnp.where(l == k, i, 0), 0))
    return _call(compute, name=name, grid=(L, R // tr),
                 in_specs=[part(k) for k in range(L)] + [tile] * 3,
                 out_specs=[tile] * 4, out_shape=[jax.ShapeDtypeStruct((L, R, C), F32)] * 4,
                 operands=(*parts_l, w, m, v), sem=("arbitrary", "arbitrary"))


def _rope_tables(seq, ctx):
    t = jnp.arange(seq)
    quarter = RET_CHUNK // 4
    inv_freq = 1.0 / (ROPE_THETA ** (jnp.arange(0, quarter, dtype=F32) / quarter))
    ang_r = (t // GRID_W).astype(F32)[:, None] * inv_freq[None, :]
    ang_c = (t % GRID_W).astype(F32)[:, None] * inv_freq[None, :]
    cr, sr, cc, sc = jnp.cos(ang_r), jnp.sin(ang_r), jnp.cos(ang_c), jnp.sin(ang_c)
    cos = jnp.concatenate([cr, cr, cc, cc], axis=-1)
    sin = jnp.concatenate([-sr, sr, -sc, sc], axis=-1)
    cos = jnp.concatenate([jnp.ones((ctx, RET_CHUNK), F32), cos], axis=0)
    sin = jnp.concatenate([jnp.zeros((ctx, RET_CHUNK), F32), sin], axis=0)
    return cos, sin


def _decay_tables(decay_logit, rev):
    C = RET_CHUNK
    lg = jax.nn.log_sigmoid(decay_logit.astype(F32))
    idx = jnp.arange(C, dtype=F32)
    diff = idx[:, None] - idx[None, :]
    if rev:
        diff = -diff
        eq, ek = C - idx, idx
    else:
        eq, ek = idx + 1.0, C - 1.0 - idx
    keep = diff >= 0
    em = jnp.where(keep, diff, 0.0)
    bc = lambda e: jnp.broadcast_to(e[:, None], (C, C))
    return {
        "dm": jnp.where(keep[None], jnp.exp(lg[:, None, None] * em[None]), 0.0),
        "qd": jnp.broadcast_to(jnp.exp(lg[:, None] * eq[None, :])[:, :, None], (lg.shape[0], C, C)),
        "kd": jnp.broadcast_to(jnp.exp(lg[:, None] * ek[None, :])[:, :, None], (lg.shape[0], C, C)),
        "cd": jnp.broadcast_to(jnp.exp(lg * C)[:, None, None], (lg.shape[0], 1, C)),
        "em": em, "eq": bc(eq), "ek": bc(ek),
    }


def _silu(z):
    return z * jax.nn.sigmoid(z)


def _dsilu(z):
    s = jax.nn.sigmoid(z)
    return s * (1.0 + z * (1.0 - s))


def kernel(x, c, ctx, c_ctx, w_mod, b_mod, norm1_g, norm2_g, w_in, ret_decay_f, ret_decay_b, conv_dw_w, conv_dw_b, conv_ln_g, conv_ln_b, w_out, ffn_w_up, ffn_dw_w, ffn_dw_b, ffn_w_down, final_norm_g, loss_target, m_c_ctx, m_w_mod, m_b_mod, m_norm1_g, m_norm2_g, m_w_in, m_ret_decay_f, m_ret_decay_b, m_conv_dw_w, m_conv_dw_b, m_conv_ln_g, m_conv_ln_b, m_w_out, m_ffn_w_up, m_ffn_dw_w, m_ffn_dw_b, m_ffn_w_down, m_final_norm_g, v_c_ctx, v_w_mod, v_b_mod, v_norm1_g, v_norm2_g, v_w_in, v_ret_decay_f, v_ret_decay_b, v_conv_dw_w, v_conv_dw_b, v_conv_ln_g, v_conv_ln_b, v_w_out, v_ffn_w_up, v_ffn_dw_w, v_ffn_dw_b, v_ffn_w_down, v_final_norm_g):
    L, D, _ = w_mod.shape
    SEQ, CTX = x.shape[1], ctx.shape[1]
    T = SEQ + CTX
    RW = D // 2
    CW = D - RW
    H = RW // RET_CHUNK
    DFF = ffn_dw_b.shape[1]
    NMOD = b_mod.shape[1] // D
    n_ctx = CTX // RET_CHUNK
    assert CTX == ROW_TILE and RW == CW and SEQ % ROW_TILE == 0 and NMOD == 6
    me = _my_rank()
    wm_n = w_mod.shape[2]
    wo_k, wd_k = w_out.shape[1], ffn_w_down.shape[1]
    cw_n, fw_n = conv_dw_w.shape[2], ffn_dw_w.shape[3]

    w_mod_b = w_mod.astype(BF16)
    w_in_b, w_out_b, w_up_b, w_down_b = (a.astype(BF16) for a in (w_in, w_out, ffn_w_up, ffn_w_down))
    as_rows = lambda g: g.reshape(1, -1, D)
    g_in, g_out, g_up, g_down, g_cw, g_fw, g_c = run_comm(
        "gather", [w_in_b[0], w_out_b[0], w_up_b[0], w_down_b[0], conv_dw_w, ffn_dw_w, _silu(c)], "gather_first")
    w_in_l, w_up_l, w_out_l, w_down_l = [g_in], [g_up], [as_rows(g_out)], [as_rows(g_down)]
    conv_w_l = [jnp.moveaxis(g_cw[:, l], 0, 1).reshape(CONV_K, CW) for l in range(L)]
    ffn_w9_l = [jnp.moveaxis(g_fw[:, l], 0, 2).reshape(9, DFF) for l in range(L)]

    s_cond = jnp.concatenate([g_c.reshape(N_DEV, D), jnp.broadcast_to(_silu(c_ctx)[None], (N_DEV, D))], axis=0)
    s_cond_b = s_cond.astype(BF16)
    mod_shard = mm_nn(s_cond_b, w_mod_b, F32, "mod_fwd")[0]
    (g_mod,) = run_comm("gather", [mod_shard], "gather_mod")
    mod_all = jnp.transpose(g_mod.reshape(N_DEV, 2 * N_DEV, L, wm_n), (2, 1, 0, 3)).reshape(L, 2 * N_DEV, NMOD * D)
    mod_all = mod_all + b_mod[:, None, :]
    mod_lat = lax.dynamic_index_in_dim(mod_all, me, axis=1, keepdims=False)
    mod_ctx = mod_all[:, N_DEV]
    mod2 = jnp.stack([mod_ctx, mod_lat], axis=1).reshape(L, 2, NMOD, D)

    cos, sin = _rope_tables(SEQ, CTX)
    xs = jnp.concatenate([ctx[0], x[0]], axis=0)

    saved = []
    for l in range(L):
        nxt = l + 1 < L
        sh1, sc1, g1, sh2, sc2, g2 = (mod2[l, :, k] for k in range(NMOD))
        tf = _decay_tables(ret_decay_f[l], False)
        tb = _decay_tables(ret_decay_b[l], True)
        h = rms_mod_fwd(xs, norm1_g[l][None], sh1, sc1, "norm1_fwd")
        if nxt:
            (p,), (gi, go) = mm_nn(h, w_in_l[l], F32, "in_proj_g", Comm("gather", [w_in_b[l + 1], w_out_b[l + 1]]))
            w_in_l.append(gi)
            w_out_l.append(as_rows(go))
        else:
            (p,) = mm_nn(h, w_in_l[l], F32, "in_proj")
        o_f, s_f = ret_fwd(p, cos, sin, tf, H, n_ctx, False, "ret_fwd_f")
        o_b, s_b = ret_fwd(p, cos, sin, tb, H, n_ctx, True, "ret_fwd_b")
        mix_r = ret_out_fwd(o_f, o_b, p, H, "ret_out_fwd")
        u2, mix_c = conv_fwd(p, conv_w_l[l], conv_dw_b[l][None], conv_ln_g[l][None], conv_ln_b[l][None], "conv_fwd")
        mix = jnp.concatenate([mix_r, mix_c], axis=1)
        (y1,) = mm_nn(mix, w_out_l[l], F32, "out_proj")
        x2 = gate_res_fwd(xs, y1, g1, "res1_fwd")
        h2 = rms_mod_fwd(x2, norm2_g[l][None], sh2, sc2, "norm2_fwd")
        if nxt:
            (up,), (gu,) = mm_nn(h2, w_up_l[l], F32, "ffn_up_g", Comm("gather", [w_up_b[l + 1]]))
            w_up_l.append(gu)
        else:
            (up,) = mm_nn(h2, w_up_l[l], F32, "ffn_up")
        cg, act = ffn_conv_fwd(up, ffn_w9_l[l], ffn_dw_b[l][None], "ffn_conv_fwd")
        if nxt:
            (y2,), (gd,) = mm_nn(act, w_down_l[l], F32, "ffn_down_g", Comm("gather", [w_down_b[l + 1]]))
            w_down_l.append(as_rows(gd))
        else:
            (y2,) = mm_nn(act, w_down_l[l], F32, "ffn_down")
        x3 = gate_res_fwd(x2, y2, g2, "res2_fwd")
        saved.append(dict(x1=xs, h=h, p=p, o_f=o_f, o_b=o_b, s_f=s_f, s_b=s_b, u2=u2, mix=mix, y1=y1, x2=x2,
                          h2=h2, up=up, cg=cg, act=act, y2=y2, tf=tf, tb=tb))
        xs = x3

    dxs, d_final_g, loss_part = final_loss(xs, final_norm_g[None], loss_target[0], "final_loss")
    loss = lax.psum(loss_part[0, 0], ("x", "y", "c"))

    landed = {n: [None] * L for n in ("w_in", "w_out", "ffn_w_up", "ffn_w_down")}
    small = {n: [None] * L for n in ("norm1_g", "norm2_g", "ret_decay_f", "ret_decay_b", "conv_dw_w", "conv_dw_b",
                                     "conv_ln_g", "conv_ln_b", "ffn_dw_w", "ffn_dw_b")}
    dmod2 = [None] * L
    g_in_prev = None
    for l in reversed(range(L)):
        sv = saved[l]
        sh1, sc1, g1, sh2, sc2, g2 = (mod2[l, :, k] for k in range(NMOD))
        dy2, dg2 = gate_bwd(dxs, sv["y2"], g2, "res2_bwd")
        if g_in_prev is not None:
            (dact,), (landed["w_in"][l + 1],) = mm_nt(dy2, w_down_l[l], BF16, "ffn_down_dx_x", Comm("exchange", [g_in_prev]))
        else:
            (dact,) = mm_nt(dy2, w_down_l[l], BF16, "ffn_down_dx")
        g_down = mm_tn(sv["act"], dy2, 1, BF16, "ffn_down_dw").reshape(N_DEV, wd_k, D)
        dcg, dval, small["ffn_dw_b"][l] = ffn_conv_bwd_act(sv["cg"], sv["up"], dact, "ffn_conv_bwd_act")
        (dgate, small["ffn_dw_w"][l]), (landed["ffn_w_down"][l],) = ffn_conv_bwd_taps(
            sv["up"], dcg, ffn_w9_l[l], "ffn_conv_bwd_taps", Comm("exchange", [g_down]))
        dup = jnp.concatenate([dgate, dval], axis=1)
        (dh2,) = mm_nt(dup, w_up_l[l], F32, "ffn_up_dx")
        g_up = mm_tn(sv["h2"], dup, N_DEV, BF16, "ffn_up_dw")
        dx2, small["norm2_g"][l], dsh2, dsc2 = rms_mod_bwd(sv["x2"], norm2_g[l][None], sc2, dh2, dxs, "norm2_bwd")
        dy1, dg1 = gate_bwd(dx2, sv["y1"], g1, "res1_bwd")
        (dmix,) = mm_nt(dy1, w_out_l[l], BF16, "out_proj_dx")
        g_out = mm_tn(sv["mix"], dy1, 1, BF16, "out_proj_dw").reshape(N_DEV, wo_k, D)
        do, dgt = ret_out_bwd(sv["o_f"], sv["o_b"], sv["p"], dmix, H, "ret_out_bwd")
        (dqf, dkf, dvf, dlg_f), (landed["ffn_w_up"][l],) = ret_bwd(
            sv["p"], cos, sin, sv["tf"], do, sv["s_f"], H, n_ctx, False, "ret_bwd_f", Comm("exchange", [g_up]))
        (dqb, dkb, dvb, dlg_b), (landed["w_out"][l],) = ret_bwd(
            sv["p"], cos, sin, sv["tb"], do, sv["s_b"], H, n_ctx, True, "ret_bwd_b", Comm("exchange", [g_out]))
        dq, dk, dv = ret_qkv_grad(dqf, dqb, dkf, dkb, dvf, dvb, cos, sin, H, "ret_qkv_grad")
        small["ret_decay_f"][l] = jnp.sum(dlg_f[:, 0, :], axis=-1) * jax.nn.sigmoid(-ret_decay_f[l])
        small["ret_decay_b"][l] = jnp.sum(dlg_b[:, 0, :], axis=-1) * jax.nn.sigmoid(-ret_decay_b[l])
        du2, small["conv_ln_g"][l], small["conv_ln_b"][l], small["conv_dw_b"][l] = conv_bwd_ln(
            sv["u2"], dmix, conv_ln_g[l][None], conv_ln_b[l][None], "conv_bwd_ln")
        da, dbg, small["conv_dw_w"][l] = conv_bwd_taps(sv["p"], du2, conv_w_l[l], "conv_bwd_taps")
        dp = jnp.concatenate([dq, dk, dv, dgt, da, dbg], axis=1)
        (dh,) = mm_nt(dp, w_in_l[l], F32, "in_proj_dx")
        g_in_prev = mm_tn(sv["h"], dp, N_DEV, BF16, "in_proj_dw")
        dxs, small["norm1_g"][l], dsh1, dsc1 = rms_mod_bwd(sv["x1"], norm1_g[l][None], sc1, dh, dx2, "norm1_bwd")
        dmod2[l] = jnp.concatenate([dsh1, dsc1, dg1, dsh2, dsc2, dg2], axis=1)
    (landed["w_in"][0],) = run_comm("exchange", [g_in_prev], "exchange_last")

    grad_x = dxs[CTX:][None]

    dmod2 = jnp.stack(dmod2)
    (g_dmod,) = run_comm("gather", [dmod2], "gather_dmod")
    dmod_all = jnp.concatenate([jnp.moveaxis(g_dmod[:, :, 1], 0, 1), jnp.moveaxis(g_dmod[:, :, 0], 0, 1)], axis=1)
    dmod_sh = lax.dynamic_slice_in_dim(dmod_all, me * wm_n, wm_n, axis=2)
    dmod_sh = jnp.moveaxis(dmod_sh, 0, 1).reshape(2 * N_DEV, L * wm_n).astype(BF16)
    g_w_mod = mm_tn(s_cond_b, dmod_sh, L, F32, "mod_dw")
    (d_cond,) = mm_nt(dmod_sh, w_mod_b, F32, "mod_dx")
    g_c_ctx_part = jnp.sum(d_cond[N_DEV:], axis=0) * _dsilu(c_ctx)
    g_b_mod_part = dmod2[:, 0] + dmod2[:, 1]

    pad128 = lambda a: jnp.pad(a.reshape(-1), (0, (-a.size) % 128))
    rep_names = ["c_ctx", "b_mod", "norm1_g", "norm2_g", "ret_decay_f", "ret_decay_b", "conv_dw_b", "conv_ln_g",
                 "conv_ln_b", "ffn_dw_b", "final_norm_g"]
    given = dict(c_ctx=(c_ctx, m_c_ctx, v_c_ctx), b_mod=(b_mod, m_b_mod, v_b_mod),
                 norm1_g=(norm1_g, m_norm1_g, v_norm1_g), norm2_g=(norm2_g, m_norm2_g, v_norm2_g),
                 ret_decay_f=(ret_decay_f, m_ret_decay_f, v_ret_decay_f),
                 ret_decay_b=(ret_decay_b, m_ret_decay_b, v_ret_decay_b),
                 conv_dw_b=(conv_dw_b, m_conv_dw_b, v_conv_dw_b), conv_ln_g=(conv_ln_g, m_conv_ln_g, v_conv_ln_g),
                 conv_ln_b=(conv_ln_b, m_conv_ln_b, v_conv_ln_b), ffn_dw_b=(ffn_dw_b, m_ffn_dw_b, v_ffn_dw_b),
                 final_norm_g=(final_norm_g, m_final_norm_g, v_final_norm_g))
    rep_part = dict(c_ctx=g_c_ctx_part, b_mod=g_b_mod_part, final_norm_g=d_final_g)
    for nme in rep_names:
        if nme not in rep_part:
            rep_part[nme] = jnp.stack([a.reshape(-1) for a in small[nme]])
    rep_sizes = [((-given[nme][0].size) % 128) + given[nme][0].size for nme in rep_names]
    n_rep = sum(rep_sizes)
    cw_part = jnp.stack(small["conv_dw_w"])
    fw_part = jnp.stack(small["ffn_dw_w"])
    packed = jnp.concatenate([pad128(rep_part[nme]) for nme in rep_names] + [cw_part.reshape(-1), fw_part.reshape(-1)])
    (g_small,) = run_comm("gather", [packed.reshape(-1, 128)], "gather_small")
    g_small = g_small.reshape(N_DEV, -1)
    rep_w, rep_m, rep_v = (jnp.concatenate([pad128(given[nme][k]) for nme in rep_names]).reshape(-1, 128) for k in range(3))
    rep_out = adamw(g_small[:, :n_rep].reshape(N_DEV, -1, 128), rep_w, rep_m, rep_v, "adamw_small")
    res = {}
    off = 0
    for nme, sz in zip(rep_names, rep_sizes):
        shape = given[nme][0].shape
        res[nme] = [o.reshape(-1)[off:off + given[nme][0].size].reshape(shape) for o in rep_out]
        off += sz

    cw_all = g_small[:, n_rep:n_rep + cw_part.size].reshape(N_DEV, L * CONV_K, CW)
    cw_mine = lax.dynamic_slice_in_dim(cw_all, me * cw_n, cw_n, axis=2)
    res["conv_dw_w"] = [o.reshape(conv_dw_w.shape) for o in adamw(
        cw_mine, conv_dw_w.reshape(L * CONV_K, cw_n), m_conv_dw_w.reshape(L * CONV_K, cw_n),
        v_conv_dw_w.reshape(L * CONV_K, cw_n), "adamw_conv_w")]
    fw_all = g_small[:, n_rep + cw_part.size:].reshape(N_DEV, L * 9, DFF)
    fw_mine = lax.dynamic_slice_in_dim(fw_all, me * fw_n, fw_n, axis=2)
    res["ffn_dw_w"] = [o.reshape(ffn_dw_w.shape) for o in adamw(
        fw_mine, ffn_dw_w.reshape(L * 9, fw_n), m_ffn_dw_w.reshape(L * 9, fw_n),
        v_ffn_dw_w.reshape(L * 9, fw_n), "adamw_ffn_w")]

    res["w_mod"] = [o.reshape(w_mod.shape) for o in adamw(
        g_w_mod.reshape(1, L * D, wm_n), w_mod.reshape(L * D, wm_n), m_w_mod.reshape(L * D, wm_n),
        v_w_mod.reshape(L * D, wm_n), "adamw_w_mod")]

    res["w_in"] = adamw_layers(landed["w_in"], w_in, m_w_in, v_w_in, "adamw_w_in")
    res["w_out"] = adamw_layers(landed["w_out"], w_out, m_w_out, v_w_out, "adamw_w_out")
    res["ffn_w_up"] = adamw_layers(landed["ffn_w_up"], ffn_w_up, m_ffn_w_up, v_ffn_w_up, "adamw_ffn_w_up")
    res["ffn_w_down"] = adamw_layers(landed["ffn_w_down"], ffn_w_down, m_ffn_w_down, v_ffn_w_down, "adamw_ffn_w_down")

    order = ["c_ctx", "w_mod", "b_mod", "norm1_g", "norm2_g", "w_in", "ret_decay_f", "ret_decay_b", "conv_dw_w",
             "conv_dw_b", "conv_ln_g", "conv_ln_b", "w_out", "ffn_w_up", "ffn_dw_w", "ffn_dw_b", "ffn_w_down",
             "final_norm_g"]
    return (loss, grad_x, *[res[nme][0] for nme in order], *[res[nme][1] for nme in order],
            *[res[nme][2] for nme in order], *[res[nme][3] for nme in order])
```

```python
import functools

import jax
import jax.numpy as jnp
from jax import lax
from jax.experimental import pallas as pl
from jax.experimental.pallas import tpu as pltpu

F32 = jnp.float32
BF16 = jnp.bfloat16
EPS = 1e-6
N_DEV = 8
ROW_TILE = 256
RET_CHUNK = 128
GRID_W = 64
CONV_K = 31
CONV_HALO = 16
CONV_ROWS = 32
FFN_ROWS = 64
FFN_HALO = 128
ROPE_THETA = 10000.0
ADAM_LR = 0.001
ADAM_B1 = 0.9
ADAM_B2 = 0.999
ADAM_EPS = 1e-08
ADAM_WD = 0.01
ADAM_STEP = 10
VMEM_LIMIT = 56 * 1024 * 1024
ADAM_VMEM_BYTES = 24 * 1024 * 1024
MESH = pl.DeviceIdType.MESH
ANY = pl.BlockSpec(memory_space=pl.ANY)


def _pick(n, cands):
    for t in cands:
        if n % t == 0:
            return t
    return n


def _sigmoid(z):
    return 1.0 / (1.0 + jnp.exp(-z))


def _my_rank():
    return 4 * lax.axis_index("x") + 2 * lax.axis_index("y") + lax.axis_index("c")


def _peer(j):
    x, y, c = lax.axis_index("x"), lax.axis_index("y"), lax.axis_index("c")
    px = 1 - x if j & 4 else x
    py = 1 - y if j & 2 else y
    pc = 1 - c if j & 1 else c
    return (px, py, pc), 4 * px + 2 * py + pc


class Comm:
    def __init__(self, kind, arrs):
        assert kind in ("gather", "exchange")
        self.kind, self.arrs, self.n = kind, list(arrs), len(arrs)
        self.in_specs = [ANY] * self.n
        self.out_specs = [ANY] * self.n
        lead = (N_DEV,) if kind == "gather" else ()
        self.out_shape = [jax.ShapeDtypeStruct(lead + a.shape, a.dtype) for a in self.arrs]
        per = self.n * (N_DEV - 1)
        self.scratch = [pltpu.SemaphoreType.DMA((per,)), pltpu.SemaphoreType.DMA((per,)),
                        pltpu.SemaphoreType.DMA((self.n,))]

    def _src(self, ref, rank):
        return ref if self.kind == "gather" else ref.at[rank]

    def _local(self, ins, outs, sems, a):
        me = _my_rank()
        return pltpu.make_async_copy(self._src(ins[a], me), outs[a].at[me], sems[2].at[a])

    def _remote(self, ins, outs, sems, a, j, receive):
        dev, rank = _peer(j)
        s = a * (N_DEV - 1) + j - 1
        slot = rank if receive else _my_rank()
        return pltpu.make_async_remote_copy(src_ref=self._src(ins[a], rank), dst_ref=outs[a].at[slot],
                                            send_sem=sems[0].at[s], recv_sem=sems[1].at[s],
                                            device_id=dev, device_id_type=MESH)

    def start(self, ins, outs, sems):
        for a in range(self.n):
            self._local(ins, outs, sems, a).start()
        for a in range(self.n):
            for j in range(1, N_DEV):
                self._remote(ins, outs, sems, a, j, False).start()

    def wait(self, ins, outs, sems):
        for a in range(self.n):
            for j in range(1, N_DEV):
                cp = self._remote(ins, outs, sems, a, j, True)
                cp.wait_recv()
                cp.wait_send()
        for a in range(self.n):
            self._local(ins, outs, sems, a).wait()


def _call(compute, *, name, grid, in_specs, out_specs, out_shape, operands, sem, scratch=(), comm=None):
    n_in, n_out, n_sc = len(in_specs), len(out_specs), len(scratch)
    k = comm.n if comm else 0

    def body(*refs):
        ins, cin = refs[:n_in], refs[n_in:n_in + k]
        o0 = n_in + k
        outs, cout = refs[o0:o0 + n_out], refs[o0 + n_out:o0 + n_out + k]
        s0 = o0 + n_out + k
        sc, sems = refs[s0:s0 + n_sc], refs[s0 + n_sc:]
        if comm:
            ids = [pl.program_id(d) for d in range(len(grid))]
            first = functools.reduce(jnp.logical_and, [i == 0 for i in ids])
            last = functools.reduce(jnp.logical_and, [i == g - 1 for i, g in zip(ids, grid)])

            @pl.when(first)
            def _():
                comm.start(cin, cout, sems)

        compute(*ins, *outs, *sc)

        if comm:
            @pl.when(last)
            def _():
                comm.wait(cin, cout, sems)

    semantics = ("arbitrary",) * len(grid) if comm else sem
    res = pl.pallas_call(
        body, name=name, grid=grid,
        in_specs=list(in_specs) + (comm.in_specs if comm else []),
        out_specs=list(out_specs) + (comm.out_specs if comm else []),
        out_shape=list(out_shape) + (comm.out_shape if comm else []),
        scratch_shapes=list(scratch) + (comm.scratch if comm else []),
        compiler_params=pltpu.CompilerParams(dimension_semantics=semantics, vmem_limit_bytes=VMEM_LIMIT),
    )(*operands, *(comm.arrs if comm else []))
    if comm:
        return list(res[:n_out]), list(res[n_out:])
    return list(res)


def run_comm(kind, arrs, name):
    comm = Comm(kind, arrs)

    def body(*refs):
        ins, outs, sems = refs[:comm.n], refs[comm.n:2 * comm.n], refs[2 * comm.n:]
        comm.start(ins, outs, sems)
        comm.wait(ins, outs, sems)

    return list(pl.pallas_call(body, name=name, in_specs=comm.in_specs, out_specs=comm.out_specs,
                               out_shape=comm.out_shape, scratch_shapes=comm.scratch)(*comm.arrs))


M_TILES = (768, 512, 256, 128)
WIDE_TILES = (2048, 1408, 1024, 768, 512, 256, 128)
MID_TILES = (1408, 1024, 768, 512, 256, 128)


def _mm_body(dot, n_steps, axis):
    if n_steps == 1:
        def compute(a_ref, b_ref, o_ref):
            o_ref[...] = dot(a_ref, b_ref).astype(o_ref.dtype).reshape(o_ref.shape)
        return compute, []

    def compute(a_ref, b_ref, o_ref, acc):
        k = pl.program_id(axis)

        @pl.when(k == 0)
        def _():
            acc[...] = jnp.zeros_like(acc)

        acc[...] += dot(a_ref, b_ref)

        @pl.when(k == n_steps - 1)
        def _():
            o_ref[...] = acc[...].astype(o_ref.dtype).reshape(o_ref.shape)

    return compute, None


def mm_nn(a, b3, out_dtype, name, comm=None):
    M, K = a.shape
    R, _, n = b3.shape
    tm, tk, tn = _pick(M, M_TILES), _pick(K, WIDE_TILES), _pick(n, MID_TILES)
    nb, nk = n // tn, K // tk
    compute, scratch = _mm_body(lambda a_ref, b_ref: jnp.dot(a_ref[...], b_ref[0], preferred_element_type=F32), nk, 2)
    return _call(
        compute, name=name, grid=(M // tm, R * nb, nk),
        in_specs=[pl.BlockSpec((tm, tk), lambda i, j, k: (i, k)),
                  pl.BlockSpec((1, tk, tn), lambda i, j, k: (j // nb, k, j % nb))],
        out_specs=[pl.BlockSpec((tm, tn), lambda i, j, k: (i, j))],
        out_shape=[jax.ShapeDtypeStruct((M, R * n), out_dtype)],
        scratch=scratch if scratch is not None else [pltpu.VMEM((tm, tn), F32)],
        operands=(a, b3), sem=("parallel", "parallel", "arbitrary"), comm=comm)


def _piece_specs(n_pieces, per, rows, t, row_of, col_of):
    if n_pieces == 1:
        return [pl.BlockSpec((rows, t), lambda *ids: (row_of(*ids), col_of(*ids)))]

    def index(q, *ids):
        local = col_of(*ids) - q * per
        inside = jnp.logical_and(local >= 0, local < per)
        return jnp.where(inside, row_of(*ids), 0), jnp.where(inside, local, 0)

    return [pl.BlockSpec((rows, t), functools.partial(index, q)) for q in range(n_pieces)]


def _for_piece(n_pieces, per, block, fn):
    if n_pieces == 1:
        fn(0)
        return
    for q in range(n_pieces):
        pl.when(block // per == q)(functools.partial(fn, q))


def mm_nt(a, b3, out_dtype, name, comm=None):
    pieces = list(a) if isinstance(a, (list, tuple)) else [a]
    P = len(pieces)
    M, W = pieces[0].shape
    R, K, n = b3.shape
    assert P * W == R * n
    tm, tko, tc = _pick(M, M_TILES), _pick(K, WIDE_TILES), _pick(n, WIDE_TILES)
    ncb = n // tc
    nc = R * ncb
    per = W // tc

    def compute(*refs):
        a_refs, b_ref, o_ref, acc = refs[:P], refs[P], refs[P + 1], refs[P + 2]
        k = pl.program_id(2)

        @pl.when(k == 0)
        def _():
            acc[...] = jnp.zeros_like(acc)

        def add(q):
            acc[...] += lax.dot_general(a_refs[q][...], b_ref[0], (((1,), (1,)), ((), ())),
                                        preferred_element_type=F32)

        _for_piece(P, per, k, add)

        @pl.when(k == nc - 1)
        def _():
            o_ref[...] = acc[...].astype(o_ref.dtype)

    return _call(
        compute, name=name, grid=(M // tm, K // tko, nc),
        in_specs=_piece_specs(P, per, tm, tc, lambda i, j, k: i, lambda i, j, k: k)
        + [pl.BlockSpec((1, tko, tc), lambda i, j, k: (k // ncb, j, k % ncb))],
        out_specs=[pl.BlockSpec((tm, tko), lambda i, j, k: (i, j))],
        out_shape=[jax.ShapeDtypeStruct((M, K), out_dtype)],
        scratch=[pltpu.VMEM((tm, tko), F32)],
        operands=(*pieces, b3), sem=("parallel", "parallel", "arbitrary"), comm=comm)


def mm_tn(a, b, R, out_dtype, name):
    pieces = list(b) if isinstance(b, (list, tuple)) else [b]
    P = len(pieces)
    M, K = a.shape
    W = pieces[0].shape[1]
    n = P * W // R
    tm, tk, tn = _pick(M, (1408,) + M_TILES), _pick(K, MID_TILES), _pick(n, MID_TILES)
    nb, nm = n // tn, M // tm
    per = W // tn

    def compute(*refs):
        a_ref, b_refs, o_ref, acc = refs[0], refs[1:1 + P], refs[1 + P], refs[2 + P]
        m = pl.program_id(2)

        @pl.when(m == 0)
        def _():
            acc[...] = jnp.zeros_like(acc)

        def add(q):
            acc[...] += lax.dot_general(a_ref[...], b_refs[q][...], (((0,), (0,)), ((), ())),
                                        preferred_element_type=F32)

        _for_piece(P, per, pl.program_id(1), add)

        @pl.when(m == nm - 1)
        def _():
            o_ref[0] = acc[...].astype(o_ref.dtype)

    return _call(
        compute, name=name, grid=(K // tk, R * nb, nm),
        in_specs=[pl.BlockSpec((tm, tk), lambda i, j, m: (m, i))]
        + _piece_specs(P, per, tm, tn, lambda i, j, m: m, lambda i, j, m: j),
        out_specs=[pl.BlockSpec((1, tk, tn), lambda i, j, m: (j // nb, i, j % nb))],
        out_shape=[jax.ShapeDtypeStruct((R, K, n), out_dtype)],
        scratch=[pltpu.VMEM((tk, tn), F32)],
        operands=(a, *pieces), sem=("parallel", "parallel", "arbitrary"))[0]


def _seg_spec(D):
    return pl.BlockSpec((None, 1, D), lambda i: (jnp.minimum(i, 1), 0, 0))


def _seg3(a):
    return a.reshape(2, 1, a.shape[-1])


def _row_spec(w, col=0):
    return pl.BlockSpec((ROW_TILE, w), lambda i: (i, col))


def _acc_spec(r, w):
    return pl.BlockSpec((r, w), lambda i: (0, 0))


def _seg_accumulate(ref, i, val):
    ref[0:1, :] += jnp.where(i == 0, val, 0.0)
    ref[1:2, :] += jnp.where(i == 0, 0.0, val)


def rms_mod_fwd(x, g, shift2, scale2, name):
    T, D = x.shape

    def compute(x_ref, g_ref, sh_ref, sc_ref, h_ref):
        xv = x_ref[...]
        rstd = lax.rsqrt(jnp.mean(xv * xv, axis=-1, keepdims=True) + EPS)
        h = (xv * rstd * g_ref[...]) * (1.0 + sc_ref[...]) + sh_ref[...]
        h_ref[...] = h.astype(h_ref.dtype)

    return _call(compute, name=name, grid=(T // ROW_TILE,),
                 in_specs=[_row_spec(D), _acc_spec(1, D), _seg_spec(D), _seg_spec(D)],
                 out_specs=[_row_spec(D)], out_shape=[jax.ShapeDtypeStruct((T, D), BF16)],
                 operands=(x, g, _seg3(shift2), _seg3(scale2)), sem=("parallel",))[0]


def rms_mod_bwd(x, g, scale2, dh, dres, name):
    T, D = x.shape

    def compute(x_ref, g_ref, sc_ref, dh_ref, dres_ref, dx_ref, dg_ref, dsh_ref, dsc_ref):
        i = pl.program_id(0)

        @pl.when(i == 0)
        def _():
            dg_ref[...] = jnp.zeros_like(dg_ref)
            dsh_ref[...] = jnp.zeros_like(dsh_ref)
            dsc_ref[...] = jnp.zeros_like(dsc_ref)

        xv = x_ref[...]
        dh = dh_ref[...].astype(F32)
        gv = g_ref[...]
        rstd = lax.rsqrt(jnp.mean(xv * xv, axis=-1, keepdims=True) + EPS)
        xh = xv * rstd
        u = dh * (1.0 + sc_ref[...])
        dg_ref[...] += jnp.sum(u * xh, axis=0, keepdims=True)
        _seg_accumulate(dsh_ref, i, jnp.sum(dh, axis=0, keepdims=True))
        _seg_accumulate(dsc_ref, i, jnp.sum(dh * xh * gv, axis=0, keepdims=True))
        dxh = u * gv
        dx = rstd * (dxh - xh * jnp.mean(dxh * xh, axis=-1, keepdims=True))
        dx_ref[...] = dres_ref[...] + dx

    return _call(compute, name=name, grid=(T // ROW_TILE,),
                 in_specs=[_row_spec(D), _acc_spec(1, D), _seg_spec(D), _row_spec(D), _row_spec(D)],
                 out_specs=[_row_spec(D), _acc_spec(1, D), _acc_spec(2, D), _acc_spec(2, D)],
                 out_shape=[jax.ShapeDtypeStruct((T, D), F32), jax.ShapeDtypeStruct((1, D), F32),
                            jax.ShapeDtypeStruct((2, D), F32), jax.ShapeDtypeStruct((2, D), F32)],
                 operands=(x, g, _seg3(scale2), dh, dres), sem=("arbitrary",))


def gate_res_fwd(x, y, gate2, name):
    T, D = x.shape

    def compute(x_ref, y_ref, g_ref, o_ref):
        o_ref[...] = x_ref[...] + g_ref[...] * y_ref[...]

    return _call(compute, name=name, grid=(T // ROW_TILE,),
                 in_specs=[_row_spec(D), _row_spec(D), _seg_spec(D)],
                 out_specs=[_row_spec(D)], out_shape=[jax.ShapeDtypeStruct((T, D), F32)],
                 operands=(x, y, _seg3(gate2)), sem=("parallel",))[0]


def gate_bwd(dxo, y, gate2, name):
    T, D = dxo.shape

    def compute(d_ref, y_ref, g_ref, dy_ref, dg_ref):
        i = pl.program_id(0)

        @pl.when(i == 0)
        def _():
            dg_ref[...] = jnp.zeros_like(dg_ref)

        d = d_ref[...]
        dy_ref[...] = (d * g_ref[...]).astype(dy_ref.dtype)
        _seg_accumulate(dg_ref, i, jnp.sum(d * y_ref[...], axis=0, keepdims=True))

    return _call(compute, name=name, grid=(T // ROW_TILE,),
                 in_specs=[_row_spec(D), _row_spec(D), _seg_spec(D)],
                 out_specs=[_row_spec(D), _acc_spec(2, D)],
                 out_shape=[jax.ShapeDtypeStruct((T, D), BF16), jax.ShapeDtypeStruct((2, D), F32)],
                 operands=(dxo, y, _seg3(gate2)), sem=("arbitrary",))


def final_loss(x, g, target, name):
    T, D = x.shape

    def compute(x_ref, g_ref, t_ref, dx_ref, dg_ref, loss_ref):
        i = pl.program_id(0)

        @pl.when(i == 0)
        def _():
            dg_ref[...] = jnp.zeros_like(dg_ref)
            loss_ref[...] = jnp.zeros_like(loss_ref)
            dx_ref[...] = jnp.zeros_like(dx_ref)

        @pl.when(i > 0)
        def _():
            xv = x_ref[...]
            gv = g_ref[...]
            rstd = lax.rsqrt(jnp.mean(xv * xv, axis=-1, keepdims=True) + EPS)
            xh = xv * rstd
            err = xh * gv - t_ref[...]
            loss_ref[...] += 0.5 * jnp.sum(jnp.mean(err * err, axis=-1, keepdims=True))
            dy = err * (1.0 / D)
            dg_ref[...] += jnp.sum(dy * xh, axis=0, keepdims=True)
            dxh = dy * gv
            dx_ref[...] = rstd * (dxh - xh * jnp.mean(dxh * xh, axis=-1, keepdims=True))

    return _call(compute, name=name, grid=(T // ROW_TILE,),
                 in_specs=[_row_spec(D), _acc_spec(1, D),
                           pl.BlockSpec((ROW_TILE, D), lambda i: (jnp.maximum(i - 1, 0), 0))],
                 out_specs=[_row_spec(D), _acc_spec(1, D), _acc_spec(8, 128)],
                 out_shape=[jax.ShapeDtypeStruct((T, D), F32), jax.ShapeDtypeStruct((1, D), F32),
                            jax.ShapeDtypeStruct((8, 128), F32)],
                 operands=(x, g, target), sem=("arbitrary",))


def _swap32(v):
    lane = lax.broadcasted_iota(jnp.int32, v.shape, 1)
    return jnp.where((lane & 63) < 32, pltpu.roll(v, 96, 1), pltpu.roll(v, 32, 1))


def _rope(v, cos, sin):
    return v * cos + _swap32(v) * sin


def _rope_t(d, cos, sin):
    return d * cos + _swap32(d * sin)


def _chunk_of(step, n_chunks, n_ctx, rev):
    if not rev:
        return step
    return jnp.where(step < n_ctx, n_ctx - 1 - step, n_chunks + n_ctx - 1 - step)


def _dot_t0(a, b):
    return lax.dot_general(a, b, (((0,), (0,)), ((), ())), preferred_element_type=F32)


def _dot_t1(a, b):
    return lax.dot_general(a, b, (((1,), (1,)), ((), ())), preferred_element_type=F32)


def _dot(a, b):
    return jnp.dot(a, b, preferred_element_type=F32)


def ret_fwd(p, cos, sin, tabs, n_heads, n_ctx, rev, name):
    T = p.shape[0]
    C = RET_CHUNK
    H = n_heads
    NC = T // C
    scale = C ** -0.5

    def compute(q_ref, k_ref, v_ref, cos_ref, sin_ref, dm_ref, qd_ref, kd_ref, cd_ref, o_ref, s_ref, S):
        t = pl.program_id(0)

        @pl.when(t == 0)
        def _():
            S[...] = jnp.zeros_like(S)

        cs, sn = cos_ref[...], sin_ref[...]
        for h in range(H):
            sl = slice(h * C, (h + 1) * C)
            q = _rope(q_ref[:, sl], cs, sn)
            k = _rope(k_ref[:, sl], cs, sn) * scale
            qb, kb, vb = q.astype(BF16), k.astype(BF16), v_ref[:, sl].astype(BF16)
            A = _dot_t1(qb, kb) * dm_ref[h]
            s_in = S[h]
            s_ref[h, 0] = s_in
            o_ref[:, sl] = _dot(A.astype(BF16), vb) + _dot(qb, s_in.astype(BF16)) * qd_ref[h]
            S[h] = s_in * cd_ref[h] + _dot_t0((k * kd_ref[h]).astype(BF16), vb)

    cmap = lambda t: _chunk_of(t, NC, n_ctx, rev)
    blk = lambda col: pl.BlockSpec((C, H * C), lambda t: (cmap(t), col))
    tab = pl.BlockSpec((C, C), lambda t: (cmap(t), 0))
    htab = lambda r: pl.BlockSpec((H, r, C), lambda t: (0, 0, 0))
    return _call(
        compute, name=name, grid=(NC,),
        in_specs=[blk(0), blk(1), blk(2), tab, tab, htab(C), htab(C), htab(C), htab(1)],
        out_specs=[blk(0), pl.BlockSpec((H, 1, C, C), lambda t: (0, cmap(t), 0, 0))],
        out_shape=[jax.ShapeDtypeStruct((T, H * C), F32), jax.ShapeDtypeStruct((H, NC, C, C), F32)],
        scratch=[pltpu.VMEM((H, C, C), F32)],
        operands=(p, p, p, cos, sin, tabs["dm"], tabs["qd"], tabs["kd"], tabs["cd"]),
        sem=("arbitrary",))


def ret_bwd(p, cos, sin, tabs, do, s_saved, n_heads, n_ctx, rev, name, comm=None):
    T = p.shape[0]
    C = RET_CHUNK
    H = n_heads
    NC = T // C
    scale = C ** -0.5

    def compute(q_ref, k_ref, v_ref, cos_ref, sin_ref, dm_ref, qd_ref, kd_ref, cd_ref, em_ref, eq_ref,
                ek_ref, do_ref, s_ref, dq_ref, dk_ref, dv_ref, dlg_ref, dS):
        t = pl.program_id(0)

        @pl.when(t == 0)
        def _():
            dS[...] = jnp.zeros_like(dS)
            dlg_ref[...] = jnp.zeros_like(dlg_ref)

        cs, sn = cos_ref[...], sin_ref[...]
        for h in range(H):
            sl = slice(h * C, (h + 1) * C)
            q = _rope(q_ref[:, sl], cs, sn)
            k = _rope(k_ref[:, sl], cs, sn) * scale
            qb, kb, vb = q.astype(BF16), k.astype(BF16), v_ref[:, sl].astype(BF16)
            dmv, qdv, kdv, cdv = dm_ref[h], qd_ref[h], kd_ref[h], cd_ref[h]
            A = _dot_t1(qb, kb) * dmv
            s_in = s_ref[h, 0]
            sb = s_in.astype(BF16)
            ds_out = dS[h]
            dsb = ds_out.astype(BF16)
            dov = do_ref[:, sl]
            dob = dov.astype(BF16)
            dA = _dot_t1(dob, vb)
            dPb = (dA * dmv).astype(BF16)
            doq = dov * qdv
            doqb = doq.astype(BF16)
            kk = k * kdv
            vds = _dot_t1(vb, dsb)
            dq_ref[:, sl] = _dot(dPb, kb) + _dot_t1(doqb, sb)
            dk_ref[:, sl] = _dot_t0(dPb, qb) + vds * kdv
            dv_ref[:, sl] = _dot_t0(A.astype(BF16), dob) + _dot(kk.astype(BF16), dsb)
            dS[h] = ds_out * cdv + _dot_t0(qb, doqb)
            o2 = _dot(qb, sb)
            part = (jnp.sum(dA * A * em_ref[...], axis=0, keepdims=True)
                    + jnp.sum(eq_ref[...] * doq * o2, axis=0, keepdims=True)
                    + jnp.sum(ek_ref[...] * kk * vds, axis=0, keepdims=True)
                    + float(C) * cdv * jnp.sum(s_in * ds_out, axis=0, keepdims=True))
            dlg_ref[h, 0:1, :] += part

    cmap = lambda t: _chunk_of(NC - 1 - t, NC, n_ctx, rev)
    blk = lambda col: pl.BlockSpec((C, H * C), lambda t: (cmap(t), col))
    tab = pl.BlockSpec((C, C), lambda t: (cmap(t), 0))
    const = pl.BlockSpec((C, C), lambda t: (0, 0))
    htab = lambda r: pl.BlockSpec((H, r, C), lambda t: (0, 0, 0))
    return _call(
        compute, name=name, grid=(NC,),
        in_specs=[blk(0), blk(1), blk(2), tab, tab, htab(C), htab(C), htab(C), htab(1), const, const,
                  const, blk(0), pl.BlockSpec((H, 1, C, C), lambda t: (0, cmap(t), 0, 0))],
        out_specs=[blk(0), blk(0), blk(0), pl.BlockSpec((H, 8, C), lambda t: (0, 0, 0))],
        out_shape=[jax.ShapeDtypeStruct((T, H * C), F32)] * 3 + [jax.ShapeDtypeStruct((H, 8, C), F32)],
        scratch=[pltpu.VMEM((H, C, C), F32)],
        operands=(p, p, p, cos, sin, tabs["dm"], tabs["qd"], tabs["kd"], tabs["cd"], tabs["em"], tabs["eq"],
                  tabs["ek"], do, s_saved),
        sem=("arbitrary",), comm=comm)


def ret_out_fwd(o_f, o_b, p, n_heads, name):
    T, RW = o_f.shape
    C = RET_CHUNK

    def compute(of_ref, ob_ref, g_ref, out_ref):
        for h in range(n_heads):
            sl = slice(h * C, (h + 1) * C)
            o = of_ref[:, sl] + ob_ref[:, sl]
            r = o * lax.rsqrt(jnp.mean(o * o, axis=-1, keepdims=True) + EPS)
            g = g_ref[:, sl]
            out_ref[:, sl] = (g * _sigmoid(g) * r).astype(out_ref.dtype)

    return _call(compute, name=name, grid=(T // ROW_TILE,),
                 in_specs=[_row_spec(RW), _row_spec(RW), _row_spec(RW, 3)],
                 out_specs=[_row_spec(RW)], out_shape=[jax.ShapeDtypeStruct((T, RW), BF16)],
                 operands=(o_f, o_b, p), sem=("parallel",))[0]


def ret_out_bwd(o_f, o_b, p, dmix, n_heads, name):
    T, RW = o_f.shape
    C = RET_CHUNK

    def compute(of_ref, ob_ref, g_ref, d_ref, do_ref, dg_ref):
        for h in range(n_heads):
            sl = slice(h * C, (h + 1) * C)
            o = of_ref[:, sl] + ob_ref[:, sl]
            rstd = lax.rsqrt(jnp.mean(o * o, axis=-1, keepdims=True) + EPS)
            r = o * rstd
            g = g_ref[:, sl]
            sg = _sigmoid(g)
            d = d_ref[:, sl].astype(F32)
            dg_ref[:, sl] = (d * r * sg * (1.0 + g * (1.0 - sg))).astype(dg_ref.dtype)
            dr = d * g * sg
            do_ref[:, sl] = rstd * (dr - r * jnp.mean(dr * r, axis=-1, keepdims=True))

    return _call(compute, name=name, grid=(T // ROW_TILE,),
                 in_specs=[_row_spec(RW), _row_spec(RW), _row_spec(RW, 3), _row_spec(RW, 0)],
                 out_specs=[_row_spec(RW), _row_spec(RW)],
                 out_shape=[jax.ShapeDtypeStruct((T, RW), F32), jax.ShapeDtypeStruct((T, RW), BF16)],
                 operands=(o_f, o_b, p, dmix), sem=("parallel",))


def ret_qkv_grad(dqf, dqb, dkf, dkb, dvf, dvb, cos, sin, n_heads, name):
    T, RW = dqf.shape
    C = RET_CHUNK
    scale = C ** -0.5

    def compute(qf, qb, kf, kb, vf, vb, cos_ref, sin_ref, dq_ref, dk_ref, dv_ref):
        cs, sn = cos_ref[...], sin_ref[...]
        for h in range(n_heads):
            sl = slice(h * C, (h + 1) * C)
            dq_ref[:, sl] = _rope_t(qf[:, sl] + qb[:, sl], cs, sn).astype(dq_ref.dtype)
            dk_ref[:, sl] = _rope_t((kf[:, sl] + kb[:, sl]) * scale, cs, sn).astype(dk_ref.dtype)
            dv_ref[:, sl] = (vf[:, sl] + vb[:, sl]).astype(dv_ref.dtype)

    return _call(compute, name=name, grid=(T // ROW_TILE,),
                 in_specs=[_row_spec(RW)] * 6 + [_row_spec(C), _row_spec(C)],
                 out_specs=[_row_spec(RW)] * 3, out_shape=[jax.ShapeDtypeStruct((T, RW), BF16)] * 3,
                 operands=(dqf, dqb, dkf, dkb, dvf, dvb, cos, sin), sem=("parallel",))


def _halo_specs(width, col, halo, n_rows):
    per = ROW_TILE // halo
    last = n_rows // halo - 1
    return [pl.BlockSpec((halo, width), lambda i: (jnp.maximum(i * per - 1, 0), col)),
            pl.BlockSpec((ROW_TILE, width), lambda i: (i, col)),
            pl.BlockSpec((halo, width), lambda i: (jnp.minimum((i + 1) * per, last), col))]


def _halo_valid(i, n_tiles):
    return i >= 2, jnp.logical_and(i >= 1, i <= n_tiles - 2)


def _shifted_copies(S):
    base = S[0]
    rows = base.shape[0]
    for s in range(1, 8):
        S[s] = pltpu.roll(base, rows - s, 0)


def _shifted_rows(S, start, n, cols=slice(None)):
    s = start % 8
    return S[s, start - s:start - s + n, cols]


def _fma_groups(acc, w_row, window):
    w8 = jnp.broadcast_to(w_row, acc[0].shape)
    return [a + w8 * window(g) for g, a in enumerate(acc)]


def _fold8(v):
    parts = [v[r:r + 8] for r in range(0, v.shape[0], 8)]
    while len(parts) > 1:
        parts = [parts[k] + parts[k + 1] for k in range(0, len(parts) - 1, 2)] + (parts[-1:] if len(parts) % 2 else [])
    return parts[0]


def conv_fwd(p, w, bias, ln_g, ln_b, name):
    T = p.shape[0]
    CW = w.shape[1]
    NT = T // ROW_TILE
    HL = CONV_HALO
    PAD = CONV_K // 2

    def compute(ap, ac, an, bp, bc, bn, w_ref, b_ref, g_ref, be_ref, u2_ref, out_ref, US):
        i = pl.program_id(0)
        vp, vn = _halo_valid(i, NT)
        US[0, 0:HL, :] = jnp.where(vp, ap[...] * _sigmoid(bp[...]), 0.0)
        US[0, HL:HL + ROW_TILE, :] = ac[...] * _sigmoid(bc[...])
        US[0, HL + ROW_TILE:, :] = jnp.where(vn, an[...] * _sigmoid(bn[...]), 0.0)
        _shifted_copies(US)
        for r0 in range(0, ROW_TILE, CONV_ROWS):
            acc = [jnp.zeros((8, CW), F32) + b_ref[...]] * (CONV_ROWS // 8)
            for j in range(CONV_K):
                acc = _fma_groups(acc, w_ref[j:j + 1, :],
                                  lambda g: _shifted_rows(US, HL - PAD + j + r0 + 8 * g, 8))
            acc = jnp.concatenate(acc, axis=0)
            u2_ref[r0:r0 + CONV_ROWS, :] = acc
            mu = jnp.mean(acc, axis=-1, keepdims=True)
            xc = acc - mu
            rstd = lax.rsqrt(jnp.mean(xc * xc, axis=-1, keepdims=True) + EPS)
            ln = xc * rstd * g_ref[...] + be_ref[...]
            out_ref[r0:r0 + CONV_ROWS, :] = (ln * _sigmoid(ln)).astype(out_ref.dtype)

    vec = _acc_spec(1, CW)
    return _call(compute, name=name, grid=(NT,),
                 in_specs=_halo_specs(CW, 4, HL, T) + _halo_specs(CW, 5, HL, T) + [_acc_spec(CONV_K, CW), vec, vec, vec],
                 out_specs=[_row_spec(CW), _row_spec(CW)],
                 out_shape=[jax.ShapeDtypeStruct((T, CW), F32), jax.ShapeDtypeStruct((T, CW), BF16)],
                 scratch=[pltpu.VMEM((8, ROW_TILE + 2 * HL, CW), F32)],
                 operands=(p, p, p, p, p, p, w, bias, ln_g, ln_b), sem=("parallel",))


def conv_bwd_ln(u2, dmix, ln_g, ln_b, name):
    T, CW = u2.shape

    def compute(u_ref, d_ref, g_ref, be_ref, du_ref, dg_ref, db_ref, dbias_ref):
        i = pl.program_id(0)

        @pl.when(i == 0)
        def _():
            dg_ref[...] = jnp.zeros_like(dg_ref)
            db_ref[...] = jnp.zeros_like(db_ref)
            dbias_ref[...] = jnp.zeros_like(dbias_ref)

        u = u_ref[...]
        gv = g_ref[...]
        mu = jnp.mean(u, axis=-1, keepdims=True)
        xc = u - mu
        rstd = lax.rsqrt(jnp.mean(xc * xc, axis=-1, keepdims=True) + EPS)
        xh = xc * rstd
        ln = xh * gv + be_ref[...]
        sg = _sigmoid(ln)
        dln = d_ref[...].astype(F32) * sg * (1.0 + ln * (1.0 - sg))
        dg_ref[...] += jnp.sum(dln * xh, axis=0, keepdims=True)
        db_ref[...] += jnp.sum(dln, axis=0, keepdims=True)
        dxh = dln * gv
        du = rstd * (dxh - jnp.mean(dxh, axis=-1, keepdims=True)
                     - xh * jnp.mean(dxh * xh, axis=-1, keepdims=True))
        du_ref[...] = du
        dbias_ref[...] += jnp.sum(du, axis=0, keepdims=True)

    vec = _acc_spec(1, CW)
    return _call(compute, name=name, grid=(T // ROW_TILE,),
                 in_specs=[_row_spec(CW), _row_spec(CW, 1), vec, vec],
                 out_specs=[_row_spec(CW), vec, vec, vec],
                 out_shape=[jax.ShapeDtypeStruct((T, CW), F32)] + [jax.ShapeDtypeStruct((1, CW), F32)] * 3,
                 operands=(u2, dmix, ln_g, ln_b), sem=("arbitrary",))


def conv_bwd_taps(p, du2, w, name):
    T = p.shape[0]
    CW = w.shape[1]
    NT = T // ROW_TILE
    HL = CONV_HALO
    PAD = CONV_K // 2

    def compute(ap, ac, an, bp, bc, bn, dp, dc, dn, w_ref, da_ref, db_ref, dw_ref, US, DUS):
        i = pl.program_id(0)

        @pl.when(i == 0)
        def _():
            dw_ref[...] = jnp.zeros_like(dw_ref)

        vp, vn = _halo_valid(i, NT)
        US[0, 0:HL, :] = jnp.where(vp, ap[...] * _sigmoid(bp[...]), 0.0)
        US[0, HL:HL + ROW_TILE, :] = ac[...] * _sigmoid(bc[...])
        US[0, HL + ROW_TILE:, :] = jnp.where(vn, an[...] * _sigmoid(bn[...]), 0.0)
        DUS[0, 0:HL, :] = jnp.where(vp, dp[...], 0.0)
        DUS[0, HL:HL + ROW_TILE, :] = dc[...]
        DUS[0, HL + ROW_TILE:, :] = jnp.where(vn, dn[...], 0.0)
        _shifted_copies(US)
        _shifted_copies(DUS)
        for r0 in range(0, ROW_TILE, CONV_ROWS):
            du = [jnp.zeros((8, CW), F32)] * (CONV_ROWS // 8)
            for j in range(CONV_K):
                du = _fma_groups(du, w_ref[j:j + 1, :],
                                 lambda g: _shifted_rows(DUS, HL + PAD - j + r0 + 8 * g, 8))
            du = jnp.concatenate(du, axis=0)
            a = ac[r0:r0 + CONV_ROWS, :]
            sg = _sigmoid(bc[r0:r0 + CONV_ROWS, :])
            da_ref[r0:r0 + CONV_ROWS, :] = (du * sg).astype(da_ref.dtype)
            db_ref[r0:r0 + CONV_ROWS, :] = (du * a * sg * (1.0 - sg)).astype(db_ref.dtype)
        for c0 in range(0, CW, 128):
            cols = slice(c0, c0 + 128)
            accs = [jnp.zeros((8, 128), F32)] * CONV_K
            for r0 in range(0, ROW_TILE, CONV_ROWS):
                d = DUS[0, HL + r0:HL + r0 + CONV_ROWS, cols]
                accs = [acc + _fold8(d * _shifted_rows(US, HL - PAD + j + r0, CONV_ROWS, cols))
                        for j, acc in enumerate(accs)]
            for j in range(CONV_K):
                dw_ref[j:j + 1, cols] += jnp.sum(accs[j], axis=0, keepdims=True)

    return _call(compute, name=name, grid=(NT,),
                 in_specs=(_halo_specs(CW, 4, HL, T) + _halo_specs(CW, 5, HL, T) + _halo_specs(CW, 0, HL, T)
                           + [_acc_spec(CONV_K, CW)]),
                 out_specs=[_row_spec(CW), _row_spec(CW), _acc_spec(CONV_K, CW)],
                 out_shape=[jax.ShapeDtypeStruct((T, CW), BF16), jax.ShapeDtypeStruct((T, CW), BF16),
                            jax.ShapeDtypeStruct((CONV_K, CW), F32)],
                 scratch=[pltpu.VMEM((8, ROW_TILE + 2 * HL, CW), F32)] * 2,
                 operands=(p, p, p, p, p, p, du2, du2, du2, w), sem=("arbitrary",))


def _ffn_halo_specs(tc, col0, n_rows):
    per = ROW_TILE // FFN_HALO
    last = n_rows // FFN_HALO - 1
    return [pl.BlockSpec((FFN_HALO, tc), lambda cb, i: (jnp.maximum(i * per - 1, 0), col0 + cb)),
            pl.BlockSpec((ROW_TILE, tc), lambda cb, i: (i, col0 + cb)),
            pl.BlockSpec((FFN_HALO, tc), lambda cb, i: (jnp.minimum((i + 1) * per, last), col0 + cb))]


def _ffn_fill(S, prev, cur, nxt, i, n_tiles):
    vp, vn = _halo_valid(i, n_tiles)
    HL = FFN_HALO
    S[1, 0:HL, :] = jnp.where(vp, prev[...].astype(F32), 0.0)
    S[1, HL:HL + ROW_TILE, :] = cur[...].astype(F32)
    S[1, HL + ROW_TILE:, :] = jnp.where(vn, nxt[...].astype(F32), 0.0)
    base = S[1]
    rows = base.shape[0]
    is_lat = i >= 1
    col = lax.broadcasted_iota(jnp.int32, base.shape, 0) & (GRID_W - 1)
    S[0] = jnp.where(jnp.logical_and(is_lat, col == 0), 0.0, pltpu.roll(base, 1, 0))
    S[2] = jnp.where(jnp.logical_and(is_lat, col == GRID_W - 1), 0.0, pltpu.roll(base, rows - 1, 0))


def _ffn_row_factor(i, di):
    return 1.0 if di == 1 else jnp.where(i >= 1, 1.0, 0.0)


def ffn_conv_fwd(up, w9, bias, name):
    T = up.shape[0]
    DFF = w9.shape[1]
    tc = _pick(DFF, (512, 256, 128))
    ncb = DFF // tc
    NT = T // ROW_TILE
    HL = FFN_HALO

    def compute(gp, gc, gn, val_ref, w_ref, b_ref, cg_ref, act_ref, G):
        i = pl.program_id(1)
        _ffn_fill(G, gp, gc, gn, i, NT)
        for r0 in range(0, ROW_TILE, FFN_ROWS):
            acc = [jnp.zeros((8, tc), F32) + b_ref[...]] * (FFN_ROWS // 8)
            for di in range(3):
                for dj in range(3):
                    wt = w_ref[3 * di + dj:3 * di + dj + 1, :] * _ffn_row_factor(i, di)
                    lo = HL + r0 + (di - 1) * GRID_W
                    acc = _fma_groups(acc, wt, lambda g: G[dj, lo + 8 * g:lo + 8 * g + 8, :])
            acc = jnp.concatenate(acc, axis=0)
            cg_ref[r0:r0 + FFN_ROWS, :] = acc.astype(cg_ref.dtype)
            val = val_ref[r0:r0 + FFN_ROWS, :].astype(F32)
            act_ref[r0:r0 + FFN_ROWS, :] = (acc * _sigmoid(acc) * val).astype(act_ref.dtype)

    tile = pl.BlockSpec((ROW_TILE, tc), lambda cb, i: (i, cb))
    return _call(compute, name=name, grid=(ncb, NT),
                 in_specs=_ffn_halo_specs(tc, 0, T) + [pl.BlockSpec((ROW_TILE, tc), lambda cb, i: (i, ncb + cb)),
                                                       pl.BlockSpec((9, tc), lambda cb, i: (0, cb)),
                                                       pl.BlockSpec((1, tc), lambda cb, i: (0, cb))],
                 out_specs=[tile, tile],
                 out_shape=[jax.ShapeDtypeStruct((T, DFF), BF16), jax.ShapeDtypeStruct((T, DFF), BF16)],
                 scratch=[pltpu.VMEM((3, ROW_TILE + 2 * HL, tc), F32)],
                 operands=(up, up, up, up, w9, bias), sem=("parallel", "parallel"))


def ffn_conv_bwd_act(cg, up, dact, name):
    T, DFF = cg.shape
    tc = _pick(DFF, (512, 256, 128))
    ncb = DFF // tc

    def compute(cg_ref, val_ref, d_ref, dcg_ref, dval_ref, db_ref):
        i = pl.program_id(1)

        @pl.when(i == 0)
        def _():
            db_ref[...] = jnp.zeros_like(db_ref)

        c = cg_ref[...].astype(F32)
        sg = _sigmoid(c)
        d = d_ref[...].astype(F32)
        dval_ref[...] = (d * c * sg).astype(dval_ref.dtype)
        dcg = d * val_ref[...].astype(F32) * sg * (1.0 + c * (1.0 - sg))
        dcg_ref[...] = dcg.astype(dcg_ref.dtype)
        db_ref[...] += jnp.sum(dcg, axis=0, keepdims=True)

    tile = pl.BlockSpec((ROW_TILE, tc), lambda cb, i: (i, cb))
    return _call(compute, name=name, grid=(ncb, T // ROW_TILE),
                 in_specs=[tile, pl.BlockSpec((ROW_TILE, tc), lambda cb, i: (i, ncb + cb)), tile],
                 out_specs=[tile, tile, pl.BlockSpec((1, tc), lambda cb, i: (0, cb))],
                 out_shape=[jax.ShapeDtypeStruct((T, DFF), BF16), jax.ShapeDtypeStruct((T, DFF), BF16),
                            jax.ShapeDtypeStruct((1, DFF), F32)],
                 operands=(cg, up, dact), sem=("parallel", "arbitrary"))


def ffn_conv_bwd_taps(up, dcg, w9, name, comm=None):
    T, DFF = dcg.shape
    tc = _pick(DFF, (512, 256, 128))
    ncb = DFF // tc
    NT = T // ROW_TILE
    HL = FFN_HALO

    def compute(gp, gc, gn, dp, dc, dn, w_ref, dgate_ref, dw_ref, G, DC):
        i = pl.program_id(1)

        @pl.when(i == 0)
        def _():
            dw_ref[...] = jnp.zeros_like(dw_ref)

        _ffn_fill(G, gp, gc, gn, i, NT)
        _ffn_fill(DC, dp, dc, dn, i, NT)
        for r0 in range(0, ROW_TILE, FFN_ROWS):
            dg = [jnp.zeros((8, tc), F32)] * (FFN_ROWS // 8)
            for di in range(3):
                for dj in range(3):
                    wt = w_ref[3 * di + dj:3 * di + dj + 1, :] * _ffn_row_factor(i, di)
                    lo = HL + r0 - (di - 1) * GRID_W
                    dg = _fma_groups(dg, wt, lambda g: DC[2 - dj, lo + 8 * g:lo + 8 * g + 8, :])
            dgate_ref[r0:r0 + FFN_ROWS, :] = jnp.concatenate(dg, axis=0).astype(dgate_ref.dtype)
        for c0 in range(0, tc, 128):
            cols = slice(c0, c0 + 128)
            accs = [jnp.zeros((8, 128), F32)] * 9
            for r0 in range(0, ROW_TILE, FFN_ROWS):
                d = DC[1, HL + r0:HL + r0 + FFN_ROWS, cols]
                for di in range(3):
                    lo = HL + r0 + (di - 1) * GRID_W
                    for dj in range(3):
                        accs[3 * di + dj] = accs[3 * di + dj] + _fold8(d * G[dj, lo:lo + FFN_ROWS, cols])
            for di in range(3):
                for dj in range(3):
                    t = 3 * di + dj
                    dw_ref[t:t + 1, cols] += _ffn_row_factor(i, di) * jnp.sum(accs[t], axis=0, keepdims=True)

    tile = pl.BlockSpec((ROW_TILE, tc), lambda cb, i: (i, cb))
    return _call(compute, name=name, grid=(ncb, NT),
                 in_specs=_ffn_halo_specs(tc, 0, T) + _ffn_halo_specs(tc, 0, T) + [pl.BlockSpec((9, tc), lambda cb, i: (0, cb))],
                 out_specs=[tile, pl.BlockSpec((9, tc), lambda cb, i: (0, cb))],
                 out_shape=[jax.ShapeDtypeStruct((T, DFF), BF16), jax.ShapeDtypeStruct((9, DFF), F32)],
                 scratch=[pltpu.VMEM((3, ROW_TILE + 2 * HL, tc), F32)] * 2,
                 operands=(up, up, up, dcg, dcg, dcg, w9), sem=("parallel", "arbitrary"), comm=comm)


def _adamw_update(g, w_ref, m_ref, v_ref, g_ref, d_ref, nm_ref, nv_ref):
    c1 = 1.0 - ADAM_B1 ** ADAM_STEP
    c2 = 1.0 - ADAM_B2 ** ADAM_STEP
    nm = ADAM_B1 * m_ref[...] + (1.0 - ADAM_B1) * g
    nv = ADAM_B2 * v_ref[...] + (1.0 - ADAM_B2) * (g * g)
    g_ref[...] = g
    nm_ref[...] = nm
    nv_ref[...] = nv
    d_ref[...] = -ADAM_LR * ((nm / c1) / (jnp.sqrt(nv / c2) + ADAM_EPS) + ADAM_WD * w_ref[...])


def _sum_parts(p_ref, P):
    g = p_ref[0].astype(F32)
    for k in range(1, P):
        g = g + p_ref[k].astype(F32)
    return g


def adamw(parts, w, m, v, name):
    P, R, C = parts.shape
    fits = lambda t: 2 * (P + 7) * t * C * 4 <= ADAM_VMEM_BYTES
    tr = R if fits(R) else _pick(R, [t for t in (1024, 512, 256, 128, 64, 32, 16, 8) if fits(t)])

    def compute(p_ref, w_ref, m_ref, v_ref, g_ref, d_ref, nm_ref, nv_ref):
        _adamw_update(_sum_parts(p_ref, P), w_ref, m_ref, v_ref, g_ref, d_ref, nm_ref, nv_ref)

    tile = pl.BlockSpec((tr, C), lambda i: (i, 0))
    return _call(compute, name=name, grid=(R // tr,),
                 in_specs=[pl.BlockSpec((P, tr, C), lambda i: (0, i, 0)), tile, tile, tile],
                 out_specs=[tile] * 4, out_shape=[jax.ShapeDtypeStruct((R, C), F32)] * 4,
                 operands=(parts, w, m, v), sem=("parallel",))


def adamw_layers(parts_l, w, m, v, name):
    L = len(parts_l)
    P, R, C = parts_l[0].shape
    psize = parts_l[0].dtype.itemsize
    fits = lambda t: 2 * t * C * (L * P * psize + 7 * 4) <= ADAM_VMEM_BYTES
    tr = _pick(R, [t for t in (1024, 512, 256, 128, 64, 32, 16, 8) if fits(t)])

    def compute(*refs):
        p_refs, (w_ref, m_ref, v_ref), outs = refs[:L], refs[L:L + 3], refs[L + 3:]
        for l in range(L):
            @pl.when(pl.program_id(0) == l)
            def _(l=l):
                _adamw_update(_sum_parts(p_refs[l], P), w_ref, m_ref, v_ref, *outs)

    tile = pl.BlockSpec((None, tr, C), lambda l, i: (l, i, 0))
    part = lambda k: pl.BlockSpec((P, tr, C), lambda l, i: (0, jnp.where(l == k, i, 0), 0))
    return _call(compute, name=name, grid=(L, R // tr),
                 in_specs=[part(k) for k in range(L)] + [tile] * 3,
                 out_specs=[tile] * 4, out_shape=[jax.ShapeDtypeStruct((L, R, C), F32)] * 4,
                 operands=(*parts_l, w, m, v), sem=("arbitrary", "arbitrary"))


def _rope_tables(seq, ctx):
    t = jnp.arange(seq)
    quarter = RET_CHUNK // 4
    inv_freq = 1.0 / (ROPE_THETA ** (jnp.arange(0, quarter, dtype=F32) / quarter))
    ang_r = (t // GRID_W).astype(F32)[:, None] * inv_freq[None, :]
    ang_c = (t % GRID_W).astype(F32)[:, None] * inv_freq[None, :]
    cr, sr, cc, sc = jnp.cos(ang_r), jnp.sin(ang_r), jnp.cos(ang_c), jnp.sin(ang_c)
    cos = jnp.concatenate([cr, cr, cc, cc], axis=-1)
    sin = jnp.concatenate([-sr, sr, -sc, sc], axis=-1)
    cos = jnp.concatenate([jnp.ones((ctx, RET_CHUNK), F32), cos], axis=0)
    sin = jnp.concatenate([jnp.zeros((ctx, RET_CHUNK), F32), sin], axis=0)
    return cos, sin


def _decay_tables(decay_logit, rev):
    C = RET_CHUNK
    lg = jax.nn.log_sigmoid(decay_logit.astype(F32))
    idx = jnp.arange(C, dtype=F32)
    diff = idx[:, None] - idx[None, :]
    if rev:
        diff = -diff
        eq, ek = C - idx, idx
    else:
        eq, ek = idx + 1.0, C - 1.0 - idx
    keep = diff >= 0
    em = jnp.where(keep, diff, 0.0)
    bc = lambda e: jnp.broadcast_to(e[:, None], (C, C))
    return {
        "dm": jnp.where(keep[None], jnp.exp(lg[:, None, None] * em[None]), 0.0),
        "qd": jnp.broadcast_to(jnp.exp(lg[:, None] * eq[None, :])[:, :, None], (lg.shape[0], C, C)),
        "kd": jnp.broadcast_to(jnp.exp(lg[:, None] * ek[None, :])[:, :, None], (lg.shape[0], C, C)),
        "cd": jnp.broadcast_to(jnp.exp(lg * C)[:, None, None], (lg.shape[0], 1, C)),
        "em": em, "eq": bc(eq), "ek": bc(ek),
    }


def _silu(z):
    return z * jax.nn.sigmoid(z)


def _dsilu(z):
    s = jax.nn.sigmoid(z)
    return s * (1.0 + z * (1.0 - s))


def kernel(x, c, ctx, c_ctx, w_mod, b_mod, norm1_g, norm2_g, w_in, ret_decay_f, ret_decay_b, conv_dw_w, conv_dw_b, conv_ln_g, conv_ln_b, w_out, ffn_w_up, ffn_dw_w, ffn_dw_b, ffn_w_down, final_norm_g, loss_target, m_c_ctx, m_w_mod, m_b_mod, m_norm1_g, m_norm2_g, m_w_in, m_ret_decay_f, m_ret_decay_b, m_conv_dw_w, m_conv_dw_b, m_conv_ln_g, m_conv_ln_b, m_w_out, m_ffn_w_up, m_ffn_dw_w, m_ffn_dw_b, m_ffn_w_down, m_final_norm_g, v_c_ctx, v_w_mod, v_b_mod, v_norm1_g, v_norm2_g, v_w_in, v_ret_decay_f, v_ret_decay_b, v_conv_dw_w, v_conv_dw_b, v_conv_ln_g, v_conv_ln_b, v_w_out, v_ffn_w_up, v_ffn_dw_w, v_ffn_dw_b, v_ffn_w_down, v_final_norm_g):
    L, D, _ = w_mod.shape
    SEQ, CTX = x.shape[1], ctx.shape[1]
    T = SEQ + CTX
    RW = D // 2
    CW = D - RW
    H = RW // RET_CHUNK
    DFF = ffn_dw_b.shape[1]
    NMOD = b_mod.shape[1] // D
    n_ctx = CTX // RET_CHUNK
    assert CTX == ROW_TILE and RW == CW and SEQ % ROW_TILE == 0 and NMOD == 6
    me = _my_rank()
    wm_n = w_mod.shape[2]
    wo_k, wd_k = w_out.shape[1], ffn_w_down.shape[1]
    cw_n, fw_n = conv_dw_w.shape[2], ffn_dw_w.shape[3]

    w_mod_b = w_mod.astype(BF16)
    w_in_b, w_out_b, w_up_b, w_down_b = (a.astype(BF16) for a in (w_in, w_out, ffn_w_up, ffn_w_down))
    as_rows = lambda g: g.reshape(1, -1, D)
    g_in, g_out, g_up, g_down, g_cw, g_fw, g_c = run_comm(
        "gather", [w_in_b[0], w_out_b[0], w_up_b[0], w_down_b[0], conv_dw_w, ffn_dw_w, _silu(c)], "gather_first")
    w_in_l, w_up_l, w_out_l, w_down_l = [g_in], [g_up], [as_rows(g_out)], [as_rows(g_down)]
    conv_w_l = [jnp.moveaxis(g_cw[:, l], 0, 1).reshape(CONV_K, CW) for l in range(L)]
    ffn_w9_l = [jnp.moveaxis(g_fw[:, l], 0, 2).reshape(9, DFF) for l in range(L)]

    s_cond = jnp.concatenate([g_c.reshape(N_DEV, D), jnp.broadcast_to(_silu(c_ctx)[None], (N_DEV, D))], axis=0)
    s_cond_b = s_cond.astype(BF16)
    mod_shard = mm_nn(s_cond_b, w_mod_b, F32, "mod_fwd")[0]
    (g_mod,) = run_comm("gather", [mod_shard], "gather_mod")
    mod_all = jnp.transpose(g_mod.reshape(N_DEV, 2 * N_DEV, L, wm_n), (2, 1, 0, 3)).reshape(L, 2 * N_DEV, NMOD * D)
    mod_all = mod_all + b_mod[:, None, :]
    mod_lat = lax.dynamic_index_in_dim(mod_all, me, axis=1, keepdims=False)
    mod_ctx = mod_all[:, N_DEV]
    mod2 = jnp.stack([mod_ctx, mod_lat], axis=1).reshape(L, 2, NMOD, D)

    cos, sin = _rope_tables(SEQ, CTX)
    xs = jnp.concatenate([ctx[0], x[0]], axis=0)

    saved = []
    for l in range(L):
        nxt = l + 1 < L
        sh1, sc1, g1, sh2, sc2, g2 = (mod2[l, :, k] for k in range(NMOD))
        tf = _decay_tables(ret_decay_f[l], False)
        tb = _decay_tables(ret_decay_b[l], True)
        h = rms_mod_fwd(xs, norm1_g[l][None], sh1, sc1, "norm1_fwd")
        if nxt:
            (p,), (gi, go) = mm_nn(h, w_in_l[l], F32, "in_proj_g", Comm("gather", [w_in_b[l + 1], w_out_b[l + 1]]))
            w_in_l.append(gi)
            w_out_l.append(as_rows(go))
        else:
            (p,) = mm_nn(h, w_in_l[l], F32, "in_proj")
        o_f, s_f = ret_fwd(p, cos, sin, tf, H, n_ctx, False, "ret_fwd_f")
        o_b, s_b = ret_fwd(p, cos, sin, tb, H, n_ctx, True, "ret_fwd_b")
        mix_r = ret_out_fwd(o_f, o_b, p, H, "ret_out_fwd")
        u2, mix_c = conv_fwd(p, conv_w_l[l], conv_dw_b[l][None], conv_ln_g[l][None], conv_ln_b[l][None], "conv_fwd")
        mix = jnp.concatenate([mix_r, mix_c], axis=1)
        (y1,) = mm_nn(mix, w_out_l[l], F32, "out_proj")
        x2 = gate_res_fwd(xs, y1, g1, "res1_fwd")
        h2 = rms_mod_fwd(x2, norm2_g[l][None], sh2, sc2, "norm2_fwd")
        if nxt:
            (up,), (gu,) = mm_nn(h2, w_up_l[l], BF16, "ffn_up_g", Comm("gather", [w_up_b[l + 1]]))
            w_up_l.append(gu)
        else:
            (up,) = mm_nn(h2, w_up_l[l], BF16, "ffn_up")
        cg, act = ffn_conv_fwd(up, ffn_w9_l[l], ffn_dw_b[l][None], "ffn_conv_fwd")
        if nxt:
            (y2,), (gd,) = mm_nn(act, w_down_l[l], F32, "ffn_down_g", Comm("gather", [w_down_b[l + 1]]))
            w_down_l.append(as_rows(gd))
        else:
            (y2,) = mm_nn(act, w_down_l[l], F32, "ffn_down")
        x3 = gate_res_fwd(x2, y2, g2, "res2_fwd")
        saved.append(dict(x1=xs, h=h, p=p, o_f=o_f, o_b=o_b, s_f=s_f, s_b=s_b, u2=u2, mix=mix, y1=y1, x2=x2,
                          h2=h2, up=up, cg=cg, act=act, y2=y2, tf=tf, tb=tb))
        xs = x3

    dxs, d_final_g, loss_part = final_loss(xs, final_norm_g[None], loss_target[0], "final_loss")
    loss = lax.psum(loss_part[0, 0], ("x", "y", "c"))

    landed = {n: [None] * L for n in ("w_in", "w_out", "ffn_w_up", "ffn_w_down")}
    small = {n: [None] * L for n in ("norm1_g", "norm2_g", "ret_decay_f", "ret_decay_b", "conv_dw_w", "conv_dw_b",
                                     "conv_ln_g", "conv_ln_b", "ffn_dw_w", "ffn_dw_b")}
    dmod2 = [None] * L
    g_in_prev = g_up_prev = None
    for l in reversed(range(L)):
        sv = saved[l]
        sh1, sc1, g1, sh2, sc2, g2 = (mod2[l, :, k] for k in range(NMOD))
        dy2, dg2 = gate_bwd(dxs, sv["y2"], g2, "res2_bwd")
        if g_in_prev is not None:
            (dact,), (landed["w_in"][l + 1],) = mm_nt(dy2, w_down_l[l], BF16, "ffn_down_dx_x", Comm("exchange", [g_in_prev]))
        else:
            (dact,) = mm_nt(dy2, w_down_l[l], BF16, "ffn_down_dx")
        g_down = mm_tn(sv["act"], dy2, 1, BF16, "ffn_down_dw").reshape(N_DEV, wd_k, D)
        dcg, dval, small["ffn_dw_b"][l] = ffn_conv_bwd_act(sv["cg"], sv["up"], dact, "ffn_conv_bwd_act")
        (dgate, small["ffn_dw_w"][l]), (landed["ffn_w_down"][l],) = ffn_conv_bwd_taps(
            sv["up"], dcg, ffn_w9_l[l], "ffn_conv_bwd_taps", Comm("exchange", [g_down]))
        dup = [dgate, dval]
        if g_up_prev is not None:
            (dh2,), (landed["ffn_w_up"][l + 1],) = mm_nt(dup, w_up_l[l], F32, "ffn_up_dx_x", Comm("exchange", [g_up_prev]))
        else:
            (dh2,) = mm_nt(dup, w_up_l[l], F32, "ffn_up_dx")
        g_up_prev = mm_tn(sv["h2"], dup, N_DEV, BF16, "ffn_up_dw")
        dx2, small["norm2_g"][l], dsh2, dsc2 = rms_mod_bwd(sv["x2"], norm2_g[l][None], sc2, dh2, dxs, "norm2_bwd")
        dy1, dg1 = gate_bwd(dx2, sv["y1"], g1, "res1_bwd")
        (dmix,) = mm_nt(dy1, w_out_l[l], BF16, "out_proj_dx")
        g_out = mm_tn(sv["mix"], dy1, 1, BF16, "out_proj_dw").reshape(N_DEV, wo_k, D)
        do, dgt = ret_out_bwd(sv["o_f"], sv["o_b"], sv["p"], dmix, H, "ret_out_bwd")
        dqf, dkf, dvf, dlg_f = ret_bwd(sv["p"], cos, sin, sv["tf"], do, sv["s_f"], H, n_ctx, False, "ret_bwd_f")
        (dqb, dkb, dvb, dlg_b), (landed["w_out"][l],) = ret_bwd(
            sv["p"], cos, sin, sv["tb"], do, sv["s_b"], H, n_ctx, True, "ret_bwd_b", Comm("exchange", [g_out]))
        dq, dk, dv = ret_qkv_grad(dqf, dqb, dkf, dkb, dvf, dvb, cos, sin, H, "ret_qkv_grad")
        small["ret_decay_f"][l] = jnp.sum(dlg_f[:, 0, :], axis=-1) * jax.nn.sigmoid(-ret_decay_f[l])
        small["ret_decay_b"][l] = jnp.sum(dlg_b[:, 0, :], axis=-1) * jax.nn.sigmoid(-ret_decay_b[l])
        du2, small["conv_ln_g"][l], small["conv_ln_b"][l], small["conv_dw_b"][l] = conv_bwd_ln(
            sv["u2"], dmix, conv_ln_g[l][None], conv_ln_b[l][None], "conv_bwd_ln")
        da, dbg, small["conv_dw_w"][l] = conv_bwd_taps(sv["p"], du2, conv_w_l[l], "conv_bwd_taps")
        dp = jnp.concatenate([dq, dk, dv, dgt, da, dbg], axis=1)
        if l == 0:
            (dh,), (landed["ffn_w_up"][0],) = mm_nt(dp, w_in_l[l], F32, "in_proj_dx_x", Comm("exchange", [g_up_prev]))
        else:
            (dh,) = mm_nt(dp, w_in_l[l], F32, "in_proj_dx")
        g_in_prev = mm_tn(sv["h"], dp, N_DEV, BF16, "in_proj_dw")
        dxs, small["norm1_g"][l], dsh1, dsc1 = rms_mod_bwd(sv["x1"], norm1_g[l][None], sc1, dh, dx2, "norm1_bwd")
        dmod2[l] = jnp.concatenate([dsh1, dsc1, dg1, dsh2, dsc2, dg2], axis=1)
    (landed["w_in"][0],) = run_comm("exchange", [g_in_prev], "exchange_last")

    grad_x = dxs[CTX:][None]

    dmod2 = jnp.stack(dmod2)
    (g_dmod,) = run_comm("gather", [dmod2], "gather_dmod")
    dmod_all = jnp.concatenate([jnp.moveaxis(g_dmod[:, :, 1], 0, 1), jnp.moveaxis(g_dmod[:, :, 0], 0, 1)], axis=1)
    dmod_sh = lax.dynamic_slice_in_dim(dmod_all, me * wm_n, wm_n, axis=2)
    dmod_sh = jnp.moveaxis(dmod_sh, 0, 1).reshape(2 * N_DEV, L * wm_n).astype(BF16)
    g_w_mod = mm_tn(s_cond_b, dmod_sh, L, F32, "mod_dw")
    (d_cond,) = mm_nt(dmod_sh, w_mod_b, F32, "mod_dx")
    g_c_ctx_part = jnp.sum(d_cond[N_DEV:], axis=0) * _dsilu(c_ctx)
    g_b_mod_part = dmod2[:, 0] + dmod2[:, 1]

    pad128 = lambda a: jnp.pad(a.reshape(-1), (0, (-a.size) % 128))
    rep_names = ["c_ctx", "b_mod", "norm1_g", "norm2_g", "ret_decay_f", "ret_decay_b", "conv_dw_b", "conv_ln_g",
                 "conv_ln_b", "ffn_dw_b", "final_norm_g"]
    given = dict(c_ctx=(c_ctx, m_c_ctx, v_c_ctx), b_mod=(b_mod, m_b_mod, v_b_mod),
                 norm1_g=(norm1_g, m_norm1_g, v_norm1_g), norm2_g=(norm2_g, m_norm2_g, v_norm2_g),
                 ret_decay_f=(ret_decay_f, m_ret_decay_f, v_ret_decay_f),
                 ret_decay_b=(ret_decay_b, m_ret_decay_b, v_ret_decay_b),
                 conv_dw_b=(conv_dw_b, m_conv_dw_b, v_conv_dw_b), conv_ln_g=(conv_ln_g, m_conv_ln_g, v_conv_ln_g),
                 conv_ln_b=(conv_ln_b, m_conv_ln_b, v_conv_ln_b), ffn_dw_b=(ffn_dw_b, m_ffn_dw_b, v_ffn_dw_b),
                 final_norm_g=(final_norm_g, m_final_norm_g, v_final_norm_g))
    rep_part = dict(c_ctx=g_c_ctx_part, b_mod=g_b_mod_part, final_norm_g=d_final_g)
    for nme in rep_names:
        if nme not in rep_part:
            rep_part[nme] = jnp.stack([a.reshape(-1) for a in small[nme]])
    rep_sizes = [((-given[nme][0].size) % 128) + given[nme][0].size for nme in rep_names]
    n_rep = sum(rep_sizes)
    cw_part = jnp.stack(small["conv_dw_w"])
    fw_part = jnp.stack(small["ffn_dw_w"])
    packed = jnp.concatenate([pad128(rep_part[nme]) for nme in rep_names] + [cw_part.reshape(-1), fw_part.reshape(-1)])
    (g_small,) = run_comm("gather", [packed.reshape(-1, 128)], "gather_small")
    g_small = g_small.reshape(N_DEV, -1)
    rep_w, rep_m, rep_v = (jnp.concatenate([pad128(given[nme][k]) for nme in rep_names]).reshape(-1, 128) for k in range(3))
    rep_out = adamw(g_small[:, :n_rep].reshape(N_DEV, -1, 128), rep_w, rep_m, rep_v, "adamw_small")
    res = {}
    off = 0
    for nme, sz in zip(rep_names, rep_sizes):
        shape = given[nme][0].shape
        res[nme] = [o.reshape(-1)[off:off + given[nme][0].size].reshape(shape) for o in rep_out]
        off += sz

    cw_all = g_small[:, n_rep:n_rep + cw_part.size].reshape(N_DEV, L * CONV_K, CW)
    cw_mine = lax.dynamic_slice_in_dim(cw_all, me * cw_n, cw_n, axis=2)
    res["conv_dw_w"] = [o.reshape(conv_dw_w.shape) for o in adamw(
        cw_mine, conv_dw_w.reshape(L * CONV_K, cw_n), m_conv_dw_w.reshape(L * CONV_K, cw_n),
        v_conv_dw_w.reshape(L * CONV_K, cw_n), "adamw_conv_w")]
    fw_all = g_small[:, n_rep + cw_part.size:].reshape(N_DEV, L * 9, DFF)
    fw_mine = lax.dynamic_slice_in_dim(fw_all, me * fw_n, fw_n, axis=2)
    res["ffn_dw_w"] = [o.reshape(ffn_dw_w.shape) for o in adamw(
        fw_mine, ffn_dw_w.reshape(L * 9, fw_n), m_ffn_dw_w.reshape(L * 9, fw_n),
        v_ffn_dw_w.reshape(L * 9, fw_n), "adamw_ffn_w")]

    res["w_mod"] = [o.reshape(w_mod.shape) for o in adamw(
        g_w_mod.reshape(1, L * D, wm_n), w_mod.reshape(L * D, wm_n), m_w_mod.reshape(L * D, wm_n),
        v_w_mod.reshape(L * D, wm_n), "adamw_w_mod")]

    res["w_in"] = adamw_layers(landed["w_in"], w_in, m_w_in, v_w_in, "adamw_w_in")
    res["w_out"] = adamw_layers(landed["w_out"], w_out, m_w_out, v_w_out, "adamw_w_out")
    res["ffn_w_up"] = adamw_layers(landed["ffn_w_up"], ffn_w_up, m_ffn_w_up, v_ffn_w_up, "adamw_ffn_w_up")
    res["ffn_w_down"] = adamw_layers(landed["ffn_w_down"], ffn_w_down, m_ffn_w_down, v_ffn_w_down, "adamw_ffn_w_down")

    order = ["c_ctx", "w_mod", "b_mod", "norm1_g", "norm2_g", "w_in", "ret_decay_f", "ret_decay_b", "conv_dw_w",
             "conv_dw_b", "conv_ln_g", "conv_ln_b", "w_out", "ffn_w_up", "ffn_dw_w", "ffn_dw_b", "ffn_w_down",
             "final_norm_g"]
    return (loss, grad_x, *[res[nme][0] for nme in order], *[res[nme][1] for nme in order],
            *[res[nme][2] for nme in order], *[res[nme][3] for nme in order])
```

```python
import functools

import jax
import jax.numpy as jnp
from jax import lax
from jax.experimental import pallas as pl
from jax.experimental.pallas import tpu as pltpu

F32 = jnp.float32
BF16 = jnp.bfloat16
EPS = 1e-6
N_DEV = 8
ROW_TILE = 256
RET_CHUNK = 128
GRID_W = 64
CONV_K = 31
CONV_HALO = 16
CONV_ROWS = 32
FFN_COL_TILES = (1408, 512, 256, 128)
FFN_TAP_ROWS = 64
FFN_HALO = 128
ROPE_THETA = 10000.0
ADAM_LR = 0.001
ADAM_B1 = 0.9
ADAM_B2 = 0.999
ADAM_EPS = 1e-08
ADAM_WD = 0.01
ADAM_STEP = 10
VMEM_LIMIT = 56 * 1024 * 1024
ADAM_VMEM_BYTES = 24 * 1024 * 1024
MESH = pl.DeviceIdType.MESH
ANY = pl.BlockSpec(memory_space=pl.ANY)


def _pick(n, cands):
    for t in cands:
        if n % t == 0:
            return t
    return n


def _sigmoid(z):
    return 1.0 / (1.0 + jnp.exp(-z))


def _my_rank():
    return 4 * lax.axis_index("x") + 2 * lax.axis_index("y") + lax.axis_index("c")


def _peer(j):
    x, y, c = lax.axis_index("x"), lax.axis_index("y"), lax.axis_index("c")
    px = 1 - x if j & 4 else x
    py = 1 - y if j & 2 else y
    pc = 1 - c if j & 1 else c
    return (px, py, pc), 4 * px + 2 * py + pc


class Comm:
    def __init__(self, kind, arrs):
        assert kind in ("gather", "exchange")
        self.kind, self.arrs, self.n = kind, list(arrs), len(arrs)
        self.in_specs = [ANY] * self.n
        self.out_specs = [ANY] * self.n
        lead = (N_DEV,) if kind == "gather" else ()
        self.out_shape = [jax.ShapeDtypeStruct(lead + a.shape, a.dtype) for a in self.arrs]
        per = self.n * (N_DEV - 1)
        self.scratch = [pltpu.SemaphoreType.DMA((per,)), pltpu.SemaphoreType.DMA((per,)),
                        pltpu.SemaphoreType.DMA((self.n,))]

    def _src(self, ref, rank):
        return ref if self.kind == "gather" else ref.at[rank]

    def _local(self, ins, outs, sems, a):
        me = _my_rank()
        return pltpu.make_async_copy(self._src(ins[a], me), outs[a].at[me], sems[2].at[a])

    def _remote(self, ins, outs, sems, a, j, receive):
        dev, rank = _peer(j)
        s = a * (N_DEV - 1) + j - 1
        slot = rank if receive else _my_rank()
        return pltpu.make_async_remote_copy(src_ref=self._src(ins[a], rank), dst_ref=outs[a].at[slot],
                                            send_sem=sems[0].at[s], recv_sem=sems[1].at[s],
                                            device_id=dev, device_id_type=MESH)

    def _gather_copy(self, ins, outs, sems, a, idx, src_slot, to):
        me = _my_rank()
        slot = me if src_slot is None else src_slot
        src = ins[a] if src_slot is None else outs[a].at[src_slot]
        s = a * (N_DEV - 1) + idx
        return pltpu.make_async_remote_copy(src_ref=src, dst_ref=outs[a].at[slot], send_sem=sems[0].at[s],
                                            recv_sem=sems[1].at[s], device_id=to, device_id_type=MESH)

    def _gather_plan(self):
        x, y, c = lax.axis_index("x"), lax.axis_index("y"), lax.axis_index("c")
        rank = lambda px, py, pc: 4 * px + 2 * py + pc
        chips = [(1 - x, y), (x, 1 - y), (1 - x, 1 - y)]
        return (x, y, 1 - c), rank(x, y, 1 - c), [((px, py, c), rank(px, py, c), rank(px, py, 1 - c)) for px, py in chips]

    def start(self, ins, outs, sems):
        for a in range(self.n):
            self._local(ins, outs, sems, a).start()
        if self.kind == "gather":
            sibling, _, chips = self._gather_plan()
            for a in range(self.n):
                self._gather_copy(ins, outs, sems, a, 0, None, sibling).start()
                for k, (dev, _, _) in enumerate(chips):
                    self._gather_copy(ins, outs, sems, a, 1 + k, None, dev).start()
            return
        for a in range(self.n):
            for j in range(1, N_DEV):
                self._remote(ins, outs, sems, a, j, False).start()

    def wait(self, ins, outs, sems):
        if self.kind == "gather":
            sibling, sib_rank, chips = self._gather_plan()
            for a in range(self.n):
                for k, (dev, slot, _) in enumerate(chips):
                    self._gather_copy(ins, outs, sems, a, 1 + k, slot, dev).wait_recv()
                    self._gather_copy(ins, outs, sems, a, 4 + k, slot, sibling).start()
            for a in range(self.n):
                self._gather_copy(ins, outs, sems, a, 0, sib_rank, sibling).wait_recv()
                for k, (_, _, sib_slot) in enumerate(chips):
                    self._gather_copy(ins, outs, sems, a, 4 + k, sib_slot, sibling).wait_recv()
                self._gather_copy(ins, outs, sems, a, 0, None, sibling).wait_send()
                for k, (dev, slot, _) in enumerate(chips):
                    self._gather_copy(ins, outs, sems, a, 1 + k, None, dev).wait_send()
                    self._gather_copy(ins, outs, sems, a, 4 + k, slot, sibling).wait_send()
        else:
            for a in range(self.n):
                for j in range(1, N_DEV):
                    cp = self._remote(ins, outs, sems, a, j, True)
                    cp.wait_recv()
                    cp.wait_send()
        for a in range(self.n):
            self._local(ins, outs, sems, a).wait()


def _call(compute, *, name, grid, in_specs, out_specs, out_shape, operands, sem, scratch=(), comm=None):
    n_in, n_out, n_sc = len(in_specs), len(out_specs), len(scratch)
    k = comm.n if comm else 0

    def body(*refs):
        ins, cin = refs[:n_in], refs[n_in:n_in + k]
        o0 = n_in + k
        outs, cout = refs[o0:o0 + n_out], refs[o0 + n_out:o0 + n_out + k]
        s0 = o0 + n_out + k
        sc, sems = refs[s0:s0 + n_sc], refs[s0 + n_sc:]
        if comm:
            ids = [pl.program_id(d) for d in range(len(grid))]
            first = functools.reduce(jnp.logical_and, [i == 0 for i in ids])
            last = functools.reduce(jnp.logical_and, [i == g - 1 for i, g in zip(ids, grid)])

            @pl.when(first)
            def _():
                comm.start(cin, cout, sems)

        compute(*ins, *outs, *sc)

        if comm:
            @pl.when(last)
            def _():
                comm.wait(cin, cout, sems)

    semantics = ("arbitrary",) * len(grid) if comm else sem
    res = pl.pallas_call(
        body, name=name, grid=grid,
        in_specs=list(in_specs) + (comm.in_specs if comm else []),
        out_specs=list(out_specs) + (comm.out_specs if comm else []),
        out_shape=list(out_shape) + (comm.out_shape if comm else []),
        scratch_shapes=list(scratch) + (comm.scratch if comm else []),
        compiler_params=pltpu.CompilerParams(dimension_semantics=semantics, vmem_limit_bytes=VMEM_LIMIT),
    )(*operands, *(comm.arrs if comm else []))
    if comm:
        return list(res[:n_out]), list(res[n_out:])
    return list(res)


def run_comm(kind, arrs, name):
    comm = Comm(kind, arrs)

    def body(*refs):
        ins, outs, sems = refs[:comm.n], refs[comm.n:2 * comm.n], refs[2 * comm.n:]
        comm.start(ins, outs, sems)
        comm.wait(ins, outs, sems)

    return list(pl.pallas_call(body, name=name, in_specs=comm.in_specs, out_specs=comm.out_specs,
                               out_shape=comm.out_shape, scratch_shapes=comm.scratch)(*comm.arrs))


M_TILES = (768, 512, 256, 128)
WIDE_TILES = (2048, 1408, 1024, 768, 512, 256, 128)
MID_TILES = (1408, 1024, 768, 512, 256, 128)


def _mm_body(dot, n_steps, axis):
    if n_steps == 1:
        def compute(a_ref, b_ref, o_ref):
            o_ref[...] = dot(a_ref, b_ref).astype(o_ref.dtype).reshape(o_ref.shape)
        return compute, []

    def compute(a_ref, b_ref, o_ref, acc):
        k = pl.program_id(axis)

        @pl.when(k == 0)
        def _():
            acc[...] = jnp.zeros_like(acc)

        acc[...] += dot(a_ref, b_ref)

        @pl.when(k == n_steps - 1)
        def _():
            o_ref[...] = acc[...].astype(o_ref.dtype).reshape(o_ref.shape)

    return compute, None


def mm_nn(a, b3, out_dtype, name, comm=None):
    M, K = a.shape
    R, _, n = b3.shape
    tm, tk, tn = _pick(M, M_TILES), _pick(K, WIDE_TILES), _pick(n, MID_TILES)
    nb, nk = n // tn, K // tk
    compute, scratch = _mm_body(lambda a_ref, b_ref: jnp.dot(a_ref[...], b_ref[0], preferred_element_type=F32), nk, 2)
    return _call(
        compute, name=name, grid=(M // tm, R * nb, nk),
        in_specs=[pl.BlockSpec((tm, tk), lambda i, j, k: (i, k)),
                  pl.BlockSpec((1, tk, tn), lambda i, j, k: (j // nb, k, j % nb))],
        out_specs=[pl.BlockSpec((tm, tn), lambda i, j, k: (i, j))],
        out_shape=[jax.ShapeDtypeStruct((M, R * n), out_dtype)],
        scratch=scratch if scratch is not None else [pltpu.VMEM((tm, tn), F32)],
        operands=(a, b3), sem=("parallel", "parallel", "arbitrary"), comm=comm)


def _piece_specs(n_pieces, per, rows, t, row_of, col_of):
    if n_pieces == 1:
        return [pl.BlockSpec((rows, t), lambda *ids: (row_of(*ids), col_of(*ids)))]

    def index(q, *ids):
        local = col_of(*ids) - q * per
        inside = jnp.logical_and(local >= 0, local < per)
        return jnp.where(inside, row_of(*ids), 0), jnp.where(inside, local, 0)

    return [pl.BlockSpec((rows, t), functools.partial(index, q)) for q in range(n_pieces)]


def _for_piece(n_pieces, per, block, fn):
    if n_pieces == 1:
        fn(0)
        return
    for q in range(n_pieces):
        pl.when(block // per == q)(functools.partial(fn, q))


def mm_nt(a, b3, out_dtype, name, comm=None):
    pieces = list(a) if isinstance(a, (list, tuple)) else [a]
    P = len(pieces)
    M, W = pieces[0].shape
    R, K, n = b3.shape
    assert P * W == R * n
    tm, tko, tc = _pick(M, M_TILES), _pick(K, WIDE_TILES), _pick(n, WIDE_TILES)
    ncb = n // tc
    nc = R * ncb
    per = W // tc

    def compute(*refs):
        a_refs, b_ref, o_ref, acc = refs[:P], refs[P], refs[P + 1], refs[P + 2]
        k = pl.program_id(2)

        @pl.when(k == 0)
        def _():
            acc[...] = jnp.zeros_like(acc)

        def add(q):
            acc[...] += lax.dot_general(a_refs[q][...], b_ref[0], (((1,), (1,)), ((), ())),
                                        preferred_element_type=F32)

        _for_piece(P, per, k, add)

        @pl.when(k == nc - 1)
        def _():
            o_ref[...] = acc[...].astype(o_ref.dtype)

    return _call(
        compute, name=name, grid=(M // tm, K // tko, nc),
        in_specs=_piece_specs(P, per, tm, tc, lambda i, j, k: i, lambda i, j, k: k)
        + [pl.BlockSpec((1, tko, tc), lambda i, j, k: (k // ncb, j, k % ncb))],
        out_specs=[pl.BlockSpec((tm, tko), lambda i, j, k: (i, j))],
        out_shape=[jax.ShapeDtypeStruct((M, K), out_dtype)],
        scratch=[pltpu.VMEM((tm, tko), F32)],
        operands=(*pieces, b3), sem=("parallel", "parallel", "arbitrary"), comm=comm)


def mm_tn(a, b, R, out_dtype, name):
    pieces = list(b) if isinstance(b, (list, tuple)) else [b]
    P = len(pieces)
    M, K = a.shape
    W = pieces[0].shape[1]
    n = P * W // R
    tm, tk, tn = _pick(M, (1408,) + M_TILES), _pick(K, MID_TILES), _pick(n, MID_TILES)
    nb, nm = n // tn, M // tm
    per = W // tn

    def compute(*refs):
        a_ref, b_refs, o_ref, acc = refs[0], refs[1:1 + P], refs[1 + P], refs[2 + P]
        m = pl.program_id(2)

        @pl.when(m == 0)
        def _():
            acc[...] = jnp.zeros_like(acc)

        def add(q):
            acc[...] += lax.dot_general(a_ref[...], b_refs[q][...], (((0,), (0,)), ((), ())),
                                        preferred_element_type=F32)

        _for_piece(P, per, pl.program_id(1), add)

        @pl.when(m == nm - 1)
        def _():
            o_ref[0] = acc[...].astype(o_ref.dtype)

    return _call(
        compute, name=name, grid=(K // tk, R * nb, nm),
        in_specs=[pl.BlockSpec((tm, tk), lambda i, j, m: (m, i))]
        + _piece_specs(P, per, tm, tn, lambda i, j, m: m, lambda i, j, m: j),
        out_specs=[pl.BlockSpec((1, tk, tn), lambda i, j, m: (j // nb, i, j % nb))],
        out_shape=[jax.ShapeDtypeStruct((R, K, n), out_dtype)],
        scratch=[pltpu.VMEM((tk, tn), F32)],
        operands=(a, *pieces), sem=("parallel", "parallel", "arbitrary"))[0]


def _seg_spec(D):
    return pl.BlockSpec((None, 1, D), lambda i: (jnp.minimum(i, 1), 0, 0))


def _seg3(a):
    return a.reshape(2, 1, a.shape[-1])


def _row_spec(w, col=0):
    return pl.BlockSpec((ROW_TILE, w), lambda i: (i, col))


def _acc_spec(r, w):
    return pl.BlockSpec((r, w), lambda i: (0, 0))


def _seg_accumulate(ref, i, val):
    ref[0:1, :] += jnp.where(i == 0, val, 0.0)
    ref[1:2, :] += jnp.where(i == 0, 0.0, val)


def rms_mod_fwd(x, g, shift2, scale2, name):
    T, D = x.shape

    def compute(x_ref, g_ref, sh_ref, sc_ref, h_ref):
        xv = x_ref[...]
        rstd = lax.rsqrt(jnp.mean(xv * xv, axis=-1, keepdims=True) + EPS)
        h = (xv * rstd * g_ref[...]) * (1.0 + sc_ref[...]) + sh_ref[...]
        h_ref[...] = h.astype(h_ref.dtype)

    return _call(compute, name=name, grid=(T // ROW_TILE,),
                 in_specs=[_row_spec(D), _acc_spec(1, D), _seg_spec(D), _seg_spec(D)],
                 out_specs=[_row_spec(D)], out_shape=[jax.ShapeDtypeStruct((T, D), BF16)],
                 operands=(x, g, _seg3(shift2), _seg3(scale2)), sem=("parallel",))[0]


def rms_mod_bwd(x, g, scale2, dh, dres, name):
    T, D = x.shape

    def compute(x_ref, g_ref, sc_ref, dh_ref, dres_ref, dx_ref, dg_ref, dsh_ref, dsc_ref):
        i = pl.program_id(0)

        @pl.when(i == 0)
        def _():
            dg_ref[...] = jnp.zeros_like(dg_ref)
            dsh_ref[...] = jnp.zeros_like(dsh_ref)
            dsc_ref[...] = jnp.zeros_like(dsc_ref)

        xv = x_ref[...]
        dh = dh_ref[...].astype(F32)
        gv = g_ref[...]
        rstd = lax.rsqrt(jnp.mean(xv * xv, axis=-1, keepdims=True) + EPS)
        xh = xv * rstd
        u = dh * (1.0 + sc_ref[...])
        dg_ref[...] += jnp.sum(u * xh, axis=0, keepdims=True)
        _seg_accumulate(dsh_ref, i, jnp.sum(dh, axis=0, keepdims=True))
        _seg_accumulate(dsc_ref, i, jnp.sum(dh * xh * gv, axis=0, keepdims=True))
        dxh = u * gv
        dx = rstd * (dxh - xh * jnp.mean(dxh * xh, axis=-1, keepdims=True))
        dx_ref[...] = dres_ref[...] + dx

    return _call(compute, name=name, grid=(T // ROW_TILE,),
                 in_specs=[_row_spec(D), _acc_spec(1, D), _seg_spec(D), _row_spec(D), _row_spec(D)],
                 out_specs=[_row_spec(D), _acc_spec(1, D), _acc_spec(2, D), _acc_spec(2, D)],
                 out_shape=[jax.ShapeDtypeStruct((T, D), F32), jax.ShapeDtypeStruct((1, D), F32),
                            jax.ShapeDtypeStruct((2, D), F32), jax.ShapeDtypeStruct((2, D), F32)],
                 operands=(x, g, _seg3(scale2), dh, dres), sem=("arbitrary",))


def gate_res_fwd(x, y, gate2, name):
    T, D = x.shape

    def compute(x_ref, y_ref, g_ref, o_ref):
        o_ref[...] = x_ref[...] + g_ref[...] * y_ref[...]

    return _call(compute, name=name, grid=(T // ROW_TILE,),
                 in_specs=[_row_spec(D), _row_spec(D), _seg_spec(D)],
                 out_specs=[_row_spec(D)], out_shape=[jax.ShapeDtypeStruct((T, D), F32)],
                 operands=(x, y, _seg3(gate2)), sem=("parallel",))[0]


def gate_bwd(dxo, y, gate2, name):
    T, D = dxo.shape

    def compute(d_ref, y_ref, g_ref, dy_ref, dg_ref):
        i = pl.program_id(0)

        @pl.when(i == 0)
        def _():
            dg_ref[...] = jnp.zeros_like(dg_ref)

        d = d_ref[...]
        dy_ref[...] = (d * g_ref[...]).astype(dy_ref.dtype)
        _seg_accumulate(dg_ref, i, jnp.sum(d * y_ref[...], axis=0, keepdims=True))

    return _call(compute, name=name, grid=(T // ROW_TILE,),
                 in_specs=[_row_spec(D), _row_spec(D), _seg_spec(D)],
                 out_specs=[_row_spec(D), _acc_spec(2, D)],
                 out_shape=[jax.ShapeDtypeStruct((T, D), BF16), jax.ShapeDtypeStruct((2, D), F32)],
                 operands=(dxo, y, _seg3(gate2)), sem=("arbitrary",))


def final_loss(x, g, target, name):
    T, D = x.shape

    def compute(x_ref, g_ref, t_ref, dx_ref, dg_ref, loss_ref):
        i = pl.program_id(0)

        @pl.when(i == 0)
        def _():
            dg_ref[...] = jnp.zeros_like(dg_ref)
            loss_ref[...] = jnp.zeros_like(loss_ref)
            dx_ref[...] = jnp.zeros_like(dx_ref)

        @pl.when(i > 0)
        def _():
            xv = x_ref[...]
            gv = g_ref[...]
            rstd = lax.rsqrt(jnp.mean(xv * xv, axis=-1, keepdims=True) + EPS)
            xh = xv * rstd
            err = xh * gv - t_ref[...]
            loss_ref[...] += 0.5 * jnp.sum(jnp.mean(err * err, axis=-1, keepdims=True))
            dy = err * (1.0 / D)
            dg_ref[...] += jnp.sum(dy * xh, axis=0, keepdims=True)
            dxh = dy * gv
            dx_ref[...] = rstd * (dxh - xh * jnp.mean(dxh * xh, axis=-1, keepdims=True))

    return _call(compute, name=name, grid=(T // ROW_TILE,),
                 in_specs=[_row_spec(D), _acc_spec(1, D),
                           pl.BlockSpec((ROW_TILE, D), lambda i: (jnp.maximum(i - 1, 0), 0))],
                 out_specs=[_row_spec(D), _acc_spec(1, D), _acc_spec(8, 128)],
                 out_shape=[jax.ShapeDtypeStruct((T, D), F32), jax.ShapeDtypeStruct((1, D), F32),
                            jax.ShapeDtypeStruct((8, 128), F32)],
                 operands=(x, g, target), sem=("arbitrary",))


def _swap32(v):
    lane = lax.broadcasted_iota(jnp.int32, v.shape, 1)
    return jnp.where((lane & 63) < 32, pltpu.roll(v, 96, 1), pltpu.roll(v, 32, 1))


def _rope(v, cos, sin):
    return v * cos + _swap32(v) * sin


def _rope_t(d, cos, sin):
    return d * cos + _swap32(d * sin)


def _chunk_of(step, n_chunks, n_ctx, rev):
    if not rev:
        return step
    return jnp.where(step < n_ctx, n_ctx - 1 - step, n_chunks + n_ctx - 1 - step)


def _dot_t0(a, b):
    return lax.dot_general(a, b, (((0,), (0,)), ((), ())), preferred_element_type=F32)


def _dot_t1(a, b):
    return lax.dot_general(a, b, (((1,), (1,)), ((), ())), preferred_element_type=F32)


def _dot(a, b):
    return jnp.dot(a, b, preferred_element_type=F32)


def ret_fwd(p, cos, sin, tabs, n_heads, n_ctx, rev, name, comm=None):
    T = p.shape[0]
    C = RET_CHUNK
    H = n_heads
    NC = T // C
    scale = C ** -0.5

    def compute(q_ref, k_ref, v_ref, cos_ref, sin_ref, dm_ref, qd_ref, kd_ref, cd_ref, o_ref, s_ref, S):
        t = pl.program_id(0)

        @pl.when(t == 0)
        def _():
            S[...] = jnp.zeros_like(S)

        cs, sn = cos_ref[...], sin_ref[...]
        for h in range(H):
            sl = slice(h * C, (h + 1) * C)
            q = _rope(q_ref[:, sl], cs, sn)
            k = _rope(k_ref[:, sl], cs, sn) * scale
            qb, kb, vb = q.astype(BF16), k.astype(BF16), v_ref[:, sl].astype(BF16)
            A = _dot_t1(qb, kb) * dm_ref[h]
            s_in = S[h]
            s_ref[h, 0] = s_in
            o_ref[:, sl] = _dot(A.astype(BF16), vb) + _dot(qb, s_in.astype(BF16)) * qd_ref[h]
            S[h] = s_in * cd_ref[h] + _dot_t0((k * kd_ref[h]).astype(BF16), vb)

    cmap = lambda t: _chunk_of(t, NC, n_ctx, rev)
    blk = lambda col: pl.BlockSpec((C, H * C), lambda t: (cmap(t), col))
    tab = pl.BlockSpec((C, C), lambda t: (cmap(t), 0))
    htab = lambda r: pl.BlockSpec((H, r, C), lambda t: (0, 0, 0))
    return _call(
        compute, name=name, grid=(NC,),
        in_specs=[blk(0), blk(1), blk(2), tab, tab, htab(C), htab(C), htab(C), htab(1)],
        out_specs=[blk(0), pl.BlockSpec((H, 1, C, C), lambda t: (0, cmap(t), 0, 0))],
        out_shape=[jax.ShapeDtypeStruct((T, H * C), F32), jax.ShapeDtypeStruct((H, NC, C, C), F32)],
        scratch=[pltpu.VMEM((H, C, C), F32)],
        operands=(p, p, p, cos, sin, tabs["dm"], tabs["qd"], tabs["kd"], tabs["cd"]),
        sem=("arbitrary",), comm=comm)


def ret_bwd(p, cos, sin, tabs, do, s_saved, n_heads, n_ctx, rev, name, comm=None):
    T = p.shape[0]
    C = RET_CHUNK
    H = n_heads
    NC = T // C
    scale = C ** -0.5

    def compute(q_ref, k_ref, v_ref, cos_ref, sin_ref, dm_ref, qd_ref, kd_ref, cd_ref, em_ref, eq_ref,
                ek_ref, do_ref, s_ref, dq_ref, dk_ref, dv_ref, dlg_ref, dS):
        t = pl.program_id(0)

        @pl.when(t == 0)
        def _():
            dS[...] = jnp.zeros_like(dS)
            dlg_ref[...] = jnp.zeros_like(dlg_ref)

        cs, sn = cos_ref[...], sin_ref[...]
        for h in range(H):
            sl = slice(h * C, (h + 1) * C)
            q = _rope(q_ref[:, sl], cs, sn)
            k = _rope(k_ref[:, sl], cs, sn) * scale
            qb, kb, vb = q.astype(BF16), k.astype(BF16), v_ref[:, sl].astype(BF16)
            dmv, qdv, kdv, cdv = dm_ref[h], qd_ref[h], kd_ref[h], cd_ref[h]
            A = _dot_t1(qb, kb) * dmv
            s_in = s_ref[h, 0]
            sb = s_in.astype(BF16)
            ds_out = dS[h]
            dsb = ds_out.astype(BF16)
            dov = do_ref[:, sl]
            dob = dov.astype(BF16)
            dA = _dot_t1(dob, vb)
            dPb = (dA * dmv).astype(BF16)
            doq = dov * qdv
            doqb = doq.astype(BF16)
            kk = k * kdv
            vds = _dot_t1(vb, dsb)
            dq_ref[:, sl] = _dot(dPb, kb) + _dot_t1(doqb, sb)
            dk_ref[:, sl] = _dot_t0(dPb, qb) + vds * kdv
            dv_ref[:, sl] = _dot_t0(A.astype(BF16), dob) + _dot(kk.astype(BF16), dsb)
            dS[h] = ds_out * cdv + _dot_t0(qb, doqb)
            o2 = _dot(qb, sb)
            part = (jnp.sum(dA * A * em_ref[...], axis=0, keepdims=True)
                    + jnp.sum(eq_ref[...] * doq * o2, axis=0, keepdims=True)
                    + jnp.sum(ek_ref[...] * kk * vds, axis=0, keepdims=True)
                    + float(C) * cdv * jnp.sum(s_in * ds_out, axis=0, keepdims=True))
            dlg_ref[h, 0:1, :] += part

    cmap = lambda t: _chunk_of(NC - 1 - t, NC, n_ctx, rev)
    blk = lambda col: pl.BlockSpec((C, H * C), lambda t: (cmap(t), col))
    tab = pl.BlockSpec((C, C), lambda t: (cmap(t), 0))
    const = pl.BlockSpec((C, C), lambda t: (0, 0))
    htab = lambda r: pl.BlockSpec((H, r, C), lambda t: (0, 0, 0))
    return _call(
        compute, name=name, grid=(NC,),
        in_specs=[blk(0), blk(1), blk(2), tab, tab, htab(C), htab(C), htab(C), htab(1), const, const,
                  const, blk(0), pl.BlockSpec((H, 1, C, C), lambda t: (0, cmap(t), 0, 0))],
        out_specs=[blk(0), blk(0), blk(0), pl.BlockSpec((H, 8, C), lambda t: (0, 0, 0))],
        out_shape=[jax.ShapeDtypeStruct((T, H * C), F32)] * 3 + [jax.ShapeDtypeStruct((H, 8, C), F32)],
        scratch=[pltpu.VMEM((H, C, C), F32)],
        operands=(p, p, p, cos, sin, tabs["dm"], tabs["qd"], tabs["kd"], tabs["cd"], tabs["em"], tabs["eq"],
                  tabs["ek"], do, s_saved),
        sem=("arbitrary",), comm=comm)


def ret_out_fwd(o_f, o_b, p, n_heads, name):
    T, RW = o_f.shape
    C = RET_CHUNK

    def compute(of_ref, ob_ref, g_ref, out_ref):
        for h in range(n_heads):
            sl = slice(h * C, (h + 1) * C)
            o = of_ref[:, sl] + ob_ref[:, sl]
            r = o * lax.rsqrt(jnp.mean(o * o, axis=-1, keepdims=True) + EPS)
            g = g_ref[:, sl]
            out_ref[:, sl] = (g * _sigmoid(g) * r).astype(out_ref.dtype)

    return _call(compute, name=name, grid=(T // ROW_TILE,),
                 in_specs=[_row_spec(RW), _row_spec(RW), _row_spec(RW, 3)],
                 out_specs=[_row_spec(RW)], out_shape=[jax.ShapeDtypeStruct((T, RW), BF16)],
                 operands=(o_f, o_b, p), sem=("parallel",))[0]


def ret_out_bwd(o_f, o_b, p, dmix, n_heads, name):
    T, RW = o_f.shape
    C = RET_CHUNK

    def compute(of_ref, ob_ref, g_ref, d_ref, do_ref, dg_ref):
        for h in range(n_heads):
            sl = slice(h * C, (h + 1) * C)
            o = of_ref[:, sl] + ob_ref[:, sl]
            rstd = lax.rsqrt(jnp.mean(o * o, axis=-1, keepdims=True) + EPS)
            r = o * rstd
            g = g_ref[:, sl]
            sg = _sigmoid(g)
            d = d_ref[:, sl].astype(F32)
            dg_ref[:, sl] = (d * r * sg * (1.0 + g * (1.0 - sg))).astype(dg_ref.dtype)
            dr = d * g * sg
            do_ref[:, sl] = rstd * (dr - r * jnp.mean(dr * r, axis=-1, keepdims=True))

    return _call(compute, name=name, grid=(T // ROW_TILE,),
                 in_specs=[_row_spec(RW), _row_spec(RW), _row_spec(RW, 3), _row_spec(RW, 0)],
                 out_specs=[_row_spec(RW), _row_spec(RW)],
                 out_shape=[jax.ShapeDtypeStruct((T, RW), F32), jax.ShapeDtypeStruct((T, RW), BF16)],
                 operands=(o_f, o_b, p, dmix), sem=("parallel",))


def ret_qkv_grad(dqf, dqb, dkf, dkb, dvf, dvb, cos, sin, n_heads, name):
    T, RW = dqf.shape
    C = RET_CHUNK
    scale = C ** -0.5

    def compute(qf, qb, kf, kb, vf, vb, cos_ref, sin_ref, dq_ref, dk_ref, dv_ref):
        cs, sn = cos_ref[...], sin_ref[...]
        for h in range(n_heads):
            sl = slice(h * C, (h + 1) * C)
            dq_ref[:, sl] = _rope_t(qf[:, sl] + qb[:, sl], cs, sn).astype(dq_ref.dtype)
            dk_ref[:, sl] = _rope_t((kf[:, sl] + kb[:, sl]) * scale, cs, sn).astype(dk_ref.dtype)
            dv_ref[:, sl] = (vf[:, sl] + vb[:, sl]).astype(dv_ref.dtype)

    return _call(compute, name=name, grid=(T // ROW_TILE,),
                 in_specs=[_row_spec(RW)] * 6 + [_row_spec(C), _row_spec(C)],
                 out_specs=[_row_spec(RW)] * 3, out_shape=[jax.ShapeDtypeStruct((T, RW), BF16)] * 3,
                 operands=(dqf, dqb, dkf, dkb, dvf, dvb, cos, sin), sem=("parallel",))


def _halo_specs(width, col, halo, n_rows):
    per = ROW_TILE // halo
    last = n_rows // halo - 1
    return [pl.BlockSpec((halo, width), lambda i: (jnp.maximum(i * per - 1, 0), col)),
            pl.BlockSpec((ROW_TILE, width), lambda i: (i, col)),
            pl.BlockSpec((halo, width), lambda i: (jnp.minimum((i + 1) * per, last), col))]


def _halo_valid(i, n_tiles):
    return i >= 2, jnp.logical_and(i >= 1, i <= n_tiles - 2)


def _shifted_copies(S):
    base = S[0]
    rows = base.shape[0]
    for s in range(1, 8):
        S[s] = pltpu.roll(base, rows - s, 0)


def _shifted_rows(S, start, n, cols=slice(None)):
    s = start % 8
    return S[s, start - s:start - s + n, cols]


def _fma_groups(acc, w_row, window):
    w8 = jnp.broadcast_to(w_row, acc[0].shape)
    return [a + w8 * window(g) for g, a in enumerate(acc)]


def _fold8(v):
    parts = [v[r:r + 8] for r in range(0, v.shape[0], 8)]
    while len(parts) > 1:
        parts = [parts[k] + parts[k + 1] for k in range(0, len(parts) - 1, 2)] + (parts[-1:] if len(parts) % 2 else [])
    return parts[0]


def conv_fwd(p, w, bias, ln_g, ln_b, name, comm=None):
    T = p.shape[0]
    CW = w.shape[1]
    NT = T // ROW_TILE
    HL = CONV_HALO
    PAD = CONV_K // 2

    def compute(ap, ac, an, bp, bc, bn, w_ref, b_ref, g_ref, be_ref, u2_ref, out_ref, US):
        i = pl.program_id(0)
        vp, vn = _halo_valid(i, NT)
        US[0, 0:HL, :] = jnp.where(vp, ap[...] * _sigmoid(bp[...]), 0.0)
        US[0, HL:HL + ROW_TILE, :] = ac[...] * _sigmoid(bc[...])
        US[0, HL + ROW_TILE:, :] = jnp.where(vn, an[...] * _sigmoid(bn[...]), 0.0)
        _shifted_copies(US)
        for r0 in range(0, ROW_TILE, CONV_ROWS):
            acc = [jnp.zeros((8, CW), F32) + b_ref[...]] * (CONV_ROWS // 8)
            for j in range(CONV_K):
                acc = _fma_groups(acc, w_ref[j:j + 1, :],
                                  lambda g: _shifted_rows(US, HL - PAD + j + r0 + 8 * g, 8))
            acc = jnp.concatenate(acc, axis=0)
            u2_ref[r0:r0 + CONV_ROWS, :] = acc
            mu = jnp.mean(acc, axis=-1, keepdims=True)
            xc = acc - mu
            rstd = lax.rsqrt(jnp.mean(xc * xc, axis=-1, keepdims=True) + EPS)
            ln = xc * rstd * g_ref[...] + be_ref[...]
            out_ref[r0:r0 + CONV_ROWS, :] = (ln * _sigmoid(ln)).astype(out_ref.dtype)

    vec = _acc_spec(1, CW)
    return _call(compute, name=name, grid=(NT,),
                 in_specs=_halo_specs(CW, 4, HL, T) + _halo_specs(CW, 5, HL, T) + [_acc_spec(CONV_K, CW), vec, vec, vec],
                 out_specs=[_row_spec(CW), _row_spec(CW)],
                 out_shape=[jax.ShapeDtypeStruct((T, CW), F32), jax.ShapeDtypeStruct((T, CW), BF16)],
                 scratch=[pltpu.VMEM((8, ROW_TILE + 2 * HL, CW), F32)],
                 operands=(p, p, p, p, p, p, w, bias, ln_g, ln_b), sem=("parallel",), comm=comm)


def conv_bwd_ln(u2, dmix, ln_g, ln_b, name):
    T, CW = u2.shape

    def compute(u_ref, d_ref, g_ref, be_ref, du_ref, dg_ref, db_ref, dbias_ref):
        i = pl.program_id(0)

        @pl.when(i == 0)
        def _():
            dg_ref[...] = jnp.zeros_like(dg_ref)
            db_ref[...] = jnp.zeros_like(db_ref)
            dbias_ref[...] = jnp.zeros_like(dbias_ref)

        u = u_ref[...]
        gv = g_ref[...]
        mu = jnp.mean(u, axis=-1, keepdims=True)
        xc = u - mu
        rstd = lax.rsqrt(jnp.mean(xc * xc, axis=-1, keepdims=True) + EPS)
        xh = xc * rstd
        ln = xh * gv + be_ref[...]
        sg = _sigmoid(ln)
        dln = d_ref[...].astype(F32) * sg * (1.0 + ln * (1.0 - sg))
        dg_ref[...] += jnp.sum(dln * xh, axis=0, keepdims=True)
        db_ref[...] += jnp.sum(dln, axis=0, keepdims=True)
        dxh = dln * gv
        du = rstd * (dxh - jnp.mean(dxh, axis=-1, keepdims=True)
                     - xh * jnp.mean(dxh * xh, axis=-1, keepdims=True))
        du_ref[...] = du
        dbias_ref[...] += jnp.sum(du, axis=0, keepdims=True)

    vec = _acc_spec(1, CW)
    return _call(compute, name=name, grid=(T // ROW_TILE,),
                 in_specs=[_row_spec(CW), _row_spec(CW, 1), vec, vec],
                 out_specs=[_row_spec(CW), vec, vec, vec],
                 out_shape=[jax.ShapeDtypeStruct((T, CW), F32)] + [jax.ShapeDtypeStruct((1, CW), F32)] * 3,
                 operands=(u2, dmix, ln_g, ln_b), sem=("arbitrary",))


def conv_bwd_taps(p, du2, w, name):
    T = p.shape[0]
    CW = w.shape[1]
    NT = T // ROW_TILE
    HL = CONV_HALO
    PAD = CONV_K // 2

    def compute(ap, ac, an, bp, bc, bn, dp, dc, dn, w_ref, da_ref, db_ref, dw_ref, US, DUS):
        i = pl.program_id(0)

        @pl.when(i == 0)
        def _():
            dw_ref[...] = jnp.zeros_like(dw_ref)

        vp, vn = _halo_valid(i, NT)
        US[0, 0:HL, :] = jnp.where(vp, ap[...] * _sigmoid(bp[...]), 0.0)
        US[0, HL:HL + ROW_TILE, :] = ac[...] * _sigmoid(bc[...])
        US[0, HL + ROW_TILE:, :] = jnp.where(vn, an[...] * _sigmoid(bn[...]), 0.0)
        DUS[0, 0:HL, :] = jnp.where(vp, dp[...], 0.0)
        DUS[0, HL:HL + ROW_TILE, :] = dc[...]
        DUS[0, HL + ROW_TILE:, :] = jnp.where(vn, dn[...], 0.0)
        _shifted_copies(US)
        _shifted_copies(DUS)
        for r0 in range(0, ROW_TILE, CONV_ROWS):
            du = [jnp.zeros((8, CW), F32)] * (CONV_ROWS // 8)
            for j in range(CONV_K):
                du = _fma_groups(du, w_ref[j:j + 1, :],
                                 lambda g: _shifted_rows(DUS, HL + PAD - j + r0 + 8 * g, 8))
            du = jnp.concatenate(du, axis=0)
            a = ac[r0:r0 + CONV_ROWS, :]
            sg = _sigmoid(bc[r0:r0 + CONV_ROWS, :])
            da_ref[r0:r0 + CONV_ROWS, :] = (du * sg).astype(da_ref.dtype)
            db_ref[r0:r0 + CONV_ROWS, :] = (du * a * sg * (1.0 - sg)).astype(db_ref.dtype)
        for c0 in range(0, CW, 128):
            cols = slice(c0, c0 + 128)
            accs = [jnp.zeros((8, 128), F32)] * CONV_K
            for r0 in range(0, ROW_TILE, CONV_ROWS):
                d = DUS[0, HL + r0:HL + r0 + CONV_ROWS, cols]
                accs = [acc + _fold8(d * _shifted_rows(US, HL - PAD + j + r0, CONV_ROWS, cols))
                        for j, acc in enumerate(accs)]
            for j in range(CONV_K):
                dw_ref[j:j + 1, cols] += jnp.sum(accs[j], axis=0, keepdims=True)

    return _call(compute, name=name, grid=(NT,),
                 in_specs=(_halo_specs(CW, 4, HL, T) + _halo_specs(CW, 5, HL, T) + _halo_specs(CW, 0, HL, T)
                           + [_acc_spec(CONV_K, CW)]),
                 out_specs=[_row_spec(CW), _row_spec(CW), _acc_spec(CONV_K, CW)],
                 out_shape=[jax.ShapeDtypeStruct((T, CW), BF16), jax.ShapeDtypeStruct((T, CW), BF16),
                            jax.ShapeDtypeStruct((CONV_K, CW), F32)],
                 scratch=[pltpu.VMEM((8, ROW_TILE + 2 * HL, CW), F32)] * 2,
                 operands=(p, p, p, p, p, p, du2, du2, du2, w), sem=("arbitrary",))


def _ffn_halo_specs(tc, col0, n_rows):
    per = ROW_TILE // FFN_HALO
    last = n_rows // FFN_HALO - 1
    return [pl.BlockSpec((FFN_HALO, tc), lambda cb, i: (jnp.maximum(i * per - 1, 0), col0 + cb)),
            pl.BlockSpec((ROW_TILE, tc), lambda cb, i: (i, col0 + cb)),
            pl.BlockSpec((FFN_HALO, tc), lambda cb, i: (jnp.minimum((i + 1) * per, last), col0 + cb))]


def _ffn_fill(S, prev, cur, nxt, i, n_tiles):
    vp, vn = _halo_valid(i, n_tiles)
    HL = FFN_HALO
    S[1, 0:HL, :] = jnp.where(vp, prev[...].astype(F32), 0.0)
    S[1, HL:HL + ROW_TILE, :] = cur[...].astype(F32)
    S[1, HL + ROW_TILE:, :] = jnp.where(vn, nxt[...].astype(F32), 0.0)
    base = S[1]
    rows = base.shape[0]
    is_lat = i >= 1
    col = lax.broadcasted_iota(jnp.int32, base.shape, 0) & (GRID_W - 1)
    S[0] = jnp.where(jnp.logical_and(is_lat, col == 0), 0.0, pltpu.roll(base, 1, 0))
    S[2] = jnp.where(jnp.logical_and(is_lat, col == GRID_W - 1), 0.0, pltpu.roll(base, rows - 1, 0))


def _ffn_acc_rows(tc):
    return max(8, min(64, (32 // (tc // 128)) * 8))


def _ffn_row_factor(i, di):
    return 1.0 if di == 1 else jnp.where(i >= 1, 1.0, 0.0)


def ffn_conv_fwd(up, w9, bias, name):
    T = up.shape[0]
    DFF = w9.shape[1]
    tc = _pick(DFF, FFN_COL_TILES)
    ncb = DFF // tc
    NT = T // ROW_TILE
    HL = FFN_HALO

    def compute(gp, gc, gn, val_ref, w_ref, b_ref, cg_ref, act_ref, G):
        i = pl.program_id(1)
        _ffn_fill(G, gp, gc, gn, i, NT)
        rows = _ffn_acc_rows(tc)
        for r0 in range(0, ROW_TILE, rows):
            acc = [jnp.zeros((8, tc), F32) + b_ref[...]] * (rows // 8)
            for di in range(3):
                for dj in range(3):
                    wt = w_ref[3 * di + dj:3 * di + dj + 1, :] * _ffn_row_factor(i, di)
                    lo = HL + r0 + (di - 1) * GRID_W
                    acc = _fma_groups(acc, wt, lambda g: G[dj, lo + 8 * g:lo + 8 * g + 8, :])
            acc = jnp.concatenate(acc, axis=0)
            cg_ref[r0:r0 + rows, :] = acc.astype(cg_ref.dtype)
            val = val_ref[r0:r0 + rows, :].astype(F32)
            act_ref[r0:r0 + rows, :] = (acc * _sigmoid(acc) * val).astype(act_ref.dtype)

    tile = pl.BlockSpec((ROW_TILE, tc), lambda cb, i: (i, cb))
    return _call(compute, name=name, grid=(ncb, NT),
                 in_specs=_ffn_halo_specs(tc, 0, T) + [pl.BlockSpec((ROW_TILE, tc), lambda cb, i: (i, ncb + cb)),
                                                       pl.BlockSpec((9, tc), lambda cb, i: (0, cb)),
                                                       pl.BlockSpec((1, tc), lambda cb, i: (0, cb))],
                 out_specs=[tile, tile],
                 out_shape=[jax.ShapeDtypeStruct((T, DFF), BF16), jax.ShapeDtypeStruct((T, DFF), BF16)],
                 scratch=[pltpu.VMEM((3, ROW_TILE + 2 * HL, tc), F32)],
                 operands=(up, up, up, up, w9, bias), sem=("parallel", "parallel"))


def ffn_conv_bwd_act(cg, up, dact, name):
    T, DFF = cg.shape
    tc = _pick(DFF, FFN_COL_TILES)
    ncb = DFF // tc

    def compute(cg_ref, val_ref, d_ref, dcg_ref, dval_ref, db_ref):
        i = pl.program_id(1)

        @pl.when(i == 0)
        def _():
            db_ref[...] = jnp.zeros_like(db_ref)

        c = cg_ref[...].astype(F32)
        sg = _sigmoid(c)
        d = d_ref[...].astype(F32)
        dval_ref[...] = (d * c * sg).astype(dval_ref.dtype)
        dcg = d * val_ref[...].astype(F32) * sg * (1.0 + c * (1.0 - sg))
        dcg_ref[...] = dcg.astype(dcg_ref.dtype)
        db_ref[...] += jnp.sum(dcg, axis=0, keepdims=True)

    tile = pl.BlockSpec((ROW_TILE, tc), lambda cb, i: (i, cb))
    return _call(compute, name=name, grid=(ncb, T // ROW_TILE),
                 in_specs=[tile, pl.BlockSpec((ROW_TILE, tc), lambda cb, i: (i, ncb + cb)), tile],
                 out_specs=[tile, tile, pl.BlockSpec((1, tc), lambda cb, i: (0, cb))],
                 out_shape=[jax.ShapeDtypeStruct((T, DFF), BF16), jax.ShapeDtypeStruct((T, DFF), BF16),
                            jax.ShapeDtypeStruct((1, DFF), F32)],
                 operands=(cg, up, dact), sem=("parallel", "arbitrary"))


def ffn_conv_bwd_taps(up, dcg, w9, name, comm=None):
    T, DFF = dcg.shape
    tc = _pick(DFF, FFN_COL_TILES)
    ncb = DFF // tc
    NT = T // ROW_TILE
    HL = FFN_HALO

    def compute(gp, gc, gn, dp, dc, dn, w_ref, dgate_ref, dw_ref, G, DC):
        i = pl.program_id(1)

        @pl.when(i == 0)
        def _():
            dw_ref[...] = jnp.zeros_like(dw_ref)

        _ffn_fill(G, gp, gc, gn, i, NT)
        _ffn_fill(DC, dp, dc, dn, i, NT)
        rows = _ffn_acc_rows(tc)
        for r0 in range(0, ROW_TILE, rows):
            dg = [jnp.zeros((8, tc), F32)] * (rows // 8)
            for di in range(3):
                for dj in range(3):
                    wt = w_ref[3 * di + dj:3 * di + dj + 1, :] * _ffn_row_factor(i, di)
                    lo = HL + r0 - (di - 1) * GRID_W
                    dg = _fma_groups(dg, wt, lambda g: DC[2 - dj, lo + 8 * g:lo + 8 * g + 8, :])
            dgate_ref[r0:r0 + rows, :] = jnp.concatenate(dg, axis=0).astype(dgate_ref.dtype)
        for c0 in range(0, tc, 128):
            cols = slice(c0, c0 + 128)
            accs = [jnp.zeros((8, 128), F32)] * 9
            for r0 in range(0, ROW_TILE, FFN_TAP_ROWS):
                d = DC[1, HL + r0:HL + r0 + FFN_TAP_ROWS, cols]
                for di in range(3):
                    lo = HL + r0 + (di - 1) * GRID_W
                    for dj in range(3):
                        accs[3 * di + dj] = accs[3 * di + dj] + _fold8(d * G[dj, lo:lo + FFN_TAP_ROWS, cols])
            for di in range(3):
                for dj in range(3):
                    t = 3 * di + dj
                    dw_ref[t:t + 1, cols] += _ffn_row_factor(i, di) * jnp.sum(accs[t], axis=0, keepdims=True)

    tile = pl.BlockSpec((ROW_TILE, tc), lambda cb, i: (i, cb))
    return _call(compute, name=name, grid=(ncb, NT),
                 in_specs=_ffn_halo_specs(tc, 0, T) + _ffn_halo_specs(tc, 0, T) + [pl.BlockSpec((9, tc), lambda cb, i: (0, cb))],
                 out_specs=[tile, pl.BlockSpec((9, tc), lambda cb, i: (0, cb))],
                 out_shape=[jax.ShapeDtypeStruct((T, DFF), BF16), jax.ShapeDtypeStruct((9, DFF), F32)],
                 scratch=[pltpu.VMEM((3, ROW_TILE + 2 * HL, tc), F32)] * 2,
                 operands=(up, up, up, dcg, dcg, dcg, w9), sem=("parallel", "arbitrary"), comm=comm)


def _adamw_update(g, w_ref, m_ref, v_ref, g_ref, d_ref, nm_ref, nv_ref):
    c1 = 1.0 - ADAM_B1 ** ADAM_STEP
    c2 = 1.0 - ADAM_B2 ** ADAM_STEP
    nm = ADAM_B1 * m_ref[...] + (1.0 - ADAM_B1) * g
    nv = ADAM_B2 * v_ref[...] + (1.0 - ADAM_B2) * (g * g)
    g_ref[...] = g
    nm_ref[...] = nm
    nv_ref[...] = nv
    d_ref[...] = -ADAM_LR * ((nm / c1) / (jnp.sqrt(nv / c2) + ADAM_EPS) + ADAM_WD * w_ref[...])


def _sum_parts(p_ref, P):
    g = p_ref[0].astype(F32)
    for k in range(1, P):
        g = g + p_ref[k].astype(F32)
    return g


def adamw(parts, w, m, v, name):
    P, R, C = parts.shape
    fits = lambda t: 2 * (P + 7) * t * C * 4 <= ADAM_VMEM_BYTES
    tr = R if fits(R) else _pick(R, [t for t in (1024, 512, 256, 128, 64, 32, 16, 8) if fits(t)])

    def compute(p_ref, w_ref, m_ref, v_ref, g_ref, d_ref, nm_ref, nv_ref):
        _adamw_update(_sum_parts(p_ref, P), w_ref, m_ref, v_ref, g_ref, d_ref, nm_ref, nv_ref)

    tile = pl.BlockSpec((tr, C), lambda i: (i, 0))
    return _call(compute, name=name, grid=(R // tr,),
                 in_specs=[pl.BlockSpec((P, tr, C), lambda i: (0, i, 0)), tile, tile, tile],
                 out_specs=[tile] * 4, out_shape=[jax.ShapeDtypeStruct((R, C), F32)] * 4,
                 operands=(parts, w, m, v), sem=("parallel",))


def adamw_layers(parts_l, w, m, v, name, comm=None):
    L = len(parts_l)
    P, R, C = parts_l[0].shape
    psize = parts_l[0].dtype.itemsize
    fits = lambda t: 2 * t * C * (L * P * psize + 7 * 4) <= ADAM_VMEM_BYTES
    tr = _pick(R, [t for t in (1024, 512, 256, 128, 64, 32, 16, 8) if fits(t)])

    def compute(*refs):
        p_refs, (w_ref, m_ref, v_ref), outs = refs[:L], refs[L:L + 3], refs[L + 3:]
        for l in range(L):
            @pl.when(pl.program_id(0) == l)
            def _(l=l):
                _adamw_update(_sum_parts(p_refs[l], P), w_ref, m_ref, v_ref, *outs)

    tile = pl.BlockSpec((None, tr, C), lambda l, i: (l, i, 0))
    part = lambda k: pl.BlockSpec((P, tr, C), lambda l, i: (0, jnp.where(l == k, i, 0), 0))
    return _call(compute, name=name, grid=(L, R // tr),
                 in_specs=[part(k) for k in range(L)] + [tile] * 3,
                 out_specs=[tile] * 4, out_shape=[jax.ShapeDtypeStruct((L, R, C), F32)] * 4,
                 operands=(*parts_l, w, m, v), sem=("arbitrary", "arbitrary"), comm=comm)


def _rope_tables(seq, ctx):
    t = jnp.arange(seq)
    quarter = RET_CHUNK // 4
    inv_freq = 1.0 / (ROPE_THETA ** (jnp.arange(0, quarter, dtype=F32) / quarter))
    ang_r = (t // GRID_W).astype(F32)[:, None] * inv_freq[None, :]
    ang_c = (t % GRID_W).astype(F32)[:, None] * inv_freq[None, :]
    cr, sr, cc, sc = jnp.cos(ang_r), jnp.sin(ang_r), jnp.cos(ang_c), jnp.sin(ang_c)
    cos = jnp.concatenate([cr, cr, cc, cc], axis=-1)
    sin = jnp.concatenate([-sr, sr, -sc, sc], axis=-1)
    cos = jnp.concatenate([jnp.ones((ctx, RET_CHUNK), F32), cos], axis=0)
    sin = jnp.concatenate([jnp.zeros((ctx, RET_CHUNK), F32), sin], axis=0)
    return cos, sin


def _decay_tables(decay_logit, rev):
    C = RET_CHUNK
    lg = jax.nn.log_sigmoid(decay_logit.astype(F32))
    idx = jnp.arange(C, dtype=F32)
    diff = idx[:, None] - idx[None, :]
    if rev:
        diff = -diff
        eq, ek = C - idx, idx
    else:
        eq, ek = idx + 1.0, C - 1.0 - idx
    keep = diff >= 0
    em = jnp.where(keep, diff, 0.0)
    bc = lambda e: jnp.broadcast_to(e[:, None], (C, C))
    return {
        "dm": jnp.where(keep[None], jnp.exp(lg[:, None, None] * em[None]), 0.0),
        "qd": jnp.broadcast_to(jnp.exp(lg[:, None] * eq[None, :])[:, :, None], (lg.shape[0], C, C)),
        "kd": jnp.broadcast_to(jnp.exp(lg[:, None] * ek[None, :])[:, :, None], (lg.shape[0], C, C)),
        "cd": jnp.broadcast_to(jnp.exp(lg * C)[:, None, None], (lg.shape[0], 1, C)),
        "em": em, "eq": bc(eq), "ek": bc(ek),
    }


def _silu(z):
    return z * jax.nn.sigmoid(z)


def _dsilu(z):
    s = jax.nn.sigmoid(z)
    return s * (1.0 + z * (1.0 - s))


def kernel(x, c, ctx, c_ctx, w_mod, b_mod, norm1_g, norm2_g, w_in, ret_decay_f, ret_decay_b, conv_dw_w, conv_dw_b, conv_ln_g, conv_ln_b, w_out, ffn_w_up, ffn_dw_w, ffn_dw_b, ffn_w_down, final_norm_g, loss_target, m_c_ctx, m_w_mod, m_b_mod, m_norm1_g, m_norm2_g, m_w_in, m_ret_decay_f, m_ret_decay_b, m_conv_dw_w, m_conv_dw_b, m_conv_ln_g, m_conv_ln_b, m_w_out, m_ffn_w_up, m_ffn_dw_w, m_ffn_dw_b, m_ffn_w_down, m_final_norm_g, v_c_ctx, v_w_mod, v_b_mod, v_norm1_g, v_norm2_g, v_w_in, v_ret_decay_f, v_ret_decay_b, v_conv_dw_w, v_conv_dw_b, v_conv_ln_g, v_conv_ln_b, v_w_out, v_ffn_w_up, v_ffn_dw_w, v_ffn_dw_b, v_ffn_w_down, v_final_norm_g):
    L, D, _ = w_mod.shape
    SEQ, CTX = x.shape[1], ctx.shape[1]
    T = SEQ + CTX
    RW = D // 2
    CW = D - RW
    H = RW // RET_CHUNK
    DFF = ffn_dw_b.shape[1]
    NMOD = b_mod.shape[1] // D
    n_ctx = CTX // RET_CHUNK
    assert CTX == ROW_TILE and RW == CW and SEQ % ROW_TILE == 0 and NMOD == 6
    me = _my_rank()
    wm_n = w_mod.shape[2]
    wo_k, wd_k = w_out.shape[1], ffn_w_down.shape[1]
    cw_n, fw_n = conv_dw_w.shape[2], ffn_dw_w.shape[3]

    w_mod_b = w_mod.astype(BF16)
    w_in_b, w_out_b, w_up_b, w_down_b = (a.astype(BF16) for a in (w_in, w_out, ffn_w_up, ffn_w_down))
    as_rows = lambda g: g.reshape(1, -1, D)
    g_in, g_out, g_up, g_down, g_cw, g_fw, g_c = run_comm(
        "gather", [w_in_b[0], w_out_b[0], w_up_b[0], w_down_b[0], conv_dw_w, ffn_dw_w, _silu(c)], "gather_first")
    w_in_l, w_up_l, w_out_l, w_down_l = [g_in], [g_up], [as_rows(g_out)], [as_rows(g_down)]
    conv_w_l = [jnp.moveaxis(g_cw[:, l], 0, 1).reshape(CONV_K, CW) for l in range(L)]
    ffn_w9_l = [jnp.moveaxis(g_fw[:, l], 0, 2).reshape(9, DFF) for l in range(L)]

    s_cond = jnp.concatenate([g_c.reshape(N_DEV, D), jnp.broadcast_to(_silu(c_ctx)[None], (N_DEV, D))], axis=0)
    s_cond_b = s_cond.astype(BF16)
    mod_shard = mm_nn(s_cond_b, w_mod_b, F32, "mod_fwd")[0]
    (g_mod,) = run_comm("gather", [mod_shard], "gather_mod")
    mod_all = jnp.transpose(g_mod.reshape(N_DEV, 2 * N_DEV, L, wm_n), (2, 1, 0, 3)).reshape(L, 2 * N_DEV, NMOD * D)
    mod_all = mod_all + b_mod[:, None, :]
    mod_lat = lax.dynamic_index_in_dim(mod_all, me, axis=1, keepdims=False)
    mod_ctx = mod_all[:, N_DEV]
    mod2 = jnp.stack([mod_ctx, mod_lat], axis=1).reshape(L, 2, NMOD, D)

    cos, sin = _rope_tables(SEQ, CTX)
    xs = jnp.concatenate([ctx[0], x[0]], axis=0)

    saved = []
    for l in range(L):
        nxt = l + 1 < L
        sh1, sc1, g1, sh2, sc2, g2 = (mod2[l, :, k] for k in range(NMOD))
        tf = _decay_tables(ret_decay_f[l], False)
        tb = _decay_tables(ret_decay_b[l], True)
        h = rms_mod_fwd(xs, norm1_g[l][None], sh1, sc1, "norm1_fwd")
        if nxt:
            (p,), (gi, go) = mm_nn(h, w_in_l[l], F32, "in_proj_g", Comm("gather", [w_in_b[l + 1], w_out_b[l + 1]]))
            w_in_l.append(gi)
            w_out_l.append(as_rows(go))
        else:
            (p,) = mm_nn(h, w_in_l[l], F32, "in_proj")
        o_f, s_f = ret_fwd(p, cos, sin, tf, H, n_ctx, False, "ret_fwd_f")
        o_b, s_b = ret_fwd(p, cos, sin, tb, H, n_ctx, True, "ret_fwd_b")
        mix_r = ret_out_fwd(o_f, o_b, p, H, "ret_out_fwd")
        u2, mix_c = conv_fwd(p, conv_w_l[l], conv_dw_b[l][None], conv_ln_g[l][None], conv_ln_b[l][None], "conv_fwd")
        mix = jnp.concatenate([mix_r, mix_c], axis=1)
        (y1,) = mm_nn(mix, w_out_l[l], F32, "out_proj")
        x2 = gate_res_fwd(xs, y1, g1, "res1_fwd")
        h2 = rms_mod_fwd(x2, norm2_g[l][None], sh2, sc2, "norm2_fwd")
        if nxt:
            (up,), (gu,) = mm_nn(h2, w_up_l[l], BF16, "ffn_up_g", Comm("gather", [w_up_b[l + 1]]))
            w_up_l.append(gu)
        else:
            (up,) = mm_nn(h2, w_up_l[l], BF16, "ffn_up")
        cg, act = ffn_conv_fwd(up, ffn_w9_l[l], ffn_dw_b[l][None], "ffn_conv_fwd")
        if nxt:
            (y2,), (gd,) = mm_nn(act, w_down_l[l], F32, "ffn_down_g", Comm("gather", [w_down_b[l + 1]]))
            w_down_l.append(as_rows(gd))
        else:
            (y2,) = mm_nn(act, w_down_l[l], F32, "ffn_down")
        x3 = gate_res_fwd(x2, y2, g2, "res2_fwd")
        saved.append(dict(x1=xs, h=h, p=p, o_f=o_f, o_b=o_b, s_f=s_f, s_b=s_b, u2=u2, mix=mix, y1=y1, x2=x2,
                          h2=h2, up=up, cg=cg, act=act, y2=y2, tf=tf, tb=tb))
        xs = x3

    dxs, d_final_g, loss_part = final_loss(xs, final_norm_g[None], loss_target[0], "final_loss")
    loss = lax.psum(loss_part[0, 0], ("x", "y", "c"))

    landed = {n: [None] * L for n in ("w_in", "w_out", "ffn_w_up", "ffn_w_down")}
    small = {n: [None] * L for n in ("norm1_g", "norm2_g", "ret_decay_f", "ret_decay_b", "conv_dw_w", "conv_dw_b",
                                     "conv_ln_g", "conv_ln_b", "ffn_dw_w", "ffn_dw_b")}
    dmod2 = [None] * L
    g_in_prev = g_up_prev = None
    for l in reversed(range(L)):
        sv = saved[l]
        sh1, sc1, g1, sh2, sc2, g2 = (mod2[l, :, k] for k in range(NMOD))
        dy2, dg2 = gate_bwd(dxs, sv["y2"], g2, "res2_bwd")
        if g_in_prev is not None:
            (dact,), (landed["w_in"][l + 1],) = mm_nt(dy2, w_down_l[l], BF16, "ffn_down_dx_x", Comm("exchange", [g_in_prev]))
        else:
            (dact,) = mm_nt(dy2, w_down_l[l], BF16, "ffn_down_dx")
        g_down = mm_tn(sv["act"], dy2, 1, BF16, "ffn_down_dw").reshape(N_DEV, wd_k, D)
        dcg, dval, small["ffn_dw_b"][l] = ffn_conv_bwd_act(sv["cg"], sv["up"], dact, "ffn_conv_bwd_act")
        (dgate, small["ffn_dw_w"][l]), (landed["ffn_w_down"][l],) = ffn_conv_bwd_taps(
            sv["up"], dcg, ffn_w9_l[l], "ffn_conv_bwd_taps", Comm("exchange", [g_down]))
        dup = [dgate, dval]
        if g_up_prev is not None:
            (dh2,), (landed["ffn_w_up"][l + 1],) = mm_nt(dup, w_up_l[l], F32, "ffn_up_dx_x", Comm("exchange", [g_up_prev]))
        else:
            (dh2,) = mm_nt(dup, w_up_l[l], F32, "ffn_up_dx")
        g_up_prev = mm_tn(sv["h2"], dup, N_DEV, BF16, "ffn_up_dw")
        dx2, small["norm2_g"][l], dsh2, dsc2 = rms_mod_bwd(sv["x2"], norm2_g[l][None], sc2, dh2, dxs, "norm2_bwd")
        dy1, dg1 = gate_bwd(dx2, sv["y1"], g1, "res1_bwd")
        (dmix,) = mm_nt(dy1, w_out_l[l], BF16, "out_proj_dx")
        g_out = mm_tn(sv["mix"], dy1, 1, BF16, "out_proj_dw").reshape(N_DEV, wo_k, D)
        do, dgt = ret_out_bwd(sv["o_f"], sv["o_b"], sv["p"], dmix, H, "ret_out_bwd")
        dqf, dkf, dvf, dlg_f = ret_bwd(sv["p"], cos, sin, sv["tf"], do, sv["s_f"], H, n_ctx, False, "ret_bwd_f")
        (dqb, dkb, dvb, dlg_b), (landed["w_out"][l],) = ret_bwd(
            sv["p"], cos, sin, sv["tb"], do, sv["s_b"], H, n_ctx, True, "ret_bwd_b", Comm("exchange", [g_out]))
        dq, dk, dv = ret_qkv_grad(dqf, dqb, dkf, dkb, dvf, dvb, cos, sin, H, "ret_qkv_grad")
        small["ret_decay_f"][l] = jnp.sum(dlg_f[:, 0, :], axis=-1) * jax.nn.sigmoid(-ret_decay_f[l])
        small["ret_decay_b"][l] = jnp.sum(dlg_b[:, 0, :], axis=-1) * jax.nn.sigmoid(-ret_decay_b[l])
        du2, small["conv_ln_g"][l], small["conv_ln_b"][l], small["conv_dw_b"][l] = conv_bwd_ln(
            sv["u2"], dmix, conv_ln_g[l][None], conv_ln_b[l][None], "conv_bwd_ln")
        da, dbg, small["conv_dw_w"][l] = conv_bwd_taps(sv["p"], du2, conv_w_l[l], "conv_bwd_taps")
        dp = jnp.concatenate([dq, dk, dv, dgt, da, dbg], axis=1)
        if l == 0:
            (dh,), (landed["ffn_w_up"][0],) = mm_nt(dp, w_in_l[l], F32, "in_proj_dx_x", Comm("exchange", [g_up_prev]))
        else:
            (dh,) = mm_nt(dp, w_in_l[l], F32, "in_proj_dx")
        g_in_prev = mm_tn(sv["h"], dp, N_DEV, BF16, "in_proj_dw")
        dxs, small["norm1_g"][l], dsh1, dsc1 = rms_mod_bwd(sv["x1"], norm1_g[l][None], sc1, dh, dx2, "norm1_bwd")
        dmod2[l] = jnp.concatenate([dsh1, dsc1, dg1, dsh2, dsc2, dg2], axis=1)

    grad_x = dxs[CTX:][None]

    dmod2 = jnp.stack(dmod2)
    (g_dmod,) = run_comm("gather", [dmod2], "gather_dmod")
    dmod_all = jnp.concatenate([jnp.moveaxis(g_dmod[:, :, 1], 0, 1), jnp.moveaxis(g_dmod[:, :, 0], 0, 1)], axis=1)
    dmod_sh = lax.dynamic_slice_in_dim(dmod_all, me * wm_n, wm_n, axis=2)
    dmod_sh = jnp.moveaxis(dmod_sh, 0, 1).reshape(2 * N_DEV, L * wm_n).astype(BF16)
    g_w_mod = mm_tn(s_cond_b, dmod_sh, L, F32, "mod_dw")
    (d_cond,) = mm_nt(dmod_sh, w_mod_b, F32, "mod_dx")
    g_c_ctx_part = jnp.sum(d_cond[N_DEV:], axis=0) * _dsilu(c_ctx)
    g_b_mod_part = dmod2[:, 0] + dmod2[:, 1]

    pad128 = lambda a: jnp.pad(a.reshape(-1), (0, (-a.size) % 128))
    rep_names = ["c_ctx", "b_mod", "norm1_g", "norm2_g", "ret_decay_f", "ret_decay_b", "conv_dw_b", "conv_ln_g",
                 "conv_ln_b", "ffn_dw_b", "final_norm_g"]
    given = dict(c_ctx=(c_ctx, m_c_ctx, v_c_ctx), b_mod=(b_mod, m_b_mod, v_b_mod),
                 norm1_g=(norm1_g, m_norm1_g, v_norm1_g), norm2_g=(norm2_g, m_norm2_g, v_norm2_g),
                 ret_decay_f=(ret_decay_f, m_ret_decay_f, v_ret_decay_f),
                 ret_decay_b=(ret_decay_b, m_ret_decay_b, v_ret_decay_b),
                 conv_dw_b=(conv_dw_b, m_conv_dw_b, v_conv_dw_b), conv_ln_g=(conv_ln_g, m_conv_ln_g, v_conv_ln_g),
                 conv_ln_b=(conv_ln_b, m_conv_ln_b, v_conv_ln_b), ffn_dw_b=(ffn_dw_b, m_ffn_dw_b, v_ffn_dw_b),
                 final_norm_g=(final_norm_g, m_final_norm_g, v_final_norm_g))
    rep_part = dict(c_ctx=g_c_ctx_part, b_mod=g_b_mod_part, final_norm_g=d_final_g)
    for nme in rep_names:
        if nme not in rep_part:
            rep_part[nme] = jnp.stack([a.reshape(-1) for a in small[nme]])
    rep_sizes = [((-given[nme][0].size) % 128) + given[nme][0].size for nme in rep_names]
    n_rep = sum(rep_sizes)
    cw_part = jnp.stack(small["conv_dw_w"])
    fw_part = jnp.stack(small["ffn_dw_w"])
    packed = jnp.concatenate([pad128(rep_part[nme]) for nme in rep_names] + [cw_part.reshape(-1), fw_part.reshape(-1)])
    (g_small,) = run_comm("gather", [packed.reshape(-1, 128)], "gather_small")
    g_small = g_small.reshape(N_DEV, -1)
    rep_w, rep_m, rep_v = (jnp.concatenate([pad128(given[nme][k]) for nme in rep_names]).reshape(-1, 128) for k in range(3))
    rep_out = adamw(g_small[:, :n_rep].reshape(N_DEV, -1, 128), rep_w, rep_m, rep_v, "adamw_small")
    res = {}
    off = 0
    for nme, sz in zip(rep_names, rep_sizes):
        shape = given[nme][0].shape
        res[nme] = [o.reshape(-1)[off:off + given[nme][0].size].reshape(shape) for o in rep_out]
        off += sz

    cw_all = g_small[:, n_rep:n_rep + cw_part.size].reshape(N_DEV, L * CONV_K, CW)
    cw_mine = lax.dynamic_slice_in_dim(cw_all, me * cw_n, cw_n, axis=2)
    res["conv_dw_w"] = [o.reshape(conv_dw_w.shape) for o in adamw(
        cw_mine, conv_dw_w.reshape(L * CONV_K, cw_n), m_conv_dw_w.reshape(L * CONV_K, cw_n),
        v_conv_dw_w.reshape(L * CONV_K, cw_n), "adamw_conv_w")]
    fw_all = g_small[:, n_rep + cw_part.size:].reshape(N_DEV, L * 9, DFF)
    fw_mine = lax.dynamic_slice_in_dim(fw_all, me * fw_n, fw_n, axis=2)
    res["ffn_dw_w"] = [o.reshape(ffn_dw_w.shape) for o in adamw(
        fw_mine, ffn_dw_w.reshape(L * 9, fw_n), m_ffn_dw_w.reshape(L * 9, fw_n),
        v_ffn_dw_w.reshape(L * 9, fw_n), "adamw_ffn_w")]

    res["w_mod"] = [o.reshape(w_mod.shape) for o in adamw(
        g_w_mod.reshape(1, L * D, wm_n), w_mod.reshape(L * D, wm_n), m_w_mod.reshape(L * D, wm_n),
        v_w_mod.reshape(L * D, wm_n), "adamw_w_mod")]

    res["ffn_w_up"], (landed["w_in"][0],) = adamw_layers(landed["ffn_w_up"], ffn_w_up, m_ffn_w_up, v_ffn_w_up,
                                                         "adamw_ffn_w_up", Comm("exchange", [g_in_prev]))
    res["w_in"] = adamw_layers(landed["w_in"], w_in, m_w_in, v_w_in, "adamw_w_in")
    res["w_out"] = adamw_layers(landed["w_out"], w_out, m_w_out, v_w_out, "adamw_w_out")
    res["ffn_w_down"] = adamw_layers(landed["ffn_w_down"], ffn_w_down, m_ffn_w_down, v_ffn_w_down, "adamw_ffn_w_down")

    order = ["c_ctx", "w_mod", "b_mod", "norm1_g", "norm2_g", "w_in", "ret_decay_f", "ret_decay_b", "conv_dw_w",
             "conv_dw_b", "conv_ln_g", "conv_ln_b", "w_out", "ffn_w_up", "ffn_dw_w", "ffn_dw_b", "ffn_w_down",
             "final_norm_g"]
    return (loss, grad_x, *[res[nme][0] for nme in order], *[res[nme][1] for nme in order],
            *[res[nme][2] for nme in order], *[res[nme][3] for nme in order])
```

```python
import functools

import jax
import jax.numpy as jnp
from jax import lax
from jax.experimental import pallas as pl
from jax.experimental.pallas import tpu as pltpu

F32 = jnp.float32
BF16 = jnp.bfloat16
EPS = 1e-6
N_DEV = 8
ROW_TILE = 256
RET_CHUNK = 128
GRID_W = 64
CONV_K = 31
CONV_HALO = 16
CONV_ROWS = 32
FFN_COL_TILES = (1408, 512, 256, 128)
FFN_TAP_ROWS = 64
FFN_HALO = 128
ROPE_THETA = 10000.0
ADAM_LR = 0.001
ADAM_B1 = 0.9
ADAM_B2 = 0.999
ADAM_EPS = 1e-08
ADAM_WD = 0.01
ADAM_STEP = 10
VMEM_LIMIT = 56 * 1024 * 1024
ADAM_VMEM_BYTES = 24 * 1024 * 1024
MESH = pl.DeviceIdType.MESH
ANY = pl.BlockSpec(memory_space=pl.ANY)


def _pick(n, cands):
    for t in cands:
        if n % t == 0:
            return t
    return n


def _sigmoid(z):
    return 1.0 / (1.0 + jnp.exp(-z))


def _my_rank():
    return 4 * lax.axis_index("x") + 2 * lax.axis_index("y") + lax.axis_index("c")


def _peer(j):
    x, y, c = lax.axis_index("x"), lax.axis_index("y"), lax.axis_index("c")
    px = 1 - x if j & 4 else x
    py = 1 - y if j & 2 else y
    pc = 1 - c if j & 1 else c
    return (px, py, pc), 4 * px + 2 * py + pc


class Comm:
    def __init__(self, kind, arrs):
        assert kind in ("gather", "exchange")
        self.kind, self.arrs, self.n = kind, list(arrs), len(arrs)
        self.in_specs = [ANY] * self.n
        self.out_specs = [ANY] * self.n
        lead = (N_DEV,) if kind == "gather" else ()
        self.out_shape = [jax.ShapeDtypeStruct(lead + a.shape, a.dtype) for a in self.arrs]
        per = self.n * (N_DEV - 1)
        self.scratch = [pltpu.SemaphoreType.DMA((per,)), pltpu.SemaphoreType.DMA((per,)),
                        pltpu.SemaphoreType.DMA((self.n,))]

    def _src(self, ref, rank):
        return ref if self.kind == "gather" else ref.at[rank]

    def _local(self, ins, outs, sems, a):
        me = _my_rank()
        return pltpu.make_async_copy(self._src(ins[a], me), outs[a].at[me], sems[2].at[a])

    def _remote(self, ins, outs, sems, a, j, receive):
        dev, rank = _peer(j)
        s = a * (N_DEV - 1) + j - 1
        slot = rank if receive else _my_rank()
        return pltpu.make_async_remote_copy(src_ref=self._src(ins[a], rank), dst_ref=outs[a].at[slot],
                                            send_sem=sems[0].at[s], recv_sem=sems[1].at[s],
                                            device_id=dev, device_id_type=MESH)

    def _gather_copy(self, ins, outs, sems, a, idx, src_slot, to):
        me = _my_rank()
        slot = me if src_slot is None else src_slot
        src = ins[a] if src_slot is None else outs[a].at[src_slot]
        s = a * (N_DEV - 1) + idx
        return pltpu.make_async_remote_copy(src_ref=src, dst_ref=outs[a].at[slot], send_sem=sems[0].at[s],
                                            recv_sem=sems[1].at[s], device_id=to, device_id_type=MESH)

    def _gather_plan(self):
        x, y, c = lax.axis_index("x"), lax.axis_index("y"), lax.axis_index("c")
        rank = lambda px, py, pc: 4 * px + 2 * py + pc
        chips = [(1 - x, y), (x, 1 - y), (1 - x, 1 - y)]
        return (x, y, 1 - c), rank(x, y, 1 - c), [((px, py, c), rank(px, py, c), rank(px, py, 1 - c)) for px, py in chips]

    def start(self, ins, outs, sems):
        for a in range(self.n):
            self._local(ins, outs, sems, a).start()
        if self.kind == "gather":
            sibling, _, chips = self._gather_plan()
            for a in range(self.n):
                self._gather_copy(ins, outs, sems, a, 0, None, sibling).start()
                for k, (dev, _, _) in enumerate(chips):
                    self._gather_copy(ins, outs, sems, a, 1 + k, None, dev).start()
            return
        for a in range(self.n):
            for j in range(1, N_DEV):
                self._remote(ins, outs, sems, a, j, False).start()

    def wait(self, ins, outs, sems):
        if self.kind == "gather":
            sibling, sib_rank, chips = self._gather_plan()
            for a in range(self.n):
                for k, (dev, slot, _) in enumerate(chips):
                    self._gather_copy(ins, outs, sems, a, 1 + k, slot, dev).wait_recv()
                    self._gather_copy(ins, outs, sems, a, 4 + k, slot, sibling).start()
            for a in range(self.n):
                self._gather_copy(ins, outs, sems, a, 0, sib_rank, sibling).wait_recv()
                for k, (_, _, sib_slot) in enumerate(chips):
                    self._gather_copy(ins, outs, sems, a, 4 + k, sib_slot, sibling).wait_recv()
                self._gather_copy(ins, outs, sems, a, 0, None, sibling).wait_send()
                for k, (dev, slot, _) in enumerate(chips):
                    self._gather_copy(ins, outs, sems, a, 1 + k, None, dev).wait_send()
                    self._gather_copy(ins, outs, sems, a, 4 + k, slot, sibling).wait_send()
        else:
            for a in range(self.n):
                for j in range(1, N_DEV):
                    cp = self._remote(ins, outs, sems, a, j, True)
                    cp.wait_recv()
                    cp.wait_send()
        for a in range(self.n):
            self._local(ins, outs, sems, a).wait()


def _call(compute, *, name, grid, in_specs, out_specs, out_shape, operands, sem, scratch=(), comm=None):
    n_in, n_out, n_sc = len(in_specs), len(out_specs), len(scratch)
    k = comm.n if comm else 0

    def body(*refs):
        ins, cin = refs[:n_in], refs[n_in:n_in + k]
        o0 = n_in + k
        outs, cout = refs[o0:o0 + n_out], refs[o0 + n_out:o0 + n_out + k]
        s0 = o0 + n_out + k
        sc, sems = refs[s0:s0 + n_sc], refs[s0 + n_sc:]
        if comm:
            ids = [pl.program_id(d) for d in range(len(grid))]
            first = functools.reduce(jnp.logical_and, [i == 0 for i in ids])
            last = functools.reduce(jnp.logical_and, [i == g - 1 for i, g in zip(ids, grid)])

            @pl.when(first)
            def _():
                comm.start(cin, cout, sems)

        compute(*ins, *outs, *sc)

        if comm:
            @pl.when(last)
            def _():
                comm.wait(cin, cout, sems)

    semantics = ("arbitrary",) * len(grid) if comm else sem
    res = pl.pallas_call(
        body, name=name, grid=grid,
        in_specs=list(in_specs) + (comm.in_specs if comm else []),
        out_specs=list(out_specs) + (comm.out_specs if comm else []),
        out_shape=list(out_shape) + (comm.out_shape if comm else []),
        scratch_shapes=list(scratch) + (comm.scratch if comm else []),
        compiler_params=pltpu.CompilerParams(dimension_semantics=semantics, vmem_limit_bytes=VMEM_LIMIT),
    )(*operands, *(comm.arrs if comm else []))
    if comm:
        return list(res[:n_out]), list(res[n_out:])
    return list(res)


def run_comm(kind, arrs, name):
    comm = Comm(kind, arrs)

    def body(*refs):
        ins, outs, sems = refs[:comm.n], refs[comm.n:2 * comm.n], refs[2 * comm.n:]
        comm.start(ins, outs, sems)
        comm.wait(ins, outs, sems)

    return list(pl.pallas_call(body, name=name, in_specs=comm.in_specs, out_specs=comm.out_specs,
                               out_shape=comm.out_shape, scratch_shapes=comm.scratch)(*comm.arrs))


M_TILES = (768, 512, 256, 128)
WIDE_TILES = (2048, 1408, 1024, 768, 512, 256, 128)
MID_TILES = (1408, 1024, 768, 512, 256, 128)


def _mm_body(dot, n_steps, axis):
    if n_steps == 1:
        def compute(a_ref, b_ref, o_ref):
            o_ref[...] = dot(a_ref, b_ref).astype(o_ref.dtype).reshape(o_ref.shape)
        return compute, []

    def compute(a_ref, b_ref, o_ref, acc):
        k = pl.program_id(axis)

        @pl.when(k == 0)
        def _():
            acc[...] = jnp.zeros_like(acc)

        acc[...] += dot(a_ref, b_ref)

        @pl.when(k == n_steps - 1)
        def _():
            o_ref[...] = acc[...].astype(o_ref.dtype).reshape(o_ref.shape)

    return compute, None


def mm_nn(a, b3, out_dtype, name, comm=None):
    M, K = a.shape
    R, _, n = b3.shape
    tm, tk, tn = _pick(M, M_TILES), _pick(K, WIDE_TILES), _pick(n, MID_TILES)
    nb, nk = n // tn, K // tk
    compute, scratch = _mm_body(lambda a_ref, b_ref: jnp.dot(a_ref[...], b_ref[0], preferred_element_type=F32), nk, 2)
    return _call(
        compute, name=name, grid=(M // tm, R * nb, nk),
        in_specs=[pl.BlockSpec((tm, tk), lambda i, j, k: (i, k)),
                  pl.BlockSpec((1, tk, tn), lambda i, j, k: (j // nb, k, j % nb))],
        out_specs=[pl.BlockSpec((tm, tn), lambda i, j, k: (i, j))],
        out_shape=[jax.ShapeDtypeStruct((M, R * n), out_dtype)],
        scratch=scratch if scratch is not None else [pltpu.VMEM((tm, tn), F32)],
        operands=(a, b3), sem=("parallel", "parallel", "arbitrary"), comm=comm)


def mm_nn_res(a, b3, x, gate2, n_ctx_rows, name, comm=None):
    M, K = a.shape
    R, _, n = b3.shape
    tm, tk, tn = _pick(M, M_TILES), _pick(K, WIDE_TILES), _pick(n, MID_TILES)
    nb, nk = n // tn, K // tk

    def compute(a_ref, b_ref, x_ref, g_ref, y_ref, xo_ref, *acc):
        k = pl.program_id(2)

        def finish(y):
            row = lax.broadcasted_iota(jnp.int32, y.shape, 0) + pl.program_id(0) * tm
            gate = jnp.where(row < n_ctx_rows, g_ref[0:1, :], g_ref[1:2, :])
            y_ref[...] = y.astype(y_ref.dtype)
            xo_ref[...] = x_ref[...] + gate * y

        part = jnp.dot(a_ref[...], b_ref[0], preferred_element_type=F32)
        if nk == 1:
            finish(part)
            return

        @pl.when(k == 0)
        def _():
            acc[0][...] = jnp.zeros_like(acc[0])

        acc[0][...] += part

        @pl.when(k == nk - 1)
        def _():
            finish(acc[0][...])

    tile = pl.BlockSpec((tm, tn), lambda i, j, k: (i, j))
    return _call(
        compute, name=name, grid=(M // tm, R * nb, nk),
        in_specs=[pl.BlockSpec((tm, tk), lambda i, j, k: (i, k)),
                  pl.BlockSpec((1, tk, tn), lambda i, j, k: (j // nb, k, j % nb)),
                  tile, pl.BlockSpec((2, tn), lambda i, j, k: (0, j))],
        out_specs=[tile, tile],
        out_shape=[jax.ShapeDtypeStruct((M, R * n), BF16), jax.ShapeDtypeStruct((M, R * n), F32)],
        scratch=[] if nk == 1 else [pltpu.VMEM((tm, tn), F32)],
        operands=(a, b3, x, gate2), sem=("parallel", "parallel", "arbitrary"), comm=comm)


def _piece_specs(n_pieces, per, rows, t, row_of, col_of):
    if n_pieces == 1:
        return [pl.BlockSpec((rows, t), lambda *ids: (row_of(*ids), col_of(*ids)))]

    def index(q, *ids):
        local = col_of(*ids) - q * per
        inside = jnp.logical_and(local >= 0, local < per)
        return jnp.where(inside, row_of(*ids), 0), jnp.where(inside, local, 0)

    return [pl.BlockSpec((rows, t), functools.partial(index, q)) for q in range(n_pieces)]


def _for_piece(n_pieces, per, block, fn):
    if n_pieces == 1:
        fn(0)
        return
    for q in range(n_pieces):
        pl.when(block // per == q)(functools.partial(fn, q))


def mm_nt(a, b3, out_dtype, name, comm=None):
    pieces = list(a) if isinstance(a, (list, tuple)) else [a]
    P = len(pieces)
    M, W = pieces[0].shape
    R, K, n = b3.shape
    assert P * W == R * n
    tm, tko, tc = _pick(M, M_TILES), _pick(K, WIDE_TILES), _pick(n, WIDE_TILES)
    ncb = n // tc
    nc = R * ncb
    per = W // tc

    def compute(*refs):
        a_refs, b_ref, o_ref, acc = refs[:P], refs[P], refs[P + 1], refs[P + 2]
        k = pl.program_id(2)

        @pl.when(k == 0)
        def _():
            acc[...] = jnp.zeros_like(acc)

        def add(q):
            acc[...] += lax.dot_general(a_refs[q][...], b_ref[0], (((1,), (1,)), ((), ())),
                                        preferred_element_type=F32)

        _for_piece(P, per, k, add)

        @pl.when(k == nc - 1)
        def _():
            o_ref[...] = acc[...].astype(o_ref.dtype)

    return _call(
        compute, name=name, grid=(M // tm, K // tko, nc),
        in_specs=_piece_specs(P, per, tm, tc, lambda i, j, k: i, lambda i, j, k: k)
        + [pl.BlockSpec((1, tko, tc), lambda i, j, k: (k // ncb, j, k % ncb))],
        out_specs=[pl.BlockSpec((tm, tko), lambda i, j, k: (i, j))],
        out_shape=[jax.ShapeDtypeStruct((M, K), out_dtype)],
        scratch=[pltpu.VMEM((tm, tko), F32)],
        operands=(*pieces, b3), sem=("parallel", "parallel", "arbitrary"), comm=comm)


def mm_tn(a, b, R, out_dtype, name):
    pieces = list(b) if isinstance(b, (list, tuple)) else [b]
    P = len(pieces)
    M, K = a.shape
    W = pieces[0].shape[1]
    n = P * W // R
    tm, tk, tn = _pick(M, (1408,) + M_TILES), _pick(K, MID_TILES), _pick(n, MID_TILES)
    nb, nm = n // tn, M // tm
    per = W // tn

    def compute(*refs):
        a_ref, b_refs, o_ref, acc = refs[0], refs[1:1 + P], refs[1 + P], refs[2 + P]
        m = pl.program_id(2)

        @pl.when(m == 0)
        def _():
            acc[...] = jnp.zeros_like(acc)

        def add(q):
            acc[...] += lax.dot_general(a_ref[...], b_refs[q][...], (((0,), (0,)), ((), ())),
                                        preferred_element_type=F32)

        _for_piece(P, per, pl.program_id(1), add)

        @pl.when(m == nm - 1)
        def _():
            o_ref[0] = acc[...].astype(o_ref.dtype)

    return _call(
        compute, name=name, grid=(K // tk, R * nb, nm),
        in_specs=[pl.BlockSpec((tm, tk), lambda i, j, m: (m, i))]
        + _piece_specs(P, per, tm, tn, lambda i, j, m: m, lambda i, j, m: j),
        out_specs=[pl.BlockSpec((1, tk, tn), lambda i, j, m: (j // nb, i, j % nb))],
        out_shape=[jax.ShapeDtypeStruct((R, K, n), out_dtype)],
        scratch=[pltpu.VMEM((tk, tn), F32)],
        operands=(a, *pieces), sem=("parallel", "parallel", "arbitrary"))[0]


def _seg_spec(D):
    return pl.BlockSpec((None, 1, D), lambda i: (jnp.minimum(i, 1), 0, 0))


def _seg3(a):
    return a.reshape(2, 1, a.shape[-1])


def _row_spec(w, col=0):
    return pl.BlockSpec((ROW_TILE, w), lambda i: (i, col))


def _acc_spec(r, w):
    return pl.BlockSpec((r, w), lambda i: (0, 0))


def _seg_accumulate(ref, i, val):
    ref[0:1, :] += jnp.where(i == 0, val, 0.0)
    ref[1:2, :] += jnp.where(i == 0, 0.0, val)


def rms_mod_fwd(x, g, shift2, scale2, name):
    T, D = x.shape

    def compute(x_ref, g_ref, sh_ref, sc_ref, h_ref):
        xv = x_ref[...]
        rstd = lax.rsqrt(jnp.mean(xv * xv, axis=-1, keepdims=True) + EPS)
        h = (xv * rstd * g_ref[...]) * (1.0 + sc_ref[...]) + sh_ref[...]
        h_ref[...] = h.astype(h_ref.dtype)

    return _call(compute, name=name, grid=(T // ROW_TILE,),
                 in_specs=[_row_spec(D), _acc_spec(1, D), _seg_spec(D), _seg_spec(D)],
                 out_specs=[_row_spec(D)], out_shape=[jax.ShapeDtypeStruct((T, D), BF16)],
                 operands=(x, g, _seg3(shift2), _seg3(scale2)), sem=("parallel",))[0]


def rms_mod_bwd(x, g, scale2, dh, dres, name, below=None):
    T, D = x.shape

    def compute(x_ref, g_ref, sc_ref, dh_ref, dres_ref, *rest):
        if below is None:
            dx_ref, dg_ref, dsh_ref, dsc_ref = rest
        else:
            y_ref, gate_ref, dx_ref, dg_ref, dsh_ref, dsc_ref, dy_ref, dgate_ref = rest
        i = pl.program_id(0)

        @pl.when(i == 0)
        def _():
            dg_ref[...] = jnp.zeros_like(dg_ref)
            dsh_ref[...] = jnp.zeros_like(dsh_ref)
            dsc_ref[...] = jnp.zeros_like(dsc_ref)
            if below is not None:
                dgate_ref[...] = jnp.zeros_like(dgate_ref)

        xv = x_ref[...]
        dh = dh_ref[...].astype(F32)
        gv = g_ref[...]
        rstd = lax.rsqrt(jnp.mean(xv * xv, axis=-1, keepdims=True) + EPS)
        xh = xv * rstd
        u = dh * (1.0 + sc_ref[...])
        dg_ref[...] += jnp.sum(u * xh, axis=0, keepdims=True)
        _seg_accumulate(dsh_ref, i, jnp.sum(dh, axis=0, keepdims=True))
        _seg_accumulate(dsc_ref, i, jnp.sum(dh * xh * gv, axis=0, keepdims=True))
        dxh = u * gv
        dx = rstd * (dxh - xh * jnp.mean(dxh * xh, axis=-1, keepdims=True))
        dxo = dres_ref[...] + dx
        dx_ref[...] = dxo
        if below is not None:
            dy_ref[...] = (dxo * gate_ref[...]).astype(dy_ref.dtype)
            _seg_accumulate(dgate_ref, i, jnp.sum(dxo * y_ref[...].astype(F32), axis=0, keepdims=True))

    in_specs = [_row_spec(D), _acc_spec(1, D), _seg_spec(D), _row_spec(D), _row_spec(D)]
    out_specs = [_row_spec(D), _acc_spec(1, D), _acc_spec(2, D), _acc_spec(2, D)]
    out_shape = [jax.ShapeDtypeStruct((T, D), F32), jax.ShapeDtypeStruct((1, D), F32),
                 jax.ShapeDtypeStruct((2, D), F32), jax.ShapeDtypeStruct((2, D), F32)]
    operands = (x, g, _seg3(scale2), dh, dres)
    if below is not None:
        in_specs += [_row_spec(D), _seg_spec(D)]
        out_specs += [_row_spec(D), _acc_spec(2, D)]
        out_shape += [jax.ShapeDtypeStruct((T, D), BF16), jax.ShapeDtypeStruct((2, D), F32)]
        operands += (below[0], _seg3(below[1]))
    return _call(compute, name=name, grid=(T // ROW_TILE,), in_specs=in_specs, out_specs=out_specs,
                 out_shape=out_shape, operands=operands, sem=("arbitrary",))


def gate_bwd(dxo, y, gate2, name):
    T, D = dxo.shape

    def compute(d_ref, y_ref, g_ref, dy_ref, dg_ref):
        i = pl.program_id(0)

        @pl.when(i == 0)
        def _():
            dg_ref[...] = jnp.zeros_like(dg_ref)

        d = d_ref[...]
        dy_ref[...] = (d * g_ref[...]).astype(dy_ref.dtype)
        _seg_accumulate(dg_ref, i, jnp.sum(d * y_ref[...].astype(F32), axis=0, keepdims=True))

    return _call(compute, name=name, grid=(T // ROW_TILE,),
                 in_specs=[_row_spec(D), _row_spec(D), _seg_spec(D)],
                 out_specs=[_row_spec(D), _acc_spec(2, D)],
                 out_shape=[jax.ShapeDtypeStruct((T, D), BF16), jax.ShapeDtypeStruct((2, D), F32)],
                 operands=(dxo, y, _seg3(gate2)), sem=("arbitrary",))


def final_loss(x, g, target, name):
    T, D = x.shape

    def compute(x_ref, g_ref, t_ref, dx_ref, dg_ref, loss_ref):
        i = pl.program_id(0)

        @pl.when(i == 0)
        def _():
            dg_ref[...] = jnp.zeros_like(dg_ref)
            loss_ref[...] = jnp.zeros_like(loss_ref)
            dx_ref[...] = jnp.zeros_like(dx_ref)

        @pl.when(i > 0)
        def _():
            xv = x_ref[...]
            gv = g_ref[...]
            rstd = lax.rsqrt(jnp.mean(xv * xv, axis=-1, keepdims=True) + EPS)
            xh = xv * rstd
            err = xh * gv - t_ref[...]
            loss_ref[...] += 0.5 * jnp.sum(jnp.mean(err * err, axis=-1, keepdims=True))
            dy = err * (1.0 / D)
            dg_ref[...] += jnp.sum(dy * xh, axis=0, keepdims=True)
            dxh = dy * gv
            dx_ref[...] = rstd * (dxh - xh * jnp.mean(dxh * xh, axis=-1, keepdims=True))

    return _call(compute, name=name, grid=(T // ROW_TILE,),
                 in_specs=[_row_spec(D), _acc_spec(1, D),
                           pl.BlockSpec((ROW_TILE, D), lambda i: (jnp.maximum(i - 1, 0), 0))],
                 out_specs=[_row_spec(D), _acc_spec(1, D), _acc_spec(8, 128)],
                 out_shape=[jax.ShapeDtypeStruct((T, D), F32), jax.ShapeDtypeStruct((1, D), F32),
                            jax.ShapeDtypeStruct((8, 128), F32)],
                 operands=(x, g, target), sem=("arbitrary",))


def _swap32(v):
    lane = lax.broadcasted_iota(jnp.int32, v.shape, 1)
    return jnp.where((lane & 63) < 32, pltpu.roll(v, 96, 1), pltpu.roll(v, 32, 1))


def _rope(v, cos, sin):
    return v * cos + _swap32(v) * sin


def _rope_t(d, cos, sin):
    return d * cos + _swap32(d * sin)


def _chunk_of(step, n_chunks, n_ctx, rev):
    if not rev:
        return step
    return jnp.where(step < n_ctx, n_ctx - 1 - step, n_chunks + n_ctx - 1 - step)


def _dot_t0(a, b):
    return lax.dot_general(a, b, (((0,), (0,)), ((), ())), preferred_element_type=F32)


def _dot_t1(a, b):
    return lax.dot_general(a, b, (((1,), (1,)), ((), ())), preferred_element_type=F32)


def _dot(a, b):
    return jnp.dot(a, b, preferred_element_type=F32)


def ret_fwd(p, cos, sin, tabs, n_heads, n_ctx, rev, name, comm=None):
    T = p.shape[0]
    C = RET_CHUNK
    H = n_heads
    NC = T // C
    scale = C ** -0.5

    def compute(q_ref, k_ref, v_ref, cos_ref, sin_ref, dm_ref, qd_ref, kd_ref, cd_ref, o_ref, s_ref, S):
        t = pl.program_id(0)

        @pl.when(t == 0)
        def _():
            S[...] = jnp.zeros_like(S)

        cs, sn = cos_ref[...], sin_ref[...]
        for h in range(H):
            sl = slice(h * C, (h + 1) * C)
            q = _rope(q_ref[:, sl], cs, sn)
            k = _rope(k_ref[:, sl], cs, sn) * scale
            qb, kb, vb = q.astype(BF16), k.astype(BF16), v_ref[:, sl].astype(BF16)
            A = _dot_t1(qb, kb) * dm_ref[h]
            s_in = S[h]
            s_ref[h, 0] = s_in
            o_ref[:, sl] = _dot(A.astype(BF16), vb) + _dot(qb, s_in.astype(BF16)) * qd_ref[h]
            S[h] = s_in * cd_ref[h] + _dot_t0((k * kd_ref[h]).astype(BF16), vb)

    cmap = lambda t: _chunk_of(t, NC, n_ctx, rev)
    blk = lambda col: pl.BlockSpec((C, H * C), lambda t: (cmap(t), col))
    tab = pl.BlockSpec((C, C), lambda t: (cmap(t), 0))
    htab = lambda r: pl.BlockSpec((H, r, C), lambda t: (0, 0, 0))
    return _call(
        compute, name=name, grid=(NC,),
        in_specs=[blk(0), blk(1), blk(2), tab, tab, htab(C), htab(C), htab(C), htab(1)],
        out_specs=[blk(0), pl.BlockSpec((H, 1, C, C), lambda t: (0, cmap(t), 0, 0))],
        out_shape=[jax.ShapeDtypeStruct((T, H * C), F32), jax.ShapeDtypeStruct((H, NC, C, C), F32)],
        scratch=[pltpu.VMEM((H, C, C), F32)],
        operands=(p, p, p, cos, sin, tabs["dm"], tabs["qd"], tabs["kd"], tabs["cd"]),
        sem=("arbitrary",), comm=comm)


def ret_bwd(p, cos, sin, tabs, do, s_saved, n_heads, n_ctx, rev, name, comm=None):
    T = p.shape[0]
    C = RET_CHUNK
    H = n_heads
    NC = T // C
    scale = C ** -0.5

    def compute(q_ref, k_ref, v_ref, cos_ref, sin_ref, dm_ref, qd_ref, kd_ref, cd_ref, em_ref, eq_ref,
                ek_ref, do_ref, s_ref, dq_ref, dk_ref, dv_ref, dlg_ref, dS):
        t = pl.program_id(0)

        @pl.when(t == 0)
        def _():
            dS[...] = jnp.zeros_like(dS)
            dlg_ref[...] = jnp.zeros_like(dlg_ref)

        cs, sn = cos_ref[...], sin_ref[...]
        for h in range(H):
            sl = slice(h * C, (h + 1) * C)
            q = _rope(q_ref[:, sl], cs, sn)
            k = _rope(k_ref[:, sl], cs, sn) * scale
            qb, kb, vb = q.astype(BF16), k.astype(BF16), v_ref[:, sl].astype(BF16)
            dmv, qdv, kdv, cdv = dm_ref[h], qd_ref[h], kd_ref[h], cd_ref[h]
            A = _dot_t1(qb, kb) * dmv
            s_in = s_ref[h, 0]
            sb = s_in.astype(BF16)
            ds_out = dS[h]
            dsb = ds_out.astype(BF16)
            dov = do_ref[:, sl]
            dob = dov.astype(BF16)
            dA = _dot_t1(dob, vb)
            dPb = (dA * dmv).astype(BF16)
            doq = dov * qdv
            doqb = doq.astype(BF16)
            kk = k * kdv
            vds = _dot_t1(vb, dsb)
            dq_ref[:, sl] = _dot(dPb, kb) + _dot_t1(doqb, sb)
            dk_ref[:, sl] = _dot_t0(dPb, qb) + vds * kdv
            dv_ref[:, sl] = _dot_t0(A.astype(BF16), dob) + _dot(kk.astype(BF16), dsb)
            dS[h] = ds_out * cdv + _dot_t0(qb, doqb)
            o2 = _dot(qb, sb)
            part = (jnp.sum(dA * A * em_ref[...], axis=0, keepdims=True)
                    + jnp.sum(eq_ref[...] * doq * o2, axis=0, keepdims=True)
                    + jnp.sum(ek_ref[...] * kk * vds, axis=0, keepdims=True)
                    + float(C) * cdv * jnp.sum(s_in * ds_out, axis=0, keepdims=True))
            dlg_ref[h, 0:1, :] += part

    cmap = lambda t: _chunk_of(NC - 1 - t, NC, n_ctx, rev)
    blk = lambda col: pl.BlockSpec((C, H * C), lambda t: (cmap(t), col))
    tab = pl.BlockSpec((C, C), lambda t: (cmap(t), 0))
    const = pl.BlockSpec((C, C), lambda t: (0, 0))
    htab = lambda r: pl.BlockSpec((H, r, C), lambda t: (0, 0, 0))
    return _call(
        compute, name=name, grid=(NC,),
        in_specs=[blk(0), blk(1), blk(2), tab, tab, htab(C), htab(C), htab(C), htab(1), const, const,
                  const, blk(0), pl.BlockSpec((H, 1, C, C), lambda t: (0, cmap(t), 0, 0))],
        out_specs=[blk(0), blk(0), blk(0), pl.BlockSpec((H, 8, C), lambda t: (0, 0, 0))],
        out_shape=[jax.ShapeDtypeStruct((T, H * C), F32)] * 3 + [jax.ShapeDtypeStruct((H, 8, C), F32)],
        scratch=[pltpu.VMEM((H, C, C), F32)],
        operands=(p, p, p, cos, sin, tabs["dm"], tabs["qd"], tabs["kd"], tabs["cd"], tabs["em"], tabs["eq"],
                  tabs["ek"], do, s_saved),
        sem=("arbitrary",), comm=comm)


def ret_out_fwd(o_f, o_b, p, n_heads, name):
    T, RW = o_f.shape
    C = RET_CHUNK

    def compute(of_ref, ob_ref, g_ref, out_ref):
        for h in range(n_heads):
            sl = slice(h * C, (h + 1) * C)
            o = of_ref[:, sl] + ob_ref[:, sl]
            r = o * lax.rsqrt(jnp.mean(o * o, axis=-1, keepdims=True) + EPS)
            g = g_ref[:, sl]
            out_ref[:, sl] = (g * _sigmoid(g) * r).astype(out_ref.dtype)

    return _call(compute, name=name, grid=(T // ROW_TILE,),
                 in_specs=[_row_spec(RW), _row_spec(RW), _row_spec(RW, 3)],
                 out_specs=[_row_spec(RW)], out_shape=[jax.ShapeDtypeStruct((T, RW), BF16)],
                 operands=(o_f, o_b, p), sem=("parallel",))[0]


def ret_out_bwd(o_f, o_b, p, dmix, n_heads, name):
    T, RW = o_f.shape
    C = RET_CHUNK

    def compute(of_ref, ob_ref, g_ref, d_ref, do_ref, dg_ref):
        for h in range(n_heads):
            sl = slice(h * C, (h + 1) * C)
            o = of_ref[:, sl] + ob_ref[:, sl]
            rstd = lax.rsqrt(jnp.mean(o * o, axis=-1, keepdims=True) + EPS)
            r = o * rstd
            g = g_ref[:, sl]
            sg = _sigmoid(g)
            d = d_ref[:, sl].astype(F32)
            dg_ref[:, sl] = (d * r * sg * (1.0 + g * (1.0 - sg))).astype(dg_ref.dtype)
            dr = d * g * sg
            do_ref[:, sl] = rstd * (dr - r * jnp.mean(dr * r, axis=-1, keepdims=True))

    return _call(compute, name=name, grid=(T // ROW_TILE,),
                 in_specs=[_row_spec(RW), _row_spec(RW), _row_spec(RW, 3), _row_spec(RW, 0)],
                 out_specs=[_row_spec(RW), _row_spec(RW)],
                 out_shape=[jax.ShapeDtypeStruct((T, RW), F32), jax.ShapeDtypeStruct((T, RW), BF16)],
                 operands=(o_f, o_b, p, dmix), sem=("parallel",))


def ret_qkv_grad(dqf, dqb, dkf, dkb, dvf, dvb, cos, sin, n_heads, name):
    T, RW = dqf.shape
    C = RET_CHUNK
    scale = C ** -0.5

    def compute(qf, qb, kf, kb, vf, vb, cos_ref, sin_ref, out_ref):
        cs, sn = cos_ref[...], sin_ref[...]
        for h in range(n_heads):
            sl = slice(h * C, (h + 1) * C)
            to = lambda block: slice(block * RW + h * C, block * RW + (h + 1) * C)
            out_ref[:, to(0)] = _rope_t(qf[:, sl] + qb[:, sl], cs, sn).astype(out_ref.dtype)
            out_ref[:, to(1)] = _rope_t((kf[:, sl] + kb[:, sl]) * scale, cs, sn).astype(out_ref.dtype)
            out_ref[:, to(2)] = (vf[:, sl] + vb[:, sl]).astype(out_ref.dtype)

    return _call(compute, name=name, grid=(T // ROW_TILE,),
                 in_specs=[_row_spec(RW)] * 6 + [_row_spec(C), _row_spec(C)],
                 out_specs=[_row_spec(3 * RW)], out_shape=[jax.ShapeDtypeStruct((T, 3 * RW), BF16)],
                 operands=(dqf, dqb, dkf, dkb, dvf, dvb, cos, sin), sem=("parallel",))[0]


def _halo_specs(width, col, halo, n_rows):
    per = ROW_TILE // halo
    last = n_rows // halo - 1
    return [pl.BlockSpec((halo, width), lambda i: (jnp.maximum(i * per - 1, 0), col)),
            pl.BlockSpec((ROW_TILE, width), lambda i: (i, col)),
            pl.BlockSpec((halo, width), lambda i: (jnp.minimum((i + 1) * per, last), col))]


def _halo_valid(i, n_tiles):
    return i >= 2, jnp.logical_and(i >= 1, i <= n_tiles - 2)


def _shifted_copies(S):
    base = S[0]
    rows = base.shape[0]
    for s in range(1, 8):
        S[s] = pltpu.roll(base, rows - s, 0)


def _shifted_rows(S, start, n, cols=slice(None)):
    s = start % 8
    return S[s, start - s:start - s + n, cols]


def _fma_groups(acc, w_row, window):
    w8 = jnp.broadcast_to(w_row, acc[0].shape)
    return [a + w8 * window(g) for g, a in enumerate(acc)]


def _fold8(v):
    parts = [v[r:r + 8] for r in range(0, v.shape[0], 8)]
    while len(parts) > 1:
        parts = [parts[k] + parts[k + 1] for k in range(0, len(parts) - 1, 2)] + (parts[-1:] if len(parts) % 2 else [])
    return parts[0]


def conv_fwd(p, w, bias, ln_g, ln_b, name, comm=None):
    T = p.shape[0]
    CW = w.shape[1]
    NT = T // ROW_TILE
    HL = CONV_HALO
    PAD = CONV_K // 2

    def compute(ap, ac, an, bp, bc, bn, w_ref, b_ref, g_ref, be_ref, u2_ref, out_ref, US):
        i = pl.program_id(0)
        vp, vn = _halo_valid(i, NT)
        US[0, 0:HL, :] = jnp.where(vp, ap[...] * _sigmoid(bp[...]), 0.0)
        US[0, HL:HL + ROW_TILE, :] = ac[...] * _sigmoid(bc[...])
        US[0, HL + ROW_TILE:, :] = jnp.where(vn, an[...] * _sigmoid(bn[...]), 0.0)
        _shifted_copies(US)
        for r0 in range(0, ROW_TILE, CONV_ROWS):
            acc = [jnp.zeros((8, CW), F32) + b_ref[...]] * (CONV_ROWS // 8)
            for j in range(CONV_K):
                acc = _fma_groups(acc, w_ref[j:j + 1, :],
                                  lambda g: _shifted_rows(US, HL - PAD + j + r0 + 8 * g, 8))
            acc = jnp.concatenate(acc, axis=0)
            u2_ref[r0:r0 + CONV_ROWS, :] = acc
            mu = jnp.mean(acc, axis=-1, keepdims=True)
            xc = acc - mu
            rstd = lax.rsqrt(jnp.mean(xc * xc, axis=-1, keepdims=True) + EPS)
            ln = xc * rstd * g_ref[...] + be_ref[...]
            out_ref[r0:r0 + CONV_ROWS, :] = (ln * _sigmoid(ln)).astype(out_ref.dtype)

    vec = _acc_spec(1, CW)
    return _call(compute, name=name, grid=(NT,),
                 in_specs=_halo_specs(CW, 4, HL, T) + _halo_specs(CW, 5, HL, T) + [_acc_spec(CONV_K, CW), vec, vec, vec],
                 out_specs=[_row_spec(CW), _row_spec(CW)],
                 out_shape=[jax.ShapeDtypeStruct((T, CW), F32), jax.ShapeDtypeStruct((T, CW), BF16)],
                 scratch=[pltpu.VMEM((8, ROW_TILE + 2 * HL, CW), F32)],
                 operands=(p, p, p, p, p, p, w, bias, ln_g, ln_b), sem=("parallel",), comm=comm)


def conv_bwd_ln(u2, dmix, ln_g, ln_b, name):
    T, CW = u2.shape

    def compute(u_ref, d_ref, g_ref, be_ref, du_ref, dg_ref, db_ref, dbias_ref):
        i = pl.program_id(0)

        @pl.when(i == 0)
        def _():
            dg_ref[...] = jnp.zeros_like(dg_ref)
            db_ref[...] = jnp.zeros_like(db_ref)
            dbias_ref[...] = jnp.zeros_like(dbias_ref)

        u = u_ref[...]
        gv = g_ref[...]
        mu = jnp.mean(u, axis=-1, keepdims=True)
        xc = u - mu
        rstd = lax.rsqrt(jnp.mean(xc * xc, axis=-1, keepdims=True) + EPS)
        xh = xc * rstd
        ln = xh * gv + be_ref[...]
        sg = _sigmoid(ln)
        dln = d_ref[...].astype(F32) * sg * (1.0 + ln * (1.0 - sg))
        dg_ref[...] += jnp.sum(dln * xh, axis=0, keepdims=True)
        db_ref[...] += jnp.sum(dln, axis=0, keepdims=True)
        dxh = dln * gv
        du = rstd * (dxh - jnp.mean(dxh, axis=-1, keepdims=True)
                     - xh * jnp.mean(dxh * xh, axis=-1, keepdims=True))
        du_ref[...] = du
        dbias_ref[...] += jnp.sum(du, axis=0, keepdims=True)

    vec = _acc_spec(1, CW)
    return _call(compute, name=name, grid=(T // ROW_TILE,),
                 in_specs=[_row_spec(CW), _row_spec(CW, 1), vec, vec],
                 out_specs=[_row_spec(CW), vec, vec, vec],
                 out_shape=[jax.ShapeDtypeStruct((T, CW), F32)] + [jax.ShapeDtypeStruct((1, CW), F32)] * 3,
                 operands=(u2, dmix, ln_g, ln_b), sem=("arbitrary",))


def conv_bwd_taps(p, du2, w, lead, name):
    T = p.shape[0]
    CW = w.shape[1]
    RW = lead.shape[1]
    NT = T // ROW_TILE
    HL = CONV_HALO
    PAD = CONV_K // 2

    def compute(ap, ac, an, bp, bc, bn, dp, dc, dn, w_ref, lead_ref, out_ref, dw_ref, US, DUS):
        i = pl.program_id(0)

        @pl.when(i == 0)
        def _():
            dw_ref[...] = jnp.zeros_like(dw_ref)

        out_ref[:, 0:RW] = lead_ref[...]

        vp, vn = _halo_valid(i, NT)
        US[0, 0:HL, :] = jnp.where(vp, ap[...] * _sigmoid(bp[...]), 0.0)
        US[0, HL:HL + ROW_TILE, :] = ac[...] * _sigmoid(bc[...])
        US[0, HL + ROW_TILE:, :] = jnp.where(vn, an[...] * _sigmoid(bn[...]), 0.0)
        DUS[0, 0:HL, :] = jnp.where(vp, dp[...], 0.0)
        DUS[0, HL:HL + ROW_TILE, :] = dc[...]
        DUS[0, HL + ROW_TILE:, :] = jnp.where(vn, dn[...], 0.0)
        _shifted_copies(US)
        _shifted_copies(DUS)
        for r0 in range(0, ROW_TILE, CONV_ROWS):
            du = [jnp.zeros((8, CW), F32)] * (CONV_ROWS // 8)
            for j in range(CONV_K):
                du = _fma_groups(du, w_ref[j:j + 1, :],
                                 lambda g: _shifted_rows(DUS, HL + PAD - j + r0 + 8 * g, 8))
            du = jnp.concatenate(du, axis=0)
            a = ac[r0:r0 + CONV_ROWS, :]
            sg = _sigmoid(bc[r0:r0 + CONV_ROWS, :])
            out_ref[r0:r0 + CONV_ROWS, RW:RW + CW] = (du * sg).astype(out_ref.dtype)
            out_ref[r0:r0 + CONV_ROWS, RW + CW:] = (du * a * sg * (1.0 - sg)).astype(out_ref.dtype)
        for c0 in range(0, CW, 128):
            cols = slice(c0, c0 + 128)
            accs = [jnp.zeros((8, 128), F32)] * CONV_K
            for r0 in range(0, ROW_TILE, CONV_ROWS):
                d = DUS[0, HL + r0:HL + r0 + CONV_ROWS, cols]
                accs = [acc + _fold8(d * _shifted_rows(US, HL - PAD + j + r0, CONV_ROWS, cols))
                        for j, acc in enumerate(accs)]
            for j in range(CONV_K):
                dw_ref[j:j + 1, cols] += jnp.sum(accs[j], axis=0, keepdims=True)

    return _call(compute, name=name, grid=(NT,),
                 in_specs=(_halo_specs(CW, 4, HL, T) + _halo_specs(CW, 5, HL, T) + _halo_specs(CW, 0, HL, T)
                           + [_acc_spec(CONV_K, CW), _row_spec(RW)]),
                 out_specs=[_row_spec(RW + 2 * CW), _acc_spec(CONV_K, CW)],
                 out_shape=[jax.ShapeDtypeStruct((T, RW + 2 * CW), BF16), jax.ShapeDtypeStruct((CONV_K, CW), F32)],
                 scratch=[pltpu.VMEM((8, ROW_TILE + 2 * HL, CW), F32)] * 2,
                 operands=(p, p, p, p, p, p, du2, du2, du2, w, lead), sem=("arbitrary",))


def _ffn_halo_specs(tc, col0, n_rows):
    per = ROW_TILE // FFN_HALO
    last = n_rows // FFN_HALO - 1
    return [pl.BlockSpec((FFN_HALO, tc), lambda cb, i: (jnp.maximum(i * per - 1, 0), col0 + cb)),
            pl.BlockSpec((ROW_TILE, tc), lambda cb, i: (i, col0 + cb)),
            pl.BlockSpec((FFN_HALO, tc), lambda cb, i: (jnp.minimum((i + 1) * per, last), col0 + cb))]


def _ffn_fill(S, prev, cur, nxt, i, n_tiles):
    vp, vn = _halo_valid(i, n_tiles)
    HL = FFN_HALO
    S[1, 0:HL, :] = jnp.where(vp, prev[...].astype(F32), 0.0)
    S[1, HL:HL + ROW_TILE, :] = cur[...].astype(F32)
    S[1, HL + ROW_TILE:, :] = jnp.where(vn, nxt[...].astype(F32), 0.0)
    base = S[1]
    rows = base.shape[0]
    is_lat = i >= 1
    col = lax.broadcasted_iota(jnp.int32, base.shape, 0) & (GRID_W - 1)
    S[0] = jnp.where(jnp.logical_and(is_lat, col == 0), 0.0, pltpu.roll(base, 1, 0))
    S[2] = jnp.where(jnp.logical_and(is_lat, col == GRID_W - 1), 0.0, pltpu.roll(base, rows - 1, 0))


def _ffn_acc_rows(tc):
    return max(8, min(64, (32 // (tc // 128)) * 8))


def _ffn_row_factor(i, di):
    return 1.0 if di == 1 else jnp.where(i >= 1, 1.0, 0.0)


def ffn_conv_fwd(up, w9, bias, name):
    T = up.shape[0]
    DFF = w9.shape[1]
    tc = _pick(DFF, FFN_COL_TILES)
    ncb = DFF // tc
    NT = T // ROW_TILE
    HL = FFN_HALO

    def compute(gp, gc, gn, val_ref, w_ref, b_ref, cg_ref, act_ref, G):
        i = pl.program_id(1)
        _ffn_fill(G, gp, gc, gn, i, NT)
        rows = _ffn_acc_rows(tc)
        for r0 in range(0, ROW_TILE, rows):
            acc = [jnp.zeros((8, tc), F32) + b_ref[...]] * (rows // 8)
            for di in range(3):
                for dj in range(3):
                    wt = w_ref[3 * di + dj:3 * di + dj + 1, :] * _ffn_row_factor(i, di)
                    lo = HL + r0 + (di - 1) * GRID_W
                    acc = _fma_groups(acc, wt, lambda g: G[dj, lo + 8 * g:lo + 8 * g + 8, :])
            acc = jnp.concatenate(acc, axis=0)
            cg_ref[r0:r0 + rows, :] = acc.astype(cg_ref.dtype)
            val = val_ref[r0:r0 + rows, :].astype(F32)
            act_ref[r0:r0 + rows, :] = (acc * _sigmoid(acc) * val).astype(act_ref.dtype)

    tile = pl.BlockSpec((ROW_TILE, tc), lambda cb, i: (i, cb))
    return _call(compute, name=name, grid=(ncb, NT),
                 in_specs=_ffn_halo_specs(tc, 0, T) + [pl.BlockSpec((ROW_TILE, tc), lambda cb, i: (i, ncb + cb)),
                                                       pl.BlockSpec((9, tc), lambda cb, i: (0, cb)),
                                                       pl.BlockSpec((1, tc), lambda cb, i: (0, cb))],
                 out_specs=[tile, tile],
                 out_shape=[jax.ShapeDtypeStruct((T, DFF), BF16), jax.ShapeDtypeStruct((T, DFF), BF16)],
                 scratch=[pltpu.VMEM((3, ROW_TILE + 2 * HL, tc), F32)],
                 operands=(up, up, up, up, w9, bias), sem=("parallel", "parallel"))


def ffn_conv_bwd_act(cg, up, dact, name):
    T, DFF = cg.shape
    tc = _pick(DFF, FFN_COL_TILES)
    ncb = DFF // tc

    def compute(cg_ref, val_ref, d_ref, dcg_ref, dval_ref, db_ref):
        i = pl.program_id(1)

        @pl.when(i == 0)
        def _():
            db_ref[...] = jnp.zeros_like(db_ref)

        c = cg_ref[...].astype(F32)
        sg = _sigmoid(c)
        d = d_ref[...].astype(F32)
        dval_ref[...] = (d * c * sg).astype(dval_ref.dtype)
        dcg = d * val_ref[...].astype(F32) * sg * (1.0 + c * (1.0 - sg))
        dcg_ref[...] = dcg.astype(dcg_ref.dtype)
        db_ref[...] += jnp.sum(dcg, axis=0, keepdims=True)

    tile = pl.BlockSpec((ROW_TILE, tc), lambda cb, i: (i, cb))
    return _call(compute, name=name, grid=(ncb, T // ROW_TILE),
                 in_specs=[tile, pl.BlockSpec((ROW_TILE, tc), lambda cb, i: (i, ncb + cb)), tile],
                 out_specs=[tile, tile, pl.BlockSpec((1, tc), lambda cb, i: (0, cb))],
                 out_shape=[jax.ShapeDtypeStruct((T, DFF), BF16), jax.ShapeDtypeStruct((T, DFF), BF16),
                            jax.ShapeDtypeStruct((1, DFF), F32)],
                 operands=(cg, up, dact), sem=("parallel", "arbitrary"))


def ffn_conv_bwd_taps(up, dcg, w9, name, comm=None):
    T, DFF = dcg.shape
    tc = _pick(DFF, FFN_COL_TILES)
    ncb = DFF // tc
    NT = T // ROW_TILE
    HL = FFN_HALO

    def compute(gp, gc, gn, dp, dc, dn, w_ref, dgate_ref, dw_ref, G, DC):
        i = pl.program_id(1)

        @pl.when(i == 0)
        def _():
            dw_ref[...] = jnp.zeros_like(dw_ref)

        _ffn_fill(G, gp, gc, gn, i, NT)
        _ffn_fill(DC, dp, dc, dn, i, NT)
        rows = _ffn_acc_rows(tc)
        for r0 in range(0, ROW_TILE, rows):
            dg = [jnp.zeros((8, tc), F32)] * (rows // 8)
            for di in range(3):
                for dj in range(3):
                    wt = w_ref[3 * di + dj:3 * di + dj + 1, :] * _ffn_row_factor(i, di)
                    lo = HL + r0 - (di - 1) * GRID_W
                    dg = _fma_groups(dg, wt, lambda g: DC[2 - dj, lo + 8 * g:lo + 8 * g + 8, :])
            dgate_ref[r0:r0 + rows, :] = jnp.concatenate(dg, axis=0).astype(dgate_ref.dtype)
        for c0 in range(0, tc, 128):
            cols = slice(c0, c0 + 128)
            accs = [jnp.zeros((8, 128), F32)] * 9
            for r0 in range(0, ROW_TILE, FFN_TAP_ROWS):
                d = DC[1, HL + r0:HL + r0 + FFN_TAP_ROWS, cols]
                for di in range(3):
                    lo = HL + r0 + (di - 1) * GRID_W
                    for dj in range(3):
                        accs[3 * di + dj] = accs[3 * di + dj] + _fold8(d * G[dj, lo:lo + FFN_TAP_ROWS, cols])
            for di in range(3):
                for dj in range(3):
                    t = 3 * di + dj
                    dw_ref[t:t + 1, cols] += _ffn_row_factor(i, di) * jnp.sum(accs[t], axis=0, keepdims=True)

    tile = pl.BlockSpec((ROW_TILE, tc), lambda cb, i: (i, cb))
    return _call(compute, name=name, grid=(ncb, NT),
                 in_specs=_ffn_halo_specs(tc, 0, T) + _ffn_halo_specs(tc, 0, T) + [pl.BlockSpec((9, tc), lambda cb, i: (0, cb))],
                 out_specs=[tile, pl.BlockSpec((9, tc), lambda cb, i: (0, cb))],
                 out_shape=[jax.ShapeDtypeStruct((T, DFF), BF16), jax.ShapeDtypeStruct((9, DFF), F32)],
                 scratch=[pltpu.VMEM((3, ROW_TILE + 2 * HL, tc), F32)] * 2,
                 operands=(up, up, up, dcg, dcg, dcg, w9), sem=("parallel", "arbitrary"), comm=comm)


def _adamw_update(g, w_ref, m_ref, v_ref, g_ref, d_ref, nm_ref, nv_ref):
    c1 = 1.0 - ADAM_B1 ** ADAM_STEP
    c2 = 1.0 - ADAM_B2 ** ADAM_STEP
    nm = ADAM_B1 * m_ref[...] + (1.0 - ADAM_B1) * g
    nv = ADAM_B2 * v_ref[...] + (1.0 - ADAM_B2) * (g * g)
    g_ref[...] = g
    nm_ref[...] = nm
    nv_ref[...] = nv
    d_ref[...] = -ADAM_LR * ((nm / c1) / (jnp.sqrt(nv / c2) + ADAM_EPS) + ADAM_WD * w_ref[...])


def _sum_parts(p_ref, P):
    g = p_ref[0].astype(F32)
    for k in range(1, P):
        g = g + p_ref[k].astype(F32)
    return g


def adamw(parts, w, m, v, name):
    P, R, C = parts.shape
    fits = lambda t: 2 * (P + 7) * t * C * 4 <= ADAM_VMEM_BYTES
    tr = R if fits(R) else _pick(R, [t for t in (1024, 512, 256, 128, 64, 32, 16, 8) if fits(t)])

    def compute(p_ref, w_ref, m_ref, v_ref, g_ref, d_ref, nm_ref, nv_ref):
        _adamw_update(_sum_parts(p_ref, P), w_ref, m_ref, v_ref, g_ref, d_ref, nm_ref, nv_ref)

    tile = pl.BlockSpec((tr, C), lambda i: (i, 0))
    return _call(compute, name=name, grid=(R // tr,),
                 in_specs=[pl.BlockSpec((P, tr, C), lambda i: (0, i, 0)), tile, tile, tile],
                 out_specs=[tile] * 4, out_shape=[jax.ShapeDtypeStruct((R, C), F32)] * 4,
                 operands=(parts, w, m, v), sem=("parallel",))


def adamw_layers(parts_l, w, m, v, name, comm=None):
    L = len(parts_l)
    P, R, C = parts_l[0].shape
    psize = parts_l[0].dtype.itemsize
    fits = lambda t: 2 * t * C * (L * P * psize + 7 * 4) <= ADAM_VMEM_BYTES
    tr = _pick(R, [t for t in (1024, 512, 256, 128, 64, 32, 16, 8) if fits(t)])

    def compute(*refs):
        p_refs, (w_ref, m_ref, v_ref), outs = refs[:L], refs[L:L + 3], refs[L + 3:]
        for l in range(L):
            @pl.when(pl.program_id(0) == l)
            def _(l=l):
                _adamw_update(_sum_parts(p_refs[l], P), w_ref, m_ref, v_ref, *outs)

    tile = pl.BlockSpec((None, tr, C), lambda l, i: (l, i, 0))
    part = lambda k: pl.BlockSpec((P, tr, C), lambda l, i: (0, jnp.where(l == k, i, 0), 0))
    return _call(compute, name=name, grid=(L, R // tr),
                 in_specs=[part(k) for k in range(L)] + [tile] * 3,
                 out_specs=[tile] * 4, out_shape=[jax.ShapeDtypeStruct((L, R, C), F32)] * 4,
                 operands=(*parts_l, w, m, v), sem=("arbitrary", "arbitrary"), comm=comm)


def _rope_tables(seq, ctx):
    t = jnp.arange(seq)
    quarter = RET_CHUNK // 4
    inv_freq = 1.0 / (ROPE_THETA ** (jnp.arange(0, quarter, dtype=F32) / quarter))
    ang_r = (t // GRID_W).astype(F32)[:, None] * inv_freq[None, :]
    ang_c = (t % GRID_W).astype(F32)[:, None] * inv_freq[None, :]
    cr, sr, cc, sc = jnp.cos(ang_r), jnp.sin(ang_r), jnp.cos(ang_c), jnp.sin(ang_c)
    cos = jnp.concatenate([cr, cr, cc, cc], axis=-1)
    sin = jnp.concatenate([-sr, sr, -sc, sc], axis=-1)
    cos = jnp.concatenate([jnp.ones((ctx, RET_CHUNK), F32), cos], axis=0)
    sin = jnp.concatenate([jnp.zeros((ctx, RET_CHUNK), F32), sin], axis=0)
    return cos, sin


def _decay_tables(decay_logit, rev):
    C = RET_CHUNK
    lg = jax.nn.log_sigmoid(decay_logit.astype(F32))
    idx = jnp.arange(C, dtype=F32)
    diff = idx[:, None] - idx[None, :]
    if rev:
        diff = -diff
        eq, ek = C - idx, idx
    else:
        eq, ek = idx + 1.0, C - 1.0 - idx
    keep = diff >= 0
    em = jnp.where(keep, diff, 0.0)
    bc = lambda e: jnp.broadcast_to(e[:, None], (C, C))
    return {
        "dm": jnp.where(keep[None], jnp.exp(lg[:, None, None] * em[None]), 0.0),
        "qd": jnp.broadcast_to(jnp.exp(lg[:, None] * eq[None, :])[:, :, None], (lg.shape[0], C, C)),
        "kd": jnp.broadcast_to(jnp.exp(lg[:, None] * ek[None, :])[:, :, None], (lg.shape[0], C, C)),
        "cd": jnp.broadcast_to(jnp.exp(lg * C)[:, None, None], (lg.shape[0], 1, C)),
        "em": em, "eq": bc(eq), "ek": bc(ek),
    }


def _silu(z):
    return z * jax.nn.sigmoid(z)


def _dsilu(z):
    s = jax.nn.sigmoid(z)
    return s * (1.0 + z * (1.0 - s))


def kernel(x, c, ctx, c_ctx, w_mod, b_mod, norm1_g, norm2_g, w_in, ret_decay_f, ret_decay_b, conv_dw_w, conv_dw_b, conv_ln_g, conv_ln_b, w_out, ffn_w_up, ffn_dw_w, ffn_dw_b, ffn_w_down, final_norm_g, loss_target, m_c_ctx, m_w_mod, m_b_mod, m_norm1_g, m_norm2_g, m_w_in, m_ret_decay_f, m_ret_decay_b, m_conv_dw_w, m_conv_dw_b, m_conv_ln_g, m_conv_ln_b, m_w_out, m_ffn_w_up, m_ffn_dw_w, m_ffn_dw_b, m_ffn_w_down, m_final_norm_g, v_c_ctx, v_w_mod, v_b_mod, v_norm1_g, v_norm2_g, v_w_in, v_ret_decay_f, v_ret_decay_b, v_conv_dw_w, v_conv_dw_b, v_conv_ln_g, v_conv_ln_b, v_w_out, v_ffn_w_up, v_ffn_dw_w, v_ffn_dw_b, v_ffn_w_down, v_final_norm_g):
    L, D, _ = w_mod.shape
    SEQ, CTX = x.shape[1], ctx.shape[1]
    T = SEQ + CTX
    RW = D // 2
    CW = D - RW
    H = RW // RET_CHUNK
    DFF = ffn_dw_b.shape[1]
    NMOD = b_mod.shape[1] // D
    n_ctx = CTX // RET_CHUNK
    assert CTX == ROW_TILE and RW == CW and SEQ % ROW_TILE == 0 and NMOD == 6
    me = _my_rank()
    wm_n = w_mod.shape[2]
    wo_k, wd_k = w_out.shape[1], ffn_w_down.shape[1]
    cw_n, fw_n = conv_dw_w.shape[2], ffn_dw_w.shape[3]

    w_mod_b = w_mod.astype(BF16)
    w_in_b, w_out_b, w_up_b, w_down_b = (a.astype(BF16) for a in (w_in, w_out, ffn_w_up, ffn_w_down))
    as_rows = lambda g: g.reshape(1, -1, D)
    g_in, g_out, g_up, g_down, g_cw, g_fw, g_c = run_comm(
        "gather", [w_in_b[0], w_out_b[0], w_up_b[0], w_down_b[0], conv_dw_w, ffn_dw_w, _silu(c)], "gather_first")
    w_in_l, w_up_l, w_out_l, w_down_l = [g_in], [g_up], [as_rows(g_out)], [as_rows(g_down)]
    conv_w_l = [jnp.moveaxis(g_cw[:, l], 0, 1).reshape(CONV_K, CW) for l in range(L)]
    ffn_w9_l = [jnp.moveaxis(g_fw[:, l], 0, 2).reshape(9, DFF) for l in range(L)]

    s_cond = jnp.concatenate([g_c.reshape(N_DEV, D), jnp.broadcast_to(_silu(c_ctx)[None], (N_DEV, D))], axis=0)
    s_cond_b = s_cond.astype(BF16)
    mod_shard = mm_nn(s_cond_b, w_mod_b, F32, "mod_fwd")[0]
    (g_mod,) = run_comm("gather", [mod_shard], "gather_mod")
    mod_all = jnp.transpose(g_mod.reshape(N_DEV, 2 * N_DEV, L, wm_n), (2, 1, 0, 3)).reshape(L, 2 * N_DEV, NMOD * D)
    mod_all = mod_all + b_mod[:, None, :]
    mod_lat = lax.dynamic_index_in_dim(mod_all, me, axis=1, keepdims=False)
    mod_ctx = mod_all[:, N_DEV]
    mod2 = jnp.stack([mod_ctx, mod_lat], axis=1).reshape(L, 2, NMOD, D)

    cos, sin = _rope_tables(SEQ, CTX)
    xs = jnp.concatenate([ctx[0], x[0]], axis=0)

    saved = []
    for l in range(L):
        nxt = l + 1 < L
        sh1, sc1, g1, sh2, sc2, g2 = (mod2[l, :, k] for k in range(NMOD))
        tf = _decay_tables(ret_decay_f[l], False)
        tb = _decay_tables(ret_decay_b[l], True)
        h = rms_mod_fwd(xs, norm1_g[l][None], sh1, sc1, "norm1_fwd")
        if nxt:
            (p,), (gi, go) = mm_nn(h, w_in_l[l], F32, "in_proj_g", Comm("gather", [w_in_b[l + 1], w_out_b[l + 1]]))
            w_in_l.append(gi)
            w_out_l.append(as_rows(go))
        else:
            (p,) = mm_nn(h, w_in_l[l], F32, "in_proj")
        o_f, s_f = ret_fwd(p, cos, sin, tf, H, n_ctx, False, "ret_fwd_f")
        o_b, s_b = ret_fwd(p, cos, sin, tb, H, n_ctx, True, "ret_fwd_b")
        mix_r = ret_out_fwd(o_f, o_b, p, H, "ret_out_fwd")
        u2, mix_c = conv_fwd(p, conv_w_l[l], conv_dw_b[l][None], conv_ln_g[l][None], conv_ln_b[l][None], "conv_fwd")
        mix = jnp.concatenate([mix_r, mix_c], axis=1)
        y1, x2 = mm_nn_res(mix, w_out_l[l], xs, g1, CTX, "out_proj")
        h2 = rms_mod_fwd(x2, norm2_g[l][None], sh2, sc2, "norm2_fwd")
        if nxt:
            (up,), (gu,) = mm_nn(h2, w_up_l[l], BF16, "ffn_up_g", Comm("gather", [w_up_b[l + 1]]))
            w_up_l.append(gu)
        else:
            (up,) = mm_nn(h2, w_up_l[l], BF16, "ffn_up")
        cg, act = ffn_conv_fwd(up, ffn_w9_l[l], ffn_dw_b[l][None], "ffn_conv_fwd")
        if nxt:
            (y2, x3), (gd,) = mm_nn_res(act, w_down_l[l], x2, g2, CTX, "ffn_down_g", Comm("gather", [w_down_b[l + 1]]))
            w_down_l.append(as_rows(gd))
        else:
            y2, x3 = mm_nn_res(act, w_down_l[l], x2, g2, CTX, "ffn_down")
        saved.append(dict(x1=xs, h=h, p=p, o_f=o_f, o_b=o_b, s_f=s_f, s_b=s_b, u2=u2, mix=mix, y1=y1, x2=x2,
                          h2=h2, up=up, cg=cg, act=act, y2=y2, tf=tf, tb=tb))
        xs = x3

    dxs, d_final_g, loss_part = final_loss(xs, final_norm_g[None], loss_target[0], "final_loss")
    loss = lax.psum(loss_part[0, 0], ("x", "y", "c"))

    landed = {n: [None] * L for n in ("w_in", "w_out", "ffn_w_up", "ffn_w_down")}
    small = {n: [None] * L for n in ("norm1_g", "norm2_g", "ret_decay_f", "ret_decay_b", "conv_dw_w", "conv_dw_b",
                                     "conv_ln_g", "conv_ln_b", "ffn_dw_w", "ffn_dw_b")}
    dmod2 = [None] * L
    g_in_prev = g_up_prev = None
    for l in reversed(range(L)):
        sv = saved[l]
        sh1, sc1, g1, sh2, sc2, g2 = (mod2[l, :, k] for k in range(NMOD))
        if l == L - 1:
            dy2, dg2 = gate_bwd(dxs, sv["y2"], g2, "res2_bwd")
        if g_in_prev is not None:
            (dact,), (landed["w_in"][l + 1],) = mm_nt(dy2, w_down_l[l], BF16, "ffn_down_dx_x", Comm("exchange", [g_in_prev]))
        else:
            (dact,) = mm_nt(dy2, w_down_l[l], BF16, "ffn_down_dx")
        g_down = mm_tn(sv["act"], dy2, 1, BF16, "ffn_down_dw").reshape(N_DEV, wd_k, D)
        dcg, dval, small["ffn_dw_b"][l] = ffn_conv_bwd_act(sv["cg"], sv["up"], dact, "ffn_conv_bwd_act")
        (dgate, small["ffn_dw_w"][l]), (landed["ffn_w_down"][l],) = ffn_conv_bwd_taps(
            sv["up"], dcg, ffn_w9_l[l], "ffn_conv_bwd_taps", Comm("exchange", [g_down]))
        dup = [dgate, dval]
        if g_up_prev is not None:
            (dh2,), (landed["ffn_w_up"][l + 1],) = mm_nt(dup, w_up_l[l], F32, "ffn_up_dx_x", Comm("exchange", [g_up_prev]))
        else:
            (dh2,) = mm_nt(dup, w_up_l[l], F32, "ffn_up_dx")
        g_up_prev = mm_tn(sv["h2"], dup, N_DEV, BF16, "ffn_up_dw")
        dx2, small["norm2_g"][l], dsh2, dsc2, dy1, dg1 = rms_mod_bwd(
            sv["x2"], norm2_g[l][None], sc2, dh2, dxs, "norm2_bwd", below=(sv["y1"], g1))
        (dmix,) = mm_nt(dy1, w_out_l[l], BF16, "out_proj_dx")
        g_out = mm_tn(sv["mix"], dy1, 1, BF16, "out_proj_dw").reshape(N_DEV, wo_k, D)
        do, dgt = ret_out_bwd(sv["o_f"], sv["o_b"], sv["p"], dmix, H, "ret_out_bwd")
        dqf, dkf, dvf, dlg_f = ret_bwd(sv["p"], cos, sin, sv["tf"], do, sv["s_f"], H, n_ctx, False, "ret_bwd_f")
        (dqb, dkb, dvb, dlg_b), (landed["w_out"][l],) = ret_bwd(
            sv["p"], cos, sin, sv["tb"], do, sv["s_b"], H, n_ctx, True, "ret_bwd_b", Comm("exchange", [g_out]))
        dqkv = ret_qkv_grad(dqf, dqb, dkf, dkb, dvf, dvb, cos, sin, H, "ret_qkv_grad")
        small["ret_decay_f"][l] = jnp.sum(dlg_f[:, 0, :], axis=-1) * jax.nn.sigmoid(-ret_decay_f[l])
        small["ret_decay_b"][l] = jnp.sum(dlg_b[:, 0, :], axis=-1) * jax.nn.sigmoid(-ret_decay_b[l])
        du2, small["conv_ln_g"][l], small["conv_ln_b"][l], small["conv_dw_b"][l] = conv_bwd_ln(
            sv["u2"], dmix, conv_ln_g[l][None], conv_ln_b[l][None], "conv_bwd_ln")
        dgab, small["conv_dw_w"][l] = conv_bwd_taps(sv["p"], du2, conv_w_l[l], dgt, "conv_bwd_taps")
        dp = [dqkv, dgab]
        if l == 0:
            (dh,), (landed["ffn_w_up"][0],) = mm_nt(dp, w_in_l[l], F32, "in_proj_dx_x", Comm("exchange", [g_up_prev]))
        else:
            (dh,) = mm_nt(dp, w_in_l[l], F32, "in_proj_dx")
        g_in_prev = mm_tn(sv["h"], dp, N_DEV, BF16, "in_proj_dw")
        dmod_l = [None, None, dg1, dsh2, dsc2, dg2]
        if l > 0:
            dxs, small["norm1_g"][l], dmod_l[0], dmod_l[1], dy2, dg2 = rms_mod_bwd(
                sv["x1"], norm1_g[l][None], sc1, dh, dx2, "norm1_bwd_res", below=(saved[l - 1]["y2"], mod2[l - 1, :, 5]))
        else:
            dxs, small["norm1_g"][l], dmod_l[0], dmod_l[1] = rms_mod_bwd(
                sv["x1"], norm1_g[l][None], sc1, dh, dx2, "norm1_bwd")
        dmod2[l] = jnp.concatenate(dmod_l, axis=1)

    grad_x = dxs[CTX:][None]

    dmod2 = jnp.stack(dmod2)
    (g_dmod,) = run_comm("gather", [dmod2], "gather_dmod")
    dmod_all = jnp.concatenate([jnp.moveaxis(g_dmod[:, :, 1], 0, 1), jnp.moveaxis(g_dmod[:, :, 0], 0, 1)], axis=1)
    dmod_sh = lax.dynamic_slice_in_dim(dmod_all, me * wm_n, wm_n, axis=2)
    dmod_sh = jnp.moveaxis(dmod_sh, 0, 1).reshape(2 * N_DEV, L * wm_n).astype(BF16)
    g_w_mod = mm_tn(s_cond_b, dmod_sh, L, F32, "mod_dw")
    (d_cond,) = mm_nt(dmod_sh, w_mod_b, F32, "mod_dx")
    g_c_ctx_part = jnp.sum(d_cond[N_DEV:], axis=0) * _dsilu(c_ctx)
    g_b_mod_part = dmod2[:, 0] + dmod2[:, 1]

    pad128 = lambda a: jnp.pad(a.reshape(-1), (0, (-a.size) % 128))
    rep_names = ["c_ctx", "b_mod", "norm1_g", "norm2_g", "ret_decay_f", "ret_decay_b", "conv_dw_b", "conv_ln_g",
                 "conv_ln_b", "ffn_dw_b", "final_norm_g"]
    given = dict(c_ctx=(c_ctx, m_c_ctx, v_c_ctx), b_mod=(b_mod, m_b_mod, v_b_mod),
                 norm1_g=(norm1_g, m_norm1_g, v_norm1_g), norm2_g=(norm2_g, m_norm2_g, v_norm2_g),
                 ret_decay_f=(ret_decay_f, m_ret_decay_f, v_ret_decay_f),
                 ret_decay_b=(ret_decay_b, m_ret_decay_b, v_ret_decay_b),
                 conv_dw_b=(conv_dw_b, m_conv_dw_b, v_conv_dw_b), conv_ln_g=(conv_ln_g, m_conv_ln_g, v_conv_ln_g),
                 conv_ln_b=(conv_ln_b, m_conv_ln_b, v_conv_ln_b), ffn_dw_b=(ffn_dw_b, m_ffn_dw_b, v_ffn_dw_b),
                 final_norm_g=(final_norm_g, m_final_norm_g, v_final_norm_g))
    rep_part = dict(c_ctx=g_c_ctx_part, b_mod=g_b_mod_part, final_norm_g=d_final_g)
    for nme in rep_names:
        if nme not in rep_part:
            rep_part[nme] = jnp.stack([a.reshape(-1) for a in small[nme]])
    rep_sizes = [((-given[nme][0].size) % 128) + given[nme][0].size for nme in rep_names]
    n_rep = sum(rep_sizes)
    cw_part = jnp.stack(small["conv_dw_w"])
    fw_part = jnp.stack(small["ffn_dw_w"])
    packed = jnp.concatenate([pad128(rep_part[nme]) for nme in rep_names] + [cw_part.reshape(-1), fw_part.reshape(-1)])
    (g_small,) = run_comm("gather", [packed.reshape(-1, 128)], "gather_small")
    g_small = g_small.reshape(N_DEV, -1)
    rep_w, rep_m, rep_v = (jnp.concatenate([pad128(given[nme][k]) for nme in rep_names]).reshape(-1, 128) for k in range(3))
    rep_out = adamw(g_small[:, :n_rep].reshape(N_DEV, -1, 128), rep_w, rep_m, rep_v, "adamw_small")
    res = {}
    off = 0
    for nme, sz in zip(rep_names, rep_sizes):
        shape = given[nme][0].shape
        res[nme] = [o.reshape(-1)[off:off + given[nme][0].size].reshape(shape) for o in rep_out]
        off += sz

    cw_all = g_small[:, n_rep:n_rep + cw_part.size].reshape(N_DEV, L * CONV_K, CW)
    cw_mine = lax.dynamic_slice_in_dim(cw_all, me * cw_n, cw_n, axis=2)
    res["conv_dw_w"] = [o.reshape(conv_dw_w.shape) for o in adamw(
        cw_mine, conv_dw_w.reshape(L * CONV_K, cw_n), m_conv_dw_w.reshape(L * CONV_K, cw_n),
        v_conv_dw_w.reshape(L * CONV_K, cw_n), "adamw_conv_w")]
    fw_all = g_small[:, n_rep + cw_part.size:].reshape(N_DEV, L * 9, DFF)
    fw_mine = lax.dynamic_slice_in_dim(fw_all, me * fw_n, fw_n, axis=2)
    res["ffn_dw_w"] = [o.reshape(ffn_dw_w.shape) for o in adamw(
        fw_mine, ffn_dw_w.reshape(L * 9, fw_n), m_ffn_dw_w.reshape(L * 9, fw_n),
        v_ffn_dw_w.reshape(L * 9, fw_n), "adamw_ffn_w")]

    res["w_mod"] = [o.reshape(w_mod.shape) for o in adamw(
        g_w_mod.reshape(1, L * D, wm_n), w_mod.reshape(L * D, wm_n), m_w_mod.reshape(L * D, wm_n),
        v_w_mod.reshape(L * D, wm_n), "adamw_w_mod")]

    res["ffn_w_up"], (landed["w_in"][0],) = adamw_layers(landed["ffn_w_up"], ffn_w_up, m_ffn_w_up, v_ffn_w_up,
                                                         "adamw_ffn_w_up", Comm("exchange", [g_in_prev]))
    res["w_in"] = adamw_layers(landed["w_in"], w_in, m_w_in, v_w_in, "adamw_w_in")
    res["w_out"] = adamw_layers(landed["w_out"], w_out, m_w_out, v_w_out, "adamw_w_out")
    res["ffn_w_down"] = adamw_layers(landed["ffn_w_down"], ffn_w_down, m_ffn_w_down, v_ffn_w_down, "adamw_ffn_w_down")

    order = ["c_ctx", "w_mod", "b_mod", "norm1_g", "norm2_g", "w_in", "ret_decay_f", "ret_decay_b", "conv_dw_w",
             "conv_dw_b", "conv_ln_g", "conv_ln_b", "w_out", "ffn_w_up", "ffn_dw_w", "ffn_dw_b", "ffn_w_down",
             "final_norm_g"]
    return (loss, grad_x, *[res[nme][0] for nme in order], *[res[nme][1] for nme in order],
            *[res[nme][2] for nme in order], *[res[nme][3] for nme in order])
```

```python
import functools

import jax
import jax.numpy as jnp
from jax import lax
from jax.experimental import pallas as pl
from jax.experimental.pallas import tpu as pltpu

F32 = jnp.float32
BF16 = jnp.bfloat16
EPS = 1e-6
N_DEV = 8
ROW_TILE = 256
RET_CHUNK = 128
GRID_W = 64
CONV_K = 31
CONV_HALO = 16
CONV_ROWS = 32
FFN_COL_TILES = (1408, 512, 256, 128)
FFN_TAP_ROWS = 64
FFN_HALO = 128
ROPE_THETA = 10000.0
ADAM_LR = 0.001
ADAM_B1 = 0.9
ADAM_B2 = 0.999
ADAM_EPS = 1e-08
ADAM_WD = 0.01
ADAM_STEP = 10
VMEM_LIMIT = 56 * 1024 * 1024
ADAM_VMEM_BYTES = 24 * 1024 * 1024
MESH = pl.DeviceIdType.MESH
ANY = pl.BlockSpec(memory_space=pl.ANY)


def _pick(n, cands):
    for t in cands:
        if n % t == 0:
            return t
    return n


def _sigmoid(z):
    return 1.0 / (1.0 + jnp.exp(-z))


def _my_rank():
    return 4 * lax.axis_index("x") + 2 * lax.axis_index("y") + lax.axis_index("c")


def _peer(j):
    x, y, c = lax.axis_index("x"), lax.axis_index("y"), lax.axis_index("c")
    px = 1 - x if j & 4 else x
    py = 1 - y if j & 2 else y
    pc = 1 - c if j & 1 else c
    return (px, py, pc), 4 * px + 2 * py + pc


class Comm:
    def __init__(self, kind, arrs):
        assert kind in ("gather", "exchange")
        self.kind, self.arrs, self.n = kind, list(arrs), len(arrs)
        self.in_specs = [ANY] * self.n
        self.out_specs = [ANY] * self.n
        lead = (N_DEV,) if kind == "gather" else ()
        self.out_shape = [jax.ShapeDtypeStruct(lead + a.shape, a.dtype) for a in self.arrs]
        per = self.n * (N_DEV - 1)
        self.scratch = [pltpu.SemaphoreType.DMA((per,)), pltpu.SemaphoreType.DMA((per,)),
                        pltpu.SemaphoreType.DMA((self.n,))]

    def _src(self, ref, rank):
        return ref if self.kind == "gather" else ref.at[rank]

    def _local(self, ins, outs, sems, a):
        me = _my_rank()
        return pltpu.make_async_copy(self._src(ins[a], me), outs[a].at[me], sems[2].at[a])

    def _remote(self, ins, outs, sems, a, j, receive):
        dev, rank = _peer(j)
        s = a * (N_DEV - 1) + j - 1
        slot = rank if receive else _my_rank()
        return pltpu.make_async_remote_copy(src_ref=self._src(ins[a], rank), dst_ref=outs[a].at[slot],
                                            send_sem=sems[0].at[s], recv_sem=sems[1].at[s],
                                            device_id=dev, device_id_type=MESH)

    def _gather_copy(self, ins, outs, sems, a, idx, src_slot, to):
        me = _my_rank()
        slot = me if src_slot is None else src_slot
        src = ins[a] if src_slot is None else outs[a].at[src_slot]
        s = a * (N_DEV - 1) + idx
        return pltpu.make_async_remote_copy(src_ref=src, dst_ref=outs[a].at[slot], send_sem=sems[0].at[s],
                                            recv_sem=sems[1].at[s], device_id=to, device_id_type=MESH)

    def _gather_plan(self):
        x, y, c = lax.axis_index("x"), lax.axis_index("y"), lax.axis_index("c")
        rank = lambda px, py, pc: 4 * px + 2 * py + pc
        chips = [(1 - x, y), (x, 1 - y), (1 - x, 1 - y)]
        return (x, y, 1 - c), rank(x, y, 1 - c), [((px, py, c), rank(px, py, c), rank(px, py, 1 - c)) for px, py in chips]

    def start(self, ins, outs, sems):
        for a in range(self.n):
            self._local(ins, outs, sems, a).start()
        if self.kind == "gather":
            sibling, _, chips = self._gather_plan()
            for a in range(self.n):
                self._gather_copy(ins, outs, sems, a, 0, None, sibling).start()
                for k, (dev, _, _) in enumerate(chips):
                    self._gather_copy(ins, outs, sems, a, 1 + k, None, dev).start()
            return
        for a in range(self.n):
            for j in range(1, N_DEV):
                self._remote(ins, outs, sems, a, j, False).start()

    def wait(self, ins, outs, sems):
        if self.kind == "gather":
            sibling, sib_rank, chips = self._gather_plan()
            for a in range(self.n):
                for k, (dev, slot, _) in enumerate(chips):
                    self._gather_copy(ins, outs, sems, a, 1 + k, slot, dev).wait_recv()
                    self._gather_copy(ins, outs, sems, a, 4 + k, slot, sibling).start()
            for a in range(self.n):
                self._gather_copy(ins, outs, sems, a, 0, sib_rank, sibling).wait_recv()
                for k, (_, _, sib_slot) in enumerate(chips):
                    self._gather_copy(ins, outs, sems, a, 4 + k, sib_slot, sibling).wait_recv()
                self._gather_copy(ins, outs, sems, a, 0, None, sibling).wait_send()
                for k, (dev, slot, _) in enumerate(chips):
                    self._gather_copy(ins, outs, sems, a, 1 + k, None, dev).wait_send()
                    self._gather_copy(ins, outs, sems, a, 4 + k, slot, sibling).wait_send()
        else:
            for a in range(self.n):
                for j in range(1, N_DEV):
                    cp = self._remote(ins, outs, sems, a, j, True)
                    cp.wait_recv()
                    cp.wait_send()
        for a in range(self.n):
            self._local(ins, outs, sems, a).wait()


def _call(compute, *, name, grid, in_specs, out_specs, out_shape, operands, sem, scratch=(), comm=None):
    n_in, n_out, n_sc = len(in_specs), len(out_specs), len(scratch)
    k = comm.n if comm else 0

    def body(*refs):
        ins, cin = refs[:n_in], refs[n_in:n_in + k]
        o0 = n_in + k
        outs, cout = refs[o0:o0 + n_out], refs[o0 + n_out:o0 + n_out + k]
        s0 = o0 + n_out + k
        sc, sems = refs[s0:s0 + n_sc], refs[s0 + n_sc:]
        if comm:
            ids = [pl.program_id(d) for d in range(len(grid))]
            first = functools.reduce(jnp.logical_and, [i == 0 for i in ids])
            last = functools.reduce(jnp.logical_and, [i == g - 1 for i, g in zip(ids, grid)])

            @pl.when(first)
            def _():
                comm.start(cin, cout, sems)

        compute(*ins, *outs, *sc)

        if comm:
            @pl.when(last)
            def _():
                comm.wait(cin, cout, sems)

    semantics = ("arbitrary",) * len(grid) if comm else sem
    res = pl.pallas_call(
        body, name=name, grid=grid,
        in_specs=list(in_specs) + (comm.in_specs if comm else []),
        out_specs=list(out_specs) + (comm.out_specs if comm else []),
        out_shape=list(out_shape) + (comm.out_shape if comm else []),
        scratch_shapes=list(scratch) + (comm.scratch if comm else []),
        compiler_params=pltpu.CompilerParams(dimension_semantics=semantics, vmem_limit_bytes=VMEM_LIMIT),
    )(*operands, *(comm.arrs if comm else []))
    if comm:
        return list(res[:n_out]), list(res[n_out:])
    return list(res)


def run_comm(kind, arrs, name):
    comm = Comm(kind, arrs)

    def body(*refs):
        ins, outs, sems = refs[:comm.n], refs[comm.n:2 * comm.n], refs[2 * comm.n:]
        comm.start(ins, outs, sems)
        comm.wait(ins, outs, sems)

    return list(pl.pallas_call(body, name=name, in_specs=comm.in_specs, out_specs=comm.out_specs,
                               out_shape=comm.out_shape, scratch_shapes=comm.scratch)(*comm.arrs))


M_TILES = (768, 512, 256, 128)
WIDE_TILES = (2048, 1408, 1024, 768, 512, 256, 128)
MID_TILES = (1408, 1024, 768, 512, 256, 128)


def _mm_body(dot, n_steps, axis):
    if n_steps == 1:
        def compute(a_ref, b_ref, o_ref):
            o_ref[...] = dot(a_ref, b_ref).astype(o_ref.dtype).reshape(o_ref.shape)
        return compute, []

    def compute(a_ref, b_ref, o_ref, acc):
        k = pl.program_id(axis)

        @pl.when(k == 0)
        def _():
            acc[...] = jnp.zeros_like(acc)

        acc[...] += dot(a_ref, b_ref)

        @pl.when(k == n_steps - 1)
        def _():
            o_ref[...] = acc[...].astype(o_ref.dtype).reshape(o_ref.shape)

    return compute, None


def mm_nn(a, b3, out_dtype, name, comm=None):
    M, K = a.shape
    R, _, n = b3.shape
    tm, tk, tn = _pick(M, M_TILES), _pick(K, WIDE_TILES), _pick(n, MID_TILES)
    nb, nk = n // tn, K // tk
    compute, scratch = _mm_body(lambda a_ref, b_ref: jnp.dot(a_ref[...], b_ref[0], preferred_element_type=F32), nk, 2)
    return _call(
        compute, name=name, grid=(M // tm, R * nb, nk),
        in_specs=[pl.BlockSpec((tm, tk), lambda i, j, k: (i, k)),
                  pl.BlockSpec((1, tk, tn), lambda i, j, k: (j // nb, k, j % nb))],
        out_specs=[pl.BlockSpec((tm, tn), lambda i, j, k: (i, j))],
        out_shape=[jax.ShapeDtypeStruct((M, R * n), out_dtype)],
        scratch=scratch if scratch is not None else [pltpu.VMEM((tm, tn), F32)],
        operands=(a, b3), sem=("parallel", "parallel", "arbitrary"), comm=comm)


def mm_nn_res(a, b3, x, gate2, n_ctx_rows, name, comm=None):
    M, K = a.shape
    R, _, n = b3.shape
    tm, tk, tn = _pick(M, M_TILES), _pick(K, WIDE_TILES), _pick(n, MID_TILES)
    nb, nk = n // tn, K // tk

    def compute(a_ref, b_ref, x_ref, g_ref, y_ref, xo_ref, *acc):
        k = pl.program_id(2)

        def finish(y):
            row = lax.broadcasted_iota(jnp.int32, y.shape, 0) + pl.program_id(0) * tm
            gate = jnp.where(row < n_ctx_rows, g_ref[0:1, :], g_ref[1:2, :])
            y_ref[...] = y.astype(y_ref.dtype)
            xo_ref[...] = x_ref[...] + gate * y

        part = jnp.dot(a_ref[...], b_ref[0], preferred_element_type=F32)
        if nk == 1:
            finish(part)
            return

        @pl.when(k == 0)
        def _():
            acc[0][...] = jnp.zeros_like(acc[0])

        acc[0][...] += part

        @pl.when(k == nk - 1)
        def _():
            finish(acc[0][...])

    tile = pl.BlockSpec((tm, tn), lambda i, j, k: (i, j))
    return _call(
        compute, name=name, grid=(M // tm, R * nb, nk),
        in_specs=[pl.BlockSpec((tm, tk), lambda i, j, k: (i, k)),
                  pl.BlockSpec((1, tk, tn), lambda i, j, k: (j // nb, k, j % nb)),
                  tile, pl.BlockSpec((2, tn), lambda i, j, k: (0, j))],
        out_specs=[tile, tile],
        out_shape=[jax.ShapeDtypeStruct((M, R * n), BF16), jax.ShapeDtypeStruct((M, R * n), F32)],
        scratch=[] if nk == 1 else [pltpu.VMEM((tm, tn), F32)],
        operands=(a, b3, x, gate2), sem=("parallel", "parallel", "arbitrary"), comm=comm)


def _piece_specs(n_pieces, per, rows, t, row_of, col_of):
    if n_pieces == 1:
        return [pl.BlockSpec((rows, t), lambda *ids: (row_of(*ids), col_of(*ids)))]

    def index(q, *ids):
        local = col_of(*ids) - q * per
        inside = jnp.logical_and(local >= 0, local < per)
        return jnp.where(inside, row_of(*ids), 0), jnp.where(inside, local, 0)

    return [pl.BlockSpec((rows, t), functools.partial(index, q)) for q in range(n_pieces)]


def _for_piece(n_pieces, per, block, fn):
    if n_pieces == 1:
        fn(0)
        return
    for q in range(n_pieces):
        pl.when(block // per == q)(functools.partial(fn, q))


def mm_nt(a, b3, out_dtype, name, comm=None):
    pieces = list(a) if isinstance(a, (list, tuple)) else [a]
    P = len(pieces)
    M, W = pieces[0].shape
    R, K, n = b3.shape
    assert P * W == R * n
    tm, tko, tc = _pick(M, M_TILES), _pick(K, WIDE_TILES), _pick(n, WIDE_TILES)
    ncb = n // tc
    nc = R * ncb
    per = W // tc

    def compute(*refs):
        a_refs, b_ref, o_ref, acc = refs[:P], refs[P], refs[P + 1], refs[P + 2]
        k = pl.program_id(2)

        @pl.when(k == 0)
        def _():
            acc[...] = jnp.zeros_like(acc)

        def add(q):
            acc[...] += lax.dot_general(a_refs[q][...], b_ref[0], (((1,), (1,)), ((), ())),
                                        preferred_element_type=F32)

        _for_piece(P, per, k, add)

        @pl.when(k == nc - 1)
        def _():
            o_ref[...] = acc[...].astype(o_ref.dtype)

    return _call(
        compute, name=name, grid=(M // tm, K // tko, nc),
        in_specs=_piece_specs(P, per, tm, tc, lambda i, j, k: i, lambda i, j, k: k)
        + [pl.BlockSpec((1, tko, tc), lambda i, j, k: (k // ncb, j, k % ncb))],
        out_specs=[pl.BlockSpec((tm, tko), lambda i, j, k: (i, j))],
        out_shape=[jax.ShapeDtypeStruct((M, K), out_dtype)],
        scratch=[pltpu.VMEM((tm, tko), F32)],
        operands=(*pieces, b3), sem=("parallel", "parallel", "arbitrary"), comm=comm)


def mm_tn(a, b, R, out_dtype, name):
    pieces = list(b) if isinstance(b, (list, tuple)) else [b]
    P = len(pieces)
    M, K = a.shape
    W = pieces[0].shape[1]
    n = P * W // R
    tm, tk, tn = _pick(M, (1408,) + M_TILES), _pick(K, MID_TILES), _pick(n, MID_TILES)
    nb, nm = n // tn, M // tm
    per = W // tn

    def compute(*refs):
        a_ref, b_refs, o_ref, acc = refs[0], refs[1:1 + P], refs[1 + P], refs[2 + P]
        m = pl.program_id(2)

        @pl.when(m == 0)
        def _():
            acc[...] = jnp.zeros_like(acc)

        def add(q):
            acc[...] += lax.dot_general(a_ref[...], b_refs[q][...], (((0,), (0,)), ((), ())),
                                        preferred_element_type=F32)

        _for_piece(P, per, pl.program_id(1), add)

        @pl.when(m == nm - 1)
        def _():
            o_ref[0] = acc[...].astype(o_ref.dtype)

    return _call(
        compute, name=name, grid=(K // tk, R * nb, nm),
        in_specs=[pl.BlockSpec((tm, tk), lambda i, j, m: (m, i))]
        + _piece_specs(P, per, tm, tn, lambda i, j, m: m, lambda i, j, m: j),
        out_specs=[pl.BlockSpec((1, tk, tn), lambda i, j, m: (j // nb, i, j % nb))],
        out_shape=[jax.ShapeDtypeStruct((R, K, n), out_dtype)],
        scratch=[pltpu.VMEM((tk, tn), F32)],
        operands=(a, *pieces), sem=("parallel", "parallel", "arbitrary"))[0]


def _seg_spec(D):
    return pl.BlockSpec((None, 1, D), lambda i: (jnp.minimum(i, 1), 0, 0))


def _seg3(a):
    return a.reshape(2, 1, a.shape[-1])


def _row_spec(w, col=0):
    return pl.BlockSpec((ROW_TILE, w), lambda i: (i, col))


def _acc_spec(r, w):
    return pl.BlockSpec((r, w), lambda i: (0, 0))


def _seg_accumulate(ref, i, val):
    ref[0:1, :] += jnp.where(i == 0, val, 0.0)
    ref[1:2, :] += jnp.where(i == 0, 0.0, val)


def rms_mod_fwd(x, g, shift2, scale2, name):
    T, D = x.shape

    def compute(x_ref, g_ref, sh_ref, sc_ref, h_ref):
        xv = x_ref[...]
        rstd = lax.rsqrt(jnp.mean(xv * xv, axis=-1, keepdims=True) + EPS)
        h = (xv * rstd * g_ref[...]) * (1.0 + sc_ref[...]) + sh_ref[...]
        h_ref[...] = h.astype(h_ref.dtype)

    return _call(compute, name=name, grid=(T // ROW_TILE,),
                 in_specs=[_row_spec(D), _acc_spec(1, D), _seg_spec(D), _seg_spec(D)],
                 out_specs=[_row_spec(D)], out_shape=[jax.ShapeDtypeStruct((T, D), BF16)],
                 operands=(x, g, _seg3(shift2), _seg3(scale2)), sem=("parallel",))[0]


def rms_mod_bwd(x, g, scale2, dh, dres, name, below=None):
    T, D = x.shape

    def compute(x_ref, g_ref, sc_ref, dh_ref, dres_ref, *rest):
        if below is None:
            dx_ref, dg_ref, dsh_ref, dsc_ref = rest
        else:
            y_ref, gate_ref, dx_ref, dg_ref, dsh_ref, dsc_ref, dy_ref, dgate_ref = rest
        i = pl.program_id(0)

        @pl.when(i == 0)
        def _():
            dg_ref[...] = jnp.zeros_like(dg_ref)
            dsh_ref[...] = jnp.zeros_like(dsh_ref)
            dsc_ref[...] = jnp.zeros_like(dsc_ref)
            if below is not None:
                dgate_ref[...] = jnp.zeros_like(dgate_ref)

        xv = x_ref[...]
        dh = dh_ref[...].astype(F32)
        gv = g_ref[...]
        rstd = lax.rsqrt(jnp.mean(xv * xv, axis=-1, keepdims=True) + EPS)
        xh = xv * rstd
        u = dh * (1.0 + sc_ref[...])
        dg_ref[...] += jnp.sum(u * xh, axis=0, keepdims=True)
        _seg_accumulate(dsh_ref, i, jnp.sum(dh, axis=0, keepdims=True))
        _seg_accumulate(dsc_ref, i, jnp.sum(dh * xh * gv, axis=0, keepdims=True))
        dxh = u * gv
        dx = rstd * (dxh - xh * jnp.mean(dxh * xh, axis=-1, keepdims=True))
        dxo = dres_ref[...] + dx
        dx_ref[...] = dxo
        if below is not None:
            dy_ref[...] = (dxo * gate_ref[...]).astype(dy_ref.dtype)
            _seg_accumulate(dgate_ref, i, jnp.sum(dxo * y_ref[...].astype(F32), axis=0, keepdims=True))

    in_specs = [_row_spec(D), _acc_spec(1, D), _seg_spec(D), _row_spec(D), _row_spec(D)]
    out_specs = [_row_spec(D), _acc_spec(1, D), _acc_spec(2, D), _acc_spec(2, D)]
    out_shape = [jax.ShapeDtypeStruct((T, D), F32), jax.ShapeDtypeStruct((1, D), F32),
                 jax.ShapeDtypeStruct((2, D), F32), jax.ShapeDtypeStruct((2, D), F32)]
    operands = (x, g, _seg3(scale2), dh, dres)
    if below is not None:
        in_specs += [_row_spec(D), _seg_spec(D)]
        out_specs += [_row_spec(D), _acc_spec(2, D)]
        out_shape += [jax.ShapeDtypeStruct((T, D), BF16), jax.ShapeDtypeStruct((2, D), F32)]
        operands += (below[0], _seg3(below[1]))
    return _call(compute, name=name, grid=(T // ROW_TILE,), in_specs=in_specs, out_specs=out_specs,
                 out_shape=out_shape, operands=operands, sem=("arbitrary",))


def gate_bwd(dxo, y, gate2, name):
    T, D = dxo.shape

    def compute(d_ref, y_ref, g_ref, dy_ref, dg_ref):
        i = pl.program_id(0)

        @pl.when(i == 0)
        def _():
            dg_ref[...] = jnp.zeros_like(dg_ref)

        d = d_ref[...]
        dy_ref[...] = (d * g_ref[...]).astype(dy_ref.dtype)
        _seg_accumulate(dg_ref, i, jnp.sum(d * y_ref[...].astype(F32), axis=0, keepdims=True))

    return _call(compute, name=name, grid=(T // ROW_TILE,),
                 in_specs=[_row_spec(D), _row_spec(D), _seg_spec(D)],
                 out_specs=[_row_spec(D), _acc_spec(2, D)],
                 out_shape=[jax.ShapeDtypeStruct((T, D), BF16), jax.ShapeDtypeStruct((2, D), F32)],
                 operands=(dxo, y, _seg3(gate2)), sem=("arbitrary",))


def final_loss(x, g, target, name):
    T, D = x.shape

    def compute(x_ref, g_ref, t_ref, dx_ref, dg_ref, loss_ref):
        i = pl.program_id(0)

        @pl.when(i == 0)
        def _():
            dg_ref[...] = jnp.zeros_like(dg_ref)
            loss_ref[...] = jnp.zeros_like(loss_ref)
            dx_ref[...] = jnp.zeros_like(dx_ref)

        @pl.when(i > 0)
        def _():
            xv = x_ref[...]
            gv = g_ref[...]
            rstd = lax.rsqrt(jnp.mean(xv * xv, axis=-1, keepdims=True) + EPS)
            xh = xv * rstd
            err = xh * gv - t_ref[...]
            loss_ref[...] += 0.5 * jnp.sum(jnp.mean(err * err, axis=-1, keepdims=True))
            dy = err * (1.0 / D)
            dg_ref[...] += jnp.sum(dy * xh, axis=0, keepdims=True)
            dxh = dy * gv
            dx_ref[...] = rstd * (dxh - xh * jnp.mean(dxh * xh, axis=-1, keepdims=True))

    return _call(compute, name=name, grid=(T // ROW_TILE,),
                 in_specs=[_row_spec(D), _acc_spec(1, D),
                           pl.BlockSpec((ROW_TILE, D), lambda i: (jnp.maximum(i - 1, 0), 0))],
                 out_specs=[_row_spec(D), _acc_spec(1, D), _acc_spec(8, 128)],
                 out_shape=[jax.ShapeDtypeStruct((T, D), F32), jax.ShapeDtypeStruct((1, D), F32),
                            jax.ShapeDtypeStruct((8, 128), F32)],
                 operands=(x, g, target), sem=("arbitrary",))


def _swap32(v):
    lane = lax.broadcasted_iota(jnp.int32, v.shape, 1)
    return jnp.where((lane & 63) < 32, pltpu.roll(v, 96, 1), pltpu.roll(v, 32, 1))


def _rope(v, cos, sin):
    return v * cos + _swap32(v) * sin


def _rope_t(d, cos, sin):
    return d * cos + _swap32(d * sin)


def _chunk_of(step, n_chunks, n_ctx, rev):
    if not rev:
        return step
    return jnp.where(step < n_ctx, n_ctx - 1 - step, n_chunks + n_ctx - 1 - step)


def _dot_t0(a, b):
    return lax.dot_general(a, b, (((0,), (0,)), ((), ())), preferred_element_type=F32)


def _dot_t1(a, b):
    return lax.dot_general(a, b, (((1,), (1,)), ((), ())), preferred_element_type=F32)


def _dot(a, b):
    return jnp.dot(a, b, preferred_element_type=F32)


def ret_fwd(p, cos, sin, tabs, n_heads, n_ctx, rev, name, comm=None):
    T = p.shape[0]
    C = RET_CHUNK
    H = n_heads
    NC = T // C
    scale = C ** -0.5

    def compute(q_ref, k_ref, v_ref, cos_ref, sin_ref, dm_ref, qd_ref, kd_ref, cd_ref, o_ref, s_ref, S):
        t = pl.program_id(0)

        @pl.when(t == 0)
        def _():
            S[...] = jnp.zeros_like(S)

        cs, sn = cos_ref[...], sin_ref[...]
        for h in range(H):
            sl = slice(h * C, (h + 1) * C)
            q = _rope(q_ref[:, sl], cs, sn)
            k = _rope(k_ref[:, sl], cs, sn) * scale
            qb, kb, vb = q.astype(BF16), k.astype(BF16), v_ref[:, sl].astype(BF16)
            A = _dot_t1(qb, kb) * dm_ref[h]
            s_in = S[h]
            s_ref[h, 0] = s_in
            o_ref[:, sl] = _dot(A.astype(BF16), vb) + _dot(qb, s_in.astype(BF16)) * qd_ref[h]
            S[h] = s_in * cd_ref[h] + _dot_t0((k * kd_ref[h]).astype(BF16), vb)

    cmap = lambda t: _chunk_of(t, NC, n_ctx, rev)
    blk = lambda col: pl.BlockSpec((C, H * C), lambda t: (cmap(t), col))
    tab = pl.BlockSpec((C, C), lambda t: (cmap(t), 0))
    htab = lambda r: pl.BlockSpec((H, r, C), lambda t: (0, 0, 0))
    return _call(
        compute, name=name, grid=(NC,),
        in_specs=[blk(0), blk(1), blk(2), tab, tab, htab(C), htab(C), htab(C), htab(1)],
        out_specs=[blk(0), pl.BlockSpec((H, 1, C, C), lambda t: (0, cmap(t), 0, 0))],
        out_shape=[jax.ShapeDtypeStruct((T, H * C), F32), jax.ShapeDtypeStruct((H, NC, C, C), F32)],
        scratch=[pltpu.VMEM((H, C, C), F32)],
        operands=(p, p, p, cos, sin, tabs["dm"], tabs["qd"], tabs["kd"], tabs["cd"]),
        sem=("arbitrary",), comm=comm)


def ret_bwd(p, cos, sin, tabs, do, s_saved, n_heads, n_ctx, rev, name, other=None, comm=None):
    T = p.shape[0]
    C = RET_CHUNK
    H = n_heads
    RW = H * C
    NC = T // C
    scale = C ** -0.5
    n_other = 0 if other is None else 3

    def compute(q_ref, k_ref, v_ref, cos_ref, sin_ref, dm_ref, qd_ref, kd_ref, cd_ref, em_ref, eq_ref,
                ek_ref, do_ref, s_ref, *rest):
        prev, (*grads, dlg_ref, dS) = rest[:n_other], rest[n_other:]
        t = pl.program_id(0)

        @pl.when(t == 0)
        def _():
            dS[...] = jnp.zeros_like(dS)
            dlg_ref[...] = jnp.zeros_like(dlg_ref)

        cs, sn = cos_ref[...], sin_ref[...]
        for h in range(H):
            sl = slice(h * C, (h + 1) * C)
            q = _rope(q_ref[:, sl], cs, sn)
            k = _rope(k_ref[:, sl], cs, sn) * scale
            qb, kb, vb = q.astype(BF16), k.astype(BF16), v_ref[:, sl].astype(BF16)
            dmv, qdv, kdv, cdv = dm_ref[h], qd_ref[h], kd_ref[h], cd_ref[h]
            A = _dot_t1(qb, kb) * dmv
            s_in = s_ref[h, 0]
            sb = s_in.astype(BF16)
            ds_out = dS[h]
            dsb = ds_out.astype(BF16)
            dov = do_ref[:, sl]
            dob = dov.astype(BF16)
            dA = _dot_t1(dob, vb)
            dPb = (dA * dmv).astype(BF16)
            doq = dov * qdv
            doqb = doq.astype(BF16)
            kk = k * kdv
            vds = _dot_t1(vb, dsb)
            dq = _dot(dPb, kb) + _dot_t1(doqb, sb)
            dk = _dot_t0(dPb, qb) + vds * kdv
            dv = _dot_t0(A.astype(BF16), dob) + _dot(kk.astype(BF16), dsb)
            if other is None:
                grads[0][:, sl], grads[1][:, sl], grads[2][:, sl] = dq, dk, dv
            else:
                out_ref = grads[0]
                to = lambda block: slice(block * RW + h * C, block * RW + (h + 1) * C)
                out_ref[:, to(0)] = _rope_t(prev[0][:, sl] + dq, cs, sn).astype(out_ref.dtype)
                out_ref[:, to(1)] = _rope_t((prev[1][:, sl] + dk) * scale, cs, sn).astype(out_ref.dtype)
                out_ref[:, to(2)] = (prev[2][:, sl] + dv).astype(out_ref.dtype)
            dS[h] = ds_out * cdv + _dot_t0(qb, doqb)
            o2 = _dot(qb, sb)
            part = (jnp.sum(dA * A * em_ref[...], axis=0, keepdims=True)
                    + jnp.sum(eq_ref[...] * doq * o2, axis=0, keepdims=True)
                    + jnp.sum(ek_ref[...] * kk * vds, axis=0, keepdims=True)
                    + float(C) * cdv * jnp.sum(s_in * ds_out, axis=0, keepdims=True))
            dlg_ref[h, 0:1, :] += part

    cmap = lambda t: _chunk_of(NC - 1 - t, NC, n_ctx, rev)
    blk = lambda col: pl.BlockSpec((C, H * C), lambda t: (cmap(t), col))
    tab = pl.BlockSpec((C, C), lambda t: (cmap(t), 0))
    const = pl.BlockSpec((C, C), lambda t: (0, 0))
    htab = lambda r: pl.BlockSpec((H, r, C), lambda t: (0, 0, 0))
    if other is None:
        grad_specs = [blk(0), blk(0), blk(0)]
        grad_shapes = [jax.ShapeDtypeStruct((T, RW), F32)] * 3
    else:
        grad_specs = [pl.BlockSpec((C, 3 * RW), lambda t: (cmap(t), 0))]
        grad_shapes = [jax.ShapeDtypeStruct((T, 3 * RW), BF16)]
    return _call(
        compute, name=name, grid=(NC,),
        in_specs=[blk(0), blk(1), blk(2), tab, tab, htab(C), htab(C), htab(C), htab(1), const, const,
                  const, blk(0), pl.BlockSpec((H, 1, C, C), lambda t: (0, cmap(t), 0, 0))] + [blk(0)] * n_other,
        out_specs=grad_specs + [pl.BlockSpec((H, 8, C), lambda t: (0, 0, 0))],
        out_shape=grad_shapes + [jax.ShapeDtypeStruct((H, 8, C), F32)],
        scratch=[pltpu.VMEM((H, C, C), F32)],
        operands=(p, p, p, cos, sin, tabs["dm"], tabs["qd"], tabs["kd"], tabs["cd"], tabs["em"], tabs["eq"],
                  tabs["ek"], do, s_saved) + (() if other is None else tuple(other)),
        sem=("arbitrary",), comm=comm)


def ret_out_fwd(o_f, o_b, p, n_heads, name):
    T, RW = o_f.shape
    C = RET_CHUNK

    def compute(of_ref, ob_ref, g_ref, out_ref):
        for h in range(n_heads):
            sl = slice(h * C, (h + 1) * C)
            o = of_ref[:, sl] + ob_ref[:, sl]
            r = o * lax.rsqrt(jnp.mean(o * o, axis=-1, keepdims=True) + EPS)
            g = g_ref[:, sl]
            out_ref[:, sl] = (g * _sigmoid(g) * r).astype(out_ref.dtype)

    return _call(compute, name=name, grid=(T // ROW_TILE,),
                 in_specs=[_row_spec(RW), _row_spec(RW), _row_spec(RW, 3)],
                 out_specs=[_row_spec(RW)], out_shape=[jax.ShapeDtypeStruct((T, RW), BF16)],
                 operands=(o_f, o_b, p), sem=("parallel",))[0]


def ret_out_bwd(o_f, o_b, p, dmix, n_heads, name):
    T, RW = o_f.shape
    C = RET_CHUNK

    def compute(of_ref, ob_ref, g_ref, d_ref, do_ref, dg_ref):
        for h in range(n_heads):
            sl = slice(h * C, (h + 1) * C)
            o = of_ref[:, sl] + ob_ref[:, sl]
            rstd = lax.rsqrt(jnp.mean(o * o, axis=-1, keepdims=True) + EPS)
            r = o * rstd
            g = g_ref[:, sl]
            sg = _sigmoid(g)
            d = d_ref[:, sl].astype(F32)
            dg_ref[:, sl] = (d * r * sg * (1.0 + g * (1.0 - sg))).astype(dg_ref.dtype)
            dr = d * g * sg
            do_ref[:, sl] = rstd * (dr - r * jnp.mean(dr * r, axis=-1, keepdims=True))

    return _call(compute, name=name, grid=(T // ROW_TILE,),
                 in_specs=[_row_spec(RW), _row_spec(RW), _row_spec(RW, 3), _row_spec(RW, 0)],
                 out_specs=[_row_spec(RW), _row_spec(RW)],
                 out_shape=[jax.ShapeDtypeStruct((T, RW), F32), jax.ShapeDtypeStruct((T, RW), BF16)],
                 operands=(o_f, o_b, p, dmix), sem=("parallel",))


def _halo_specs(width, col, halo, n_rows):
    per = ROW_TILE // halo
    last = n_rows // halo - 1
    return [pl.BlockSpec((halo, width), lambda i: (jnp.maximum(i * per - 1, 0), col)),
            pl.BlockSpec((ROW_TILE, width), lambda i: (i, col)),
            pl.BlockSpec((halo, width), lambda i: (jnp.minimum((i + 1) * per, last), col))]


def _halo_valid(i, n_tiles):
    return i >= 2, jnp.logical_and(i >= 1, i <= n_tiles - 2)


def _shifted_copies(S):
    base = S[0]
    rows = base.shape[0]
    for s in range(1, 8):
        S[s] = pltpu.roll(base, rows - s, 0)


def _shifted_rows(S, start, n, cols=slice(None)):
    s = start % 8
    return S[s, start - s:start - s + n, cols]


def _fma_groups(acc, w_row, window):
    w8 = jnp.broadcast_to(w_row, acc[0].shape)
    return [a + w8 * window(g) for g, a in enumerate(acc)]


def _fold8(v):
    parts = [v[r:r + 8] for r in range(0, v.shape[0], 8)]
    while len(parts) > 1:
        parts = [parts[k] + parts[k + 1] for k in range(0, len(parts) - 1, 2)] + (parts[-1:] if len(parts) % 2 else [])
    return parts[0]


def conv_fwd(p, w, bias, ln_g, ln_b, name, comm=None):
    T = p.shape[0]
    CW = w.shape[1]
    NT = T // ROW_TILE
    HL = CONV_HALO
    PAD = CONV_K // 2

    def compute(ap, ac, an, bp, bc, bn, w_ref, b_ref, g_ref, be_ref, u2_ref, out_ref, US):
        i = pl.program_id(0)
        vp, vn = _halo_valid(i, NT)
        US[0, 0:HL, :] = jnp.where(vp, ap[...] * _sigmoid(bp[...]), 0.0)
        US[0, HL:HL + ROW_TILE, :] = ac[...] * _sigmoid(bc[...])
        US[0, HL + ROW_TILE:, :] = jnp.where(vn, an[...] * _sigmoid(bn[...]), 0.0)
        _shifted_copies(US)
        for r0 in range(0, ROW_TILE, CONV_ROWS):
            acc = [jnp.zeros((8, CW), F32) + b_ref[...]] * (CONV_ROWS // 8)
            for j in range(CONV_K):
                acc = _fma_groups(acc, w_ref[j:j + 1, :],
                                  lambda g: _shifted_rows(US, HL - PAD + j + r0 + 8 * g, 8))
            acc = jnp.concatenate(acc, axis=0)
            u2_ref[r0:r0 + CONV_ROWS, :] = acc
            mu = jnp.mean(acc, axis=-1, keepdims=True)
            xc = acc - mu
            rstd = lax.rsqrt(jnp.mean(xc * xc, axis=-1, keepdims=True) + EPS)
            ln = xc * rstd * g_ref[...] + be_ref[...]
            out_ref[r0:r0 + CONV_ROWS, :] = (ln * _sigmoid(ln)).astype(out_ref.dtype)

    vec = _acc_spec(1, CW)
    return _call(compute, name=name, grid=(NT,),
                 in_specs=_halo_specs(CW, 4, HL, T) + _halo_specs(CW, 5, HL, T) + [_acc_spec(CONV_K, CW), vec, vec, vec],
                 out_specs=[_row_spec(CW), _row_spec(CW)],
                 out_shape=[jax.ShapeDtypeStruct((T, CW), F32), jax.ShapeDtypeStruct((T, CW), BF16)],
                 scratch=[pltpu.VMEM((8, ROW_TILE + 2 * HL, CW), F32)],
                 operands=(p, p, p, p, p, p, w, bias, ln_g, ln_b), sem=("parallel",), comm=comm)


def conv_bwd_ln(u2, dmix, ln_g, ln_b, name):
    T, CW = u2.shape

    def compute(u_ref, d_ref, g_ref, be_ref, du_ref, dg_ref, db_ref, dbias_ref):
        i = pl.program_id(0)

        @pl.when(i == 0)
        def _():
            dg_ref[...] = jnp.zeros_like(dg_ref)
            db_ref[...] = jnp.zeros_like(db_ref)
            dbias_ref[...] = jnp.zeros_like(dbias_ref)

        u = u_ref[...]
        gv = g_ref[...]
        mu = jnp.mean(u, axis=-1, keepdims=True)
        xc = u - mu
        rstd = lax.rsqrt(jnp.mean(xc * xc, axis=-1, keepdims=True) + EPS)
        xh = xc * rstd
        ln = xh * gv + be_ref[...]
        sg = _sigmoid(ln)
        dln = d_ref[...].astype(F32) * sg * (1.0 + ln * (1.0 - sg))
        dg_ref[...] += jnp.sum(dln * xh, axis=0, keepdims=True)
        db_ref[...] += jnp.sum(dln, axis=0, keepdims=True)
        dxh = dln * gv
        du = rstd * (dxh - jnp.mean(dxh, axis=-1, keepdims=True)
                     - xh * jnp.mean(dxh * xh, axis=-1, keepdims=True))
        du_ref[...] = du
        dbias_ref[...] += jnp.sum(du, axis=0, keepdims=True)

    vec = _acc_spec(1, CW)
    return _call(compute, name=name, grid=(T // ROW_TILE,),
                 in_specs=[_row_spec(CW), _row_spec(CW, 1), vec, vec],
                 out_specs=[_row_spec(CW), vec, vec, vec],
                 out_shape=[jax.ShapeDtypeStruct((T, CW), F32)] + [jax.ShapeDtypeStruct((1, CW), F32)] * 3,
                 operands=(u2, dmix, ln_g, ln_b), sem=("arbitrary",))


def conv_bwd_taps(p, du2, w, lead, name):
    T = p.shape[0]
    CW = w.shape[1]
    RW = lead.shape[1]
    NT = T // ROW_TILE
    HL = CONV_HALO
    PAD = CONV_K // 2

    def compute(ap, ac, an, bp, bc, bn, dp, dc, dn, w_ref, lead_ref, out_ref, dw_ref, US, DUS):
        i = pl.program_id(0)

        @pl.when(i == 0)
        def _():
            dw_ref[...] = jnp.zeros_like(dw_ref)

        out_ref[:, 0:RW] = lead_ref[...]

        vp, vn = _halo_valid(i, NT)
        US[0, 0:HL, :] = jnp.where(vp, ap[...] * _sigmoid(bp[...]), 0.0)
        US[0, HL:HL + ROW_TILE, :] = ac[...] * _sigmoid(bc[...])
        US[0, HL + ROW_TILE:, :] = jnp.where(vn, an[...] * _sigmoid(bn[...]), 0.0)
        DUS[0, 0:HL, :] = jnp.where(vp, dp[...], 0.0)
        DUS[0, HL:HL + ROW_TILE, :] = dc[...]
        DUS[0, HL + ROW_TILE:, :] = jnp.where(vn, dn[...], 0.0)
        _shifted_copies(US)
        _shifted_copies(DUS)
        for r0 in range(0, ROW_TILE, CONV_ROWS):
            du = [jnp.zeros((8, CW), F32)] * (CONV_ROWS // 8)
            for j in range(CONV_K):
                du = _fma_groups(du, w_ref[j:j + 1, :],
                                 lambda g: _shifted_rows(DUS, HL + PAD - j + r0 + 8 * g, 8))
            du = jnp.concatenate(du, axis=0)
            a = ac[r0:r0 + CONV_ROWS, :]
            sg = _sigmoid(bc[r0:r0 + CONV_ROWS, :])
            out_ref[r0:r0 + CONV_ROWS, RW:RW + CW] = (du * sg).astype(out_ref.dtype)
            out_ref[r0:r0 + CONV_ROWS, RW + CW:] = (du * a * sg * (1.0 - sg)).astype(out_ref.dtype)
        for c0 in range(0, CW, 128):
            cols = slice(c0, c0 + 128)
            accs = [jnp.zeros((8, 128), F32)] * CONV_K
            for r0 in range(0, ROW_TILE, CONV_ROWS):
                d = DUS[0, HL + r0:HL + r0 + CONV_ROWS, cols]
                accs = [acc + _fold8(d * _shifted_rows(US, HL - PAD + j + r0, CONV_ROWS, cols))
                        for j, acc in enumerate(accs)]
            for j in range(CONV_K):
                dw_ref[j:j + 1, cols] += jnp.sum(accs[j], axis=0, keepdims=True)

    return _call(compute, name=name, grid=(NT,),
                 in_specs=(_halo_specs(CW, 4, HL, T) + _halo_specs(CW, 5, HL, T) + _halo_specs(CW, 0, HL, T)
                           + [_acc_spec(CONV_K, CW), _row_spec(RW)]),
                 out_specs=[_row_spec(RW + 2 * CW), _acc_spec(CONV_K, CW)],
                 out_shape=[jax.ShapeDtypeStruct((T, RW + 2 * CW), BF16), jax.ShapeDtypeStruct((CONV_K, CW), F32)],
                 scratch=[pltpu.VMEM((8, ROW_TILE + 2 * HL, CW), F32)] * 2,
                 operands=(p, p, p, p, p, p, du2, du2, du2, w, lead), sem=("arbitrary",))


def _ffn_halo_specs(tc, col0, n_rows):
    per = ROW_TILE // FFN_HALO
    last = n_rows // FFN_HALO - 1
    return [pl.BlockSpec((FFN_HALO, tc), lambda cb, i: (jnp.maximum(i * per - 1, 0), col0 + cb)),
            pl.BlockSpec((ROW_TILE, tc), lambda cb, i: (i, col0 + cb)),
            pl.BlockSpec((FFN_HALO, tc), lambda cb, i: (jnp.minimum((i + 1) * per, last), col0 + cb))]


def _ffn_fill(S, prev, cur, nxt, i, n_tiles):
    vp, vn = _halo_valid(i, n_tiles)
    HL = FFN_HALO
    S[1, 0:HL, :] = jnp.where(vp, prev[...].astype(F32), 0.0)
    S[1, HL:HL + ROW_TILE, :] = cur[...].astype(F32)
    S[1, HL + ROW_TILE:, :] = jnp.where(vn, nxt[...].astype(F32), 0.0)
    lo, hi = HL - GRID_W, HL + ROW_TILE + GRID_W
    win = S[1, lo - 8:hi + 8, :]
    rows = win.shape[0]
    is_lat = i >= 1
    col = (lax.broadcasted_iota(jnp.int32, win.shape, 0) + (lo - 8)) & (GRID_W - 1)
    S[0, lo:hi, :] = jnp.where(jnp.logical_and(is_lat, col == 0), 0.0, pltpu.roll(win, 1, 0))[8:rows - 8]
    S[2, lo:hi, :] = jnp.where(jnp.logical_and(is_lat, col == GRID_W - 1), 0.0, pltpu.roll(win, rows - 1, 0))[8:rows - 8]


def _ffn_acc_rows(tc):
    return max(8, min(64, (32 // (tc // 128)) * 8))


def _ffn_row_factor(i, di):
    return 1.0 if di == 1 else jnp.where(i >= 1, 1.0, 0.0)


def ffn_conv_fwd(up, w9, bias, name):
    T = up.shape[0]
    DFF = w9.shape[1]
    tc = _pick(DFF, FFN_COL_TILES)
    ncb = DFF // tc
    NT = T // ROW_TILE
    HL = FFN_HALO

    def compute(gp, gc, gn, val_ref, w_ref, b_ref, cg_ref, act_ref, G):
        i = pl.program_id(1)
        _ffn_fill(G, gp, gc, gn, i, NT)
        rows = _ffn_acc_rows(tc)
        for r0 in range(0, ROW_TILE, rows):
            acc = [jnp.zeros((8, tc), F32) + b_ref[...]] * (rows // 8)
            for di in range(3):
                for dj in range(3):
                    wt = w_ref[3 * di + dj:3 * di + dj + 1, :] * _ffn_row_factor(i, di)
                    lo = HL + r0 + (di - 1) * GRID_W
                    acc = _fma_groups(acc, wt, lambda g: G[dj, lo + 8 * g:lo + 8 * g + 8, :])
            acc = jnp.concatenate(acc, axis=0)
            cg_ref[r0:r0 + rows, :] = acc.astype(cg_ref.dtype)
            val = val_ref[r0:r0 + rows, :].astype(F32)
            act_ref[r0:r0 + rows, :] = (acc * _sigmoid(acc) * val).astype(act_ref.dtype)

    tile = pl.BlockSpec((ROW_TILE, tc), lambda cb, i: (i, cb))
    return _call(compute, name=name, grid=(ncb, NT),
                 in_specs=_ffn_halo_specs(tc, 0, T) + [pl.BlockSpec((ROW_TILE, tc), lambda cb, i: (i, ncb + cb)),
                                                       pl.BlockSpec((9, tc), lambda cb, i: (0, cb)),
                                                       pl.BlockSpec((1, tc), lambda cb, i: (0, cb))],
                 out_specs=[tile, tile],
                 out_shape=[jax.ShapeDtypeStruct((T, DFF), BF16), jax.ShapeDtypeStruct((T, DFF), BF16)],
                 scratch=[pltpu.VMEM((3, ROW_TILE + 2 * HL, tc), F32)],
                 operands=(up, up, up, up, w9, bias), sem=("parallel", "parallel"))


def ffn_conv_bwd_act(cg, up, dact, name):
    T, DFF = cg.shape
    tc = _pick(DFF, FFN_COL_TILES)
    ncb = DFF // tc

    def compute(cg_ref, val_ref, d_ref, dcg_ref, dval_ref, db_ref):
        i = pl.program_id(1)

        @pl.when(i == 0)
        def _():
            db_ref[...] = jnp.zeros_like(db_ref)

        c = cg_ref[...].astype(F32)
        sg = _sigmoid(c)
        d = d_ref[...].astype(F32)
        dval_ref[...] = (d * c * sg).astype(dval_ref.dtype)
        dcg = d * val_ref[...].astype(F32) * sg * (1.0 + c * (1.0 - sg))
        dcg_ref[...] = dcg.astype(dcg_ref.dtype)
        db_ref[...] += jnp.sum(dcg, axis=0, keepdims=True)

    tile = pl.BlockSpec((ROW_TILE, tc), lambda cb, i: (i, cb))
    return _call(compute, name=name, grid=(ncb, T // ROW_TILE),
                 in_specs=[tile, pl.BlockSpec((ROW_TILE, tc), lambda cb, i: (i, ncb + cb)), tile],
                 out_specs=[tile, tile, pl.BlockSpec((1, tc), lambda cb, i: (0, cb))],
                 out_shape=[jax.ShapeDtypeStruct((T, DFF), BF16), jax.ShapeDtypeStruct((T, DFF), BF16),
                            jax.ShapeDtypeStruct((1, DFF), F32)],
                 operands=(cg, up, dact), sem=("parallel", "arbitrary"))


def ffn_conv_bwd_taps(up, dcg, w9, name, comm=None):
    T, DFF = dcg.shape
    tc = _pick(DFF, FFN_COL_TILES)
    ncb = DFF // tc
    NT = T // ROW_TILE
    HL = FFN_HALO

    def compute(gp, gc, gn, dp, dc, dn, w_ref, dgate_ref, dw_ref, G, DC):
        i = pl.program_id(1)

        @pl.when(i == 0)
        def _():
            dw_ref[...] = jnp.zeros_like(dw_ref)

        _ffn_fill(G, gp, gc, gn, i, NT)
        _ffn_fill(DC, dp, dc, dn, i, NT)
        rows = _ffn_acc_rows(tc)
        for r0 in range(0, ROW_TILE, rows):
            dg = [jnp.zeros((8, tc), F32)] * (rows // 8)
            for di in range(3):
                for dj in range(3):
                    wt = w_ref[3 * di + dj:3 * di + dj + 1, :] * _ffn_row_factor(i, di)
                    lo = HL + r0 - (di - 1) * GRID_W
                    dg = _fma_groups(dg, wt, lambda g: DC[2 - dj, lo + 8 * g:lo + 8 * g + 8, :])
            dgate_ref[r0:r0 + rows, :] = jnp.concatenate(dg, axis=0).astype(dgate_ref.dtype)
        for c0 in range(0, tc, 128):
            cols = slice(c0, c0 + 128)
            accs = [jnp.zeros((8, 128), F32)] * 9
            for r0 in range(0, ROW_TILE, FFN_TAP_ROWS):
                d = DC[1, HL + r0:HL + r0 + FFN_TAP_ROWS, cols]
                for di in range(3):
                    lo = HL + r0 + (di - 1) * GRID_W
                    for dj in range(3):
                        accs[3 * di + dj] = accs[3 * di + dj] + _fold8(d * G[dj, lo:lo + FFN_TAP_ROWS, cols])
            for di in range(3):
                for dj in range(3):
                    t = 3 * di + dj
                    dw_ref[t:t + 1, cols] += _ffn_row_factor(i, di) * jnp.sum(accs[t], axis=0, keepdims=True)

    tile = pl.BlockSpec((ROW_TILE, tc), lambda cb, i: (i, cb))
    return _call(compute, name=name, grid=(ncb, NT),
                 in_specs=_ffn_halo_specs(tc, 0, T) + _ffn_halo_specs(tc, 0, T) + [pl.BlockSpec((9, tc), lambda cb, i: (0, cb))],
                 out_specs=[tile, pl.BlockSpec((9, tc), lambda cb, i: (0, cb))],
                 out_shape=[jax.ShapeDtypeStruct((T, DFF), BF16), jax.ShapeDtypeStruct((9, DFF), F32)],
                 scratch=[pltpu.VMEM((3, ROW_TILE + 2 * HL, tc), F32)] * 2,
                 operands=(up, up, up, dcg, dcg, dcg, w9), sem=("parallel", "arbitrary"), comm=comm)


def _adamw_update(g, w_ref, m_ref, v_ref, g_ref, d_ref, nm_ref, nv_ref):
    c1 = 1.0 - ADAM_B1 ** ADAM_STEP
    c2 = 1.0 - ADAM_B2 ** ADAM_STEP
    nm = ADAM_B1 * m_ref[...] + (1.0 - ADAM_B1) * g
    nv = ADAM_B2 * v_ref[...] + (1.0 - ADAM_B2) * (g * g)
    g_ref[...] = g
    nm_ref[...] = nm
    nv_ref[...] = nv
    d_ref[...] = -ADAM_LR * ((nm / c1) / (jnp.sqrt(nv / c2) + ADAM_EPS) + ADAM_WD * w_ref[...])


def _sum_parts(p_ref, P):
    g = p_ref[0].astype(F32)
    for k in range(1, P):
        g = g + p_ref[k].astype(F32)
    return g


def adamw(parts, w, m, v, name):
    P, R, C = parts.shape
    fits = lambda t: 2 * (P + 7) * t * C * 4 <= ADAM_VMEM_BYTES
    tr = R if fits(R) else _pick(R, [t for t in (1024, 512, 256, 128, 64, 32, 16, 8) if fits(t)])

    def compute(p_ref, w_ref, m_ref, v_ref, g_ref, d_ref, nm_ref, nv_ref):
        _adamw_update(_sum_parts(p_ref, P), w_ref, m_ref, v_ref, g_ref, d_ref, nm_ref, nv_ref)

    tile = pl.BlockSpec((tr, C), lambda i: (i, 0))
    return _call(compute, name=name, grid=(R // tr,),
                 in_specs=[pl.BlockSpec((P, tr, C), lambda i: (0, i, 0)), tile, tile, tile],
                 out_specs=[tile] * 4, out_shape=[jax.ShapeDtypeStruct((R, C), F32)] * 4,
                 operands=(parts, w, m, v), sem=("parallel",))


def adamw_layers(parts_l, w, m, v, name, comm=None):
    L = len(parts_l)
    P, R, C = parts_l[0].shape
    psize = parts_l[0].dtype.itemsize
    fits = lambda t: 2 * t * C * (L * P * psize + 7 * 4) <= ADAM_VMEM_BYTES
    tr = _pick(R, [t for t in (1024, 512, 256, 128, 64, 32, 16, 8) if fits(t)])

    def compute(*refs):
        p_refs, (w_ref, m_ref, v_ref), outs = refs[:L], refs[L:L + 3], refs[L + 3:]
        for l in range(L):
            @pl.when(pl.program_id(0) == l)
            def _(l=l):
                _adamw_update(_sum_parts(p_refs[l], P), w_ref, m_ref, v_ref, *outs)

    tile = pl.BlockSpec((None, tr, C), lambda l, i: (l, i, 0))
    part = lambda k: pl.BlockSpec((P, tr, C), lambda l, i: (0, jnp.where(l == k, i, 0), 0))
    return _call(compute, name=name, grid=(L, R // tr),
                 in_specs=[part(k) for k in range(L)] + [tile] * 3,
                 out_specs=[tile] * 4, out_shape=[jax.ShapeDtypeStruct((L, R, C), F32)] * 4,
                 operands=(*parts_l, w, m, v), sem=("arbitrary", "arbitrary"), comm=comm)


def _rope_tables(seq, ctx):
    t = jnp.arange(seq)
    quarter = RET_CHUNK // 4
    inv_freq = 1.0 / (ROPE_THETA ** (jnp.arange(0, quarter, dtype=F32) / quarter))
    ang_r = (t // GRID_W).astype(F32)[:, None] * inv_freq[None, :]
    ang_c = (t % GRID_W).astype(F32)[:, None] * inv_freq[None, :]
    cr, sr, cc, sc = jnp.cos(ang_r), jnp.sin(ang_r), jnp.cos(ang_c), jnp.sin(ang_c)
    cos = jnp.concatenate([cr, cr, cc, cc], axis=-1)
    sin = jnp.concatenate([-sr, sr, -sc, sc], axis=-1)
    cos = jnp.concatenate([jnp.ones((ctx, RET_CHUNK), F32), cos], axis=0)
    sin = jnp.concatenate([jnp.zeros((ctx, RET_CHUNK), F32), sin], axis=0)
    return cos, sin


def _decay_tables(decay_logit, rev):
    C = RET_CHUNK
    lg = jax.nn.log_sigmoid(decay_logit.astype(F32))
    idx = jnp.arange(C, dtype=F32)
    diff = idx[:, None] - idx[None, :]
    if rev:
        diff = -diff
        eq, ek = C - idx, idx
    else:
        eq, ek = idx + 1.0, C - 1.0 - idx
    keep = diff >= 0
    em = jnp.where(keep, diff, 0.0)
    bc = lambda e: jnp.broadcast_to(e[:, None], (C, C))
    return {
        "dm": jnp.where(keep[None], jnp.exp(lg[:, None, None] * em[None]), 0.0),
        "qd": jnp.broadcast_to(jnp.exp(lg[:, None] * eq[None, :])[:, :, None], (lg.shape[0], C, C)),
        "kd": jnp.broadcast_to(jnp.exp(lg[:, None] * ek[None, :])[:, :, None], (lg.shape[0], C, C)),
        "cd": jnp.broadcast_to(jnp.exp(lg * C)[:, None, None], (lg.shape[0], 1, C)),
        "em": em, "eq": bc(eq), "ek": bc(ek),
    }


def _silu(z):
    return z * jax.nn.sigmoid(z)


def _dsilu(z):
    s = jax.nn.sigmoid(z)
    return s * (1.0 + z * (1.0 - s))


def kernel(x, c, ctx, c_ctx, w_mod, b_mod, norm1_g, norm2_g, w_in, ret_decay_f, ret_decay_b, conv_dw_w, conv_dw_b, conv_ln_g, conv_ln_b, w_out, ffn_w_up, ffn_dw_w, ffn_dw_b, ffn_w_down, final_norm_g, loss_target, m_c_ctx, m_w_mod, m_b_mod, m_norm1_g, m_norm2_g, m_w_in, m_ret_decay_f, m_ret_decay_b, m_conv_dw_w, m_conv_dw_b, m_conv_ln_g, m_conv_ln_b, m_w_out, m_ffn_w_up, m_ffn_dw_w, m_ffn_dw_b, m_ffn_w_down, m_final_norm_g, v_c_ctx, v_w_mod, v_b_mod, v_norm1_g, v_norm2_g, v_w_in, v_ret_decay_f, v_ret_decay_b, v_conv_dw_w, v_conv_dw_b, v_conv_ln_g, v_conv_ln_b, v_w_out, v_ffn_w_up, v_ffn_dw_w, v_ffn_dw_b, v_ffn_w_down, v_final_norm_g):
    L, D, _ = w_mod.shape
    SEQ, CTX = x.shape[1], ctx.shape[1]
    T = SEQ + CTX
    RW = D // 2
    CW = D - RW
    H = RW // RET_CHUNK
    DFF = ffn_dw_b.shape[1]
    NMOD = b_mod.shape[1] // D
    n_ctx = CTX // RET_CHUNK
    assert CTX == ROW_TILE and RW == CW and SEQ % ROW_TILE == 0 and NMOD == 6
    me = _my_rank()
    wm_n = w_mod.shape[2]
    wo_k, wd_k = w_out.shape[1], ffn_w_down.shape[1]
    cw_n, fw_n = conv_dw_w.shape[2], ffn_dw_w.shape[3]

    w_mod_b = w_mod.astype(BF16)
    w_in_b, w_out_b, w_up_b, w_down_b = (a.astype(BF16) for a in (w_in, w_out, ffn_w_up, ffn_w_down))
    as_rows = lambda g: g.reshape(1, -1, D)
    g_in, g_out, g_up, g_down, g_cw, g_fw, g_c = run_comm(
        "gather", [w_in_b[0], w_out_b[0], w_up_b[0], w_down_b[0], conv_dw_w, ffn_dw_w, _silu(c)], "gather_first")
    w_in_l, w_up_l, w_out_l, w_down_l = [g_in], [g_up], [as_rows(g_out)], [as_rows(g_down)]
    conv_w_l = [jnp.moveaxis(g_cw[:, l], 0, 1).reshape(CONV_K, CW) for l in range(L)]
    ffn_w9_l = [jnp.moveaxis(g_fw[:, l], 0, 2).reshape(9, DFF) for l in range(L)]

    s_cond = jnp.concatenate([g_c.reshape(N_DEV, D), jnp.broadcast_to(_silu(c_ctx)[None], (N_DEV, D))], axis=0)
    s_cond_b = s_cond.astype(BF16)
    mod_shard = mm_nn(s_cond_b, w_mod_b, F32, "mod_fwd")[0]
    (g_mod,) = run_comm("gather", [mod_shard], "gather_mod")
    mod_all = jnp.transpose(g_mod.reshape(N_DEV, 2 * N_DEV, L, wm_n), (2, 1, 0, 3)).reshape(L, 2 * N_DEV, NMOD * D)
    mod_all = mod_all + b_mod[:, None, :]
    mod_lat = lax.dynamic_index_in_dim(mod_all, me, axis=1, keepdims=False)
    mod_ctx = mod_all[:, N_DEV]
    mod2 = jnp.stack([mod_ctx, mod_lat], axis=1).reshape(L, 2, NMOD, D)

    cos, sin = _rope_tables(SEQ, CTX)
    xs = jnp.concatenate([ctx[0], x[0]], axis=0)

    saved = []
    for l in range(L):
        nxt = l + 1 < L
        sh1, sc1, g1, sh2, sc2, g2 = (mod2[l, :, k] for k in range(NMOD))
        tf = _decay_tables(ret_decay_f[l], False)
        tb = _decay_tables(ret_decay_b[l], True)
        h = rms_mod_fwd(xs, norm1_g[l][None], sh1, sc1, "norm1_fwd")
        if nxt:
            (p,), (gi, go) = mm_nn(h, w_in_l[l], F32, "in_proj_g", Comm("gather", [w_in_b[l + 1], w_out_b[l + 1]]))
            w_in_l.append(gi)
            w_out_l.append(as_rows(go))
        else:
            (p,) = mm_nn(h, w_in_l[l], F32, "in_proj")
        o_f, s_f = ret_fwd(p, cos, sin, tf, H, n_ctx, False, "ret_fwd_f")
        o_b, s_b = ret_fwd(p, cos, sin, tb, H, n_ctx, True, "ret_fwd_b")
        mix_r = ret_out_fwd(o_f, o_b, p, H, "ret_out_fwd")
        u2, mix_c = conv_fwd(p, conv_w_l[l], conv_dw_b[l][None], conv_ln_g[l][None], conv_ln_b[l][None], "conv_fwd")
        mix = jnp.concatenate([mix_r, mix_c], axis=1)
        y1, x2 = mm_nn_res(mix, w_out_l[l], xs, g1, CTX, "out_proj")
        h2 = rms_mod_fwd(x2, norm2_g[l][None], sh2, sc2, "norm2_fwd")
        if nxt:
            (up,), (gu,) = mm_nn(h2, w_up_l[l], BF16, "ffn_up_g", Comm("gather", [w_up_b[l + 1]]))
            w_up_l.append(gu)
        else:
            (up,) = mm_nn(h2, w_up_l[l], BF16, "ffn_up")
        cg, act = ffn_conv_fwd(up, ffn_w9_l[l], ffn_dw_b[l][None], "ffn_conv_fwd")
        if nxt:
            (y2, x3), (gd,) = mm_nn_res(act, w_down_l[l], x2, g2, CTX, "ffn_down_g", Comm("gather", [w_down_b[l + 1]]))
            w_down_l.append(as_rows(gd))
        else:
            y2, x3 = mm_nn_res(act, w_down_l[l], x2, g2, CTX, "ffn_down")
        saved.append(dict(x1=xs, h=h, p=p, o_f=o_f, o_b=o_b, s_f=s_f, s_b=s_b, u2=u2, mix=mix, y1=y1, x2=x2,
                          h2=h2, up=up, cg=cg, act=act, y2=y2, tf=tf, tb=tb))
        xs = x3

    dxs, d_final_g, loss_part = final_loss(xs, final_norm_g[None], loss_target[0], "final_loss")
    loss = lax.psum(loss_part[0, 0], ("x", "y", "c"))

    landed = {n: [None] * L for n in ("w_in", "w_out", "ffn_w_up", "ffn_w_down")}
    small = {n: [None] * L for n in ("norm1_g", "norm2_g", "ret_decay_f", "ret_decay_b", "conv_dw_w", "conv_dw_b",
                                     "conv_ln_g", "conv_ln_b", "ffn_dw_w", "ffn_dw_b")}
    dmod2 = [None] * L
    g_in_prev = g_up_prev = None
    for l in reversed(range(L)):
        sv = saved[l]
        sh1, sc1, g1, sh2, sc2, g2 = (mod2[l, :, k] for k in range(NMOD))
        if l == L - 1:
            dy2, dg2 = gate_bwd(dxs, sv["y2"], g2, "res2_bwd")
        if g_in_prev is not None:
            (dact,), (landed["w_in"][l + 1],) = mm_nt(dy2, w_down_l[l], BF16, "ffn_down_dx_x", Comm("exchange", [g_in_prev]))
        else:
            (dact,) = mm_nt(dy2, w_down_l[l], BF16, "ffn_down_dx")
        g_down = mm_tn(sv["act"], dy2, 1, BF16, "ffn_down_dw").reshape(N_DEV, wd_k, D)
        dcg, dval, small["ffn_dw_b"][l] = ffn_conv_bwd_act(sv["cg"], sv["up"], dact, "ffn_conv_bwd_act")
        (dgate, small["ffn_dw_w"][l]), (landed["ffn_w_down"][l],) = ffn_conv_bwd_taps(
            sv["up"], dcg, ffn_w9_l[l], "ffn_conv_bwd_taps", Comm("exchange", [g_down]))
        dup = [dgate, dval]
        if g_up_prev is not None:
            (dh2,), (landed["ffn_w_up"][l + 1],) = mm_nt(dup, w_up_l[l], F32, "ffn_up_dx_x", Comm("exchange", [g_up_prev]))
        else:
            (dh2,) = mm_nt(dup, w_up_l[l], F32, "ffn_up_dx")
        g_up_prev = mm_tn(sv["h2"], dup, N_DEV, BF16, "ffn_up_dw")
        dx2, small["norm2_g"][l], dsh2, dsc2, dy1, dg1 = rms_mod_bwd(
            sv["x2"], norm2_g[l][None], sc2, dh2, dxs, "norm2_bwd", below=(sv["y1"], g1))
        (dmix,) = mm_nt(dy1, w_out_l[l], BF16, "out_proj_dx")
        g_out = mm_tn(sv["mix"], dy1, 1, BF16, "out_proj_dw").reshape(N_DEV, wo_k, D)
        do, dgt = ret_out_bwd(sv["o_f"], sv["o_b"], sv["p"], dmix, H, "ret_out_bwd")
        dqf, dkf, dvf, dlg_f = ret_bwd(sv["p"], cos, sin, sv["tf"], do, sv["s_f"], H, n_ctx, False, "ret_bwd_f")
        (dqkv, dlg_b), (landed["w_out"][l],) = ret_bwd(
            sv["p"], cos, sin, sv["tb"], do, sv["s_b"], H, n_ctx, True, "ret_bwd_b", other=(dqf, dkf, dvf),
            comm=Comm("exchange", [g_out]))
        small["ret_decay_f"][l] = jnp.sum(dlg_f[:, 0, :], axis=-1) * jax.nn.sigmoid(-ret_decay_f[l])
        small["ret_decay_b"][l] = jnp.sum(dlg_b[:, 0, :], axis=-1) * jax.nn.sigmoid(-ret_decay_b[l])
        du2, small["conv_ln_g"][l], small["conv_ln_b"][l], small["conv_dw_b"][l] = conv_bwd_ln(
            sv["u2"], dmix, conv_ln_g[l][None], conv_ln_b[l][None], "conv_bwd_ln")
        dgab, small["conv_dw_w"][l] = conv_bwd_taps(sv["p"], du2, conv_w_l[l], dgt, "conv_bwd_taps")
        dp = [dqkv, dgab]
        if l == 0:
            (dh,), (landed["ffn_w_up"][0],) = mm_nt(dp, w_in_l[l], F32, "in_proj_dx_x", Comm("exchange", [g_up_prev]))
        else:
            (dh,) = mm_nt(dp, w_in_l[l], F32, "in_proj_dx")
        g_in_prev = mm_tn(sv["h"], dp, N_DEV, BF16, "in_proj_dw")
        dmod_l = [None, None, dg1, dsh2, dsc2, dg2]
        if l > 0:
            dxs, small["norm1_g"][l], dmod_l[0], dmod_l[1], dy2, dg2 = rms_mod_bwd(
                sv["x1"], norm1_g[l][None], sc1, dh, dx2, "norm1_bwd_res", below=(saved[l - 1]["y2"], mod2[l - 1, :, 5]))
        else:
            dxs, small["norm1_g"][l], dmod_l[0], dmod_l[1] = rms_mod_bwd(
                sv["x1"], norm1_g[l][None], sc1, dh, dx2, "norm1_bwd")
        dmod2[l] = jnp.concatenate(dmod_l, axis=1)

    grad_x = dxs[CTX:][None]

    dmod2 = jnp.stack(dmod2)
    (g_dmod,) = run_comm("gather", [dmod2], "gather_dmod")
    dmod_all = jnp.concatenate([jnp.moveaxis(g_dmod[:, :, 1], 0, 1), jnp.moveaxis(g_dmod[:, :, 0], 0, 1)], axis=1)
    dmod_sh = lax.dynamic_slice_in_dim(dmod_all, me * wm_n, wm_n, axis=2)
    dmod_sh = jnp.moveaxis(dmod_sh, 0, 1).reshape(2 * N_DEV, L * wm_n).astype(BF16)
    g_w_mod = mm_tn(s_cond_b, dmod_sh, L, F32, "mod_dw")
    (d_cond,) = mm_nt(dmod_sh, w_mod_b, F32, "mod_dx")
    g_c_ctx_part = jnp.sum(d_cond[N_DEV:], axis=0) * _dsilu(c_ctx)
    g_b_mod_part = dmod2[:, 0] + dmod2[:, 1]

    pad128 = lambda a: jnp.pad(a.reshape(-1), (0, (-a.size) % 128))
    rep_names = ["c_ctx", "b_mod", "norm1_g", "norm2_g", "ret_decay_f", "ret_decay_b", "conv_dw_b", "conv_ln_g",
                 "conv_ln_b", "ffn_dw_b", "final_norm_g"]
    given = dict(c_ctx=(c_ctx, m_c_ctx, v_c_ctx), b_mod=(b_mod, m_b_mod, v_b_mod),
                 norm1_g=(norm1_g, m_norm1_g, v_norm1_g), norm2_g=(norm2_g, m_norm2_g, v_norm2_g),
                 ret_decay_f=(ret_decay_f, m_ret_decay_f, v_ret_decay_f),
                 ret_decay_b=(ret_decay_b, m_ret_decay_b, v_ret_decay_b),
                 conv_dw_b=(conv_dw_b, m_conv_dw_b, v_conv_dw_b), conv_ln_g=(conv_ln_g, m_conv_ln_g, v_conv_ln_g),
                 conv_ln_b=(conv_ln_b, m_conv_ln_b, v_conv_ln_b), ffn_dw_b=(ffn_dw_b, m_ffn_dw_b, v_ffn_dw_b),
                 final_norm_g=(final_norm_g, m_final_norm_g, v_final_norm_g))
    rep_part = dict(c_ctx=g_c_ctx_part, b_mod=g_b_mod_part, final_norm_g=d_final_g)
    for nme in rep_names:
        if nme not in rep_part:
            rep_part[nme] = jnp.stack([a.reshape(-1) for a in small[nme]])
    rep_sizes = [((-given[nme][0].size) % 128) + given[nme][0].size for nme in rep_names]
    n_rep = sum(rep_sizes)
    cw_part = jnp.stack(small["conv_dw_w"])
    fw_part = jnp.stack(small["ffn_dw_w"])
    packed = jnp.concatenate([pad128(rep_part[nme]) for nme in rep_names] + [cw_part.reshape(-1), fw_part.reshape(-1)])
    (g_small,) = run_comm("gather", [packed.reshape(-1, 128)], "gather_small")
    g_small = g_small.reshape(N_DEV, -1)
    rep_w, rep_m, rep_v = (jnp.concatenate([pad128(given[nme][k]) for nme in rep_names]).reshape(-1, 128) for k in range(3))
    rep_out = adamw(g_small[:, :n_rep].reshape(N_DEV, -1, 128), rep_w, rep_m, rep_v, "adamw_small")
    res = {}
    off = 0
    for nme, sz in zip(rep_names, rep_sizes):
        shape = given[nme][0].shape
        res[nme] = [o.reshape(-1)[off:off + given[nme][0].size].reshape(shape) for o in rep_out]
        off += sz

    cw_all = g_small[:, n_rep:n_rep + cw_part.size].reshape(N_DEV, L * CONV_K, CW)
    cw_mine = lax.dynamic_slice_in_dim(cw_all, me * cw_n, cw_n, axis=2)
    res["conv_dw_w"] = [o.reshape(conv_dw_w.shape) for o in adamw(
        cw_mine, conv_dw_w.reshape(L * CONV_K, cw_n), m_conv_dw_w.reshape(L * CONV_K, cw_n),
        v_conv_dw_w.reshape(L * CONV_K, cw_n), "adamw_conv_w")]
    fw_all = g_small[:, n_rep + cw_part.size:].reshape(N_DEV, L * 9, DFF)
    fw_mine = lax.dynamic_slice_in_dim(fw_all, me * fw_n, fw_n, axis=2)
    res["ffn_dw_w"] = [o.reshape(ffn_dw_w.shape) for o in adamw(
        fw_mine, ffn_dw_w.reshape(L * 9, fw_n), m_ffn_dw_w.reshape(L * 9, fw_n),
        v_ffn_dw_w.reshape(L * 9, fw_n), "adamw_ffn_w")]

    res["w_mod"] = [o.reshape(w_mod.shape) for o in adamw(
        g_w_mod.reshape(1, L * D, wm_n), w_mod.reshape(L * D, wm_n), m_w_mod.reshape(L * D, wm_n),
        v_w_mod.reshape(L * D, wm_n), "adamw_w_mod")]

    res["ffn_w_up"], (landed["w_in"][0],) = adamw_layers(landed["ffn_w_up"], ffn_w_up, m_ffn_w_up, v_ffn_w_up,
                                                         "adamw_ffn_w_up", Comm("exchange", [g_in_prev]))
    res["w_in"] = adamw_layers(landed["w_in"], w_in, m_w_in, v_w_in, "adamw_w_in")
    res["w_out"] = adamw_layers(landed["w_out"], w_out, m_w_out, v_w_out, "adamw_w_out")
    res["ffn_w_down"] = adamw_layers(landed["ffn_w_down"], ffn_w_down, m_ffn_w_down, v_ffn_w_down, "adamw_ffn_w_down")

    order = ["c_ctx", "w_mod", "b_mod", "norm1_g", "norm2_g", "w_in", "ret_decay_f", "ret_decay_b", "conv_dw_w",
             "conv_dw_b", "conv_ln_g", "conv_ln_b", "w_out", "ffn_w_up", "ffn_dw_w", "ffn_dw_b", "ffn_w_down",
             "final_norm_g"]
    return (loss, grad_x, *[res[nme][0] for nme in order], *[res[nme][1] for nme in order],
            *[res[nme][2] for nme in order], *[res[nme][3] for nme in order])
```
